```python
import math
import jax, jax.numpy as jnp
from jax import lax
import numpy as np


D_MODEL = 1024
BATCH = 32
SEQ = 2048
DEPTH = 2

PLE_DIM = 256
N_BRANCH = 3
SSD_HEADS = 16
SSD_HEAD_DIM = 64
SSD_INNER = SSD_HEADS * SSD_HEAD_DIM
SSD_GROUPS = 4
SSD_STATE = 128
SSD_CONV = 4
SSD_CHUNK = 128
SSD_XBC = SSD_INNER + 2 * SSD_GROUPS * SSD_STATE
CONF_CH = 1024
CONF_KERNEL = 31
ATT_Q_HEADS = 16
ATT_KV_HEADS = 4
ATT_HEAD_DIM = 64
ATT_WINDOW = 128
ATT_BLOCK = 128
ROPE_THETA = 10000.0
MOE_GROUPS = 4
MOE_EXPERTS_PER_GROUP = 8
MOE_EXPERTS = MOE_GROUPS * MOE_EXPERTS_PER_GROUP
MOE_TOP_K = 2
MOE_FF = 512
MOE_BLOCK = 256
DEEPNORM_ALPHA = (2 * DEPTH) ** 0.25
DEEPNORM_BETA = (8 * DEPTH) ** -0.25
LN_EPS = 1e-5
IN_WIDTHS = (N_BRANCH * D_MODEL, SSD_INNER, SSD_XBC, SSD_HEADS, 2 * CONF_CH,
             ATT_Q_HEADS * ATT_HEAD_DIM, ATT_KV_HEADS * ATT_HEAD_DIM, ATT_KV_HEADS * ATT_HEAD_DIM)
IN_WIDTH = sum(IN_WIDTHS)

kernel_name = 'hybrid_ssd_conformer_swa_hmoe_deepnorm'


def _split_points(widths):
    pts, acc = [], 0
    for w in widths[:-1]:
        acc += w
        pts.append(acc)
    return pts


def layer_norm(x, g, b):
    xf = x.astype(jnp.float32)
    mu = jnp.mean(xf, axis=-1, keepdims=True)
    var = jnp.mean(jnp.square(xf - mu), axis=-1, keepdims=True)
    return ((xf - mu) * lax.rsqrt(var + LN_EPS) * g + b).astype(x.dtype)


def causal_dwconv(x, w, b):
    width = w.shape[0]
    y = lax.conv_general_dilated(x, w[:, None, :].astype(x.dtype), window_strides=(1,),
                                 padding=[(width - 1, 0)],
                                 dimension_numbers=('NWC', 'WIO', 'NWC'),
                                 feature_group_count=x.shape[-1])
    return y + b


def rope(x, positions):
    half = x.shape[-1] // 2
    inv_freq = ROPE_THETA ** (-jnp.arange(half, dtype=jnp.float32) / half)
    ang = positions.astype(jnp.float32)[..., None] * inv_freq
    cos = jnp.cos(ang)[:, :, None, :]
    sin = jnp.sin(ang)[:, :, None, :]
    xf = x.astype(jnp.float32)
    x1, x2 = xf[..., :half], xf[..., half:]
    return jnp.concatenate([x1 * cos - x2 * sin, x2 * cos + x1 * sin], axis=-1).astype(x.dtype)


def ssd_mixer(z, xbc, dt_raw, conv_w, conv_b, dt_bias, a_log, d_skip, norm_w, w_proj):
    bsz, seq, _ = z.shape
    G, R, P, N, L = SSD_GROUPS, SSD_HEADS // SSD_GROUPS, SSD_HEAD_DIM, SSD_STATE, SSD_CHUNK
    nc = seq // L
    f32 = jnp.float32
    xbc = jax.nn.silu(causal_dwconv(xbc, conv_w, conv_b))
    xs, bm, cm = jnp.split(xbc, [SSD_INNER, SSD_INNER + G * N], axis=-1)
    dt = jax.nn.softplus(dt_raw.astype(f32) + dt_bias.astype(f32))
    a = -jnp.exp(a_log.astype(f32)).reshape(G, R)
    xh = xs.astype(f32).reshape(bsz, nc, L, G, R, P)
    dtc = dt.reshape(bsz, nc, L, G, R)
    bc = bm.astype(f32).reshape(bsz, nc, L, G, N)
    cc = cm.astype(f32).reshape(bsz, nc, L, G, N)
    xdt = xh * dtc[..., None]
    cs = jnp.moveaxis(jnp.cumsum(dtc * a, axis=2), 2, -1)
    causal = jnp.tril(jnp.ones((L, L), dtype=bool))
    seg = cs[..., :, None] - cs[..., None, :]
    decay_in = jnp.exp(jnp.where(causal, seg, -jnp.inf))
    cb = jnp.einsum('bclgn,bcsgn->bcgls', cc, bc)
    y_diag = jnp.einsum('bcgrls,bcsgrp->bclgrp', cb[:, :, :, None] * decay_in, xdt)
    decay_to_end = jnp.exp(cs[..., -1:] - cs)
    chunk_states = jnp.einsum('bclgn,bcgrl,bclgrp->bcgrpn', bc, decay_to_end, xdt)
    chunk_decay = jnp.exp(cs[..., -1])

    def step(h, inp):
        st, dec = inp
        return h * dec[..., None, None] + st, h

    h0 = jnp.zeros((bsz, G, R, P, N), f32)
    _, prev = lax.scan(step, h0, (jnp.moveaxis(chunk_states, 1, 0), jnp.moveaxis(chunk_decay, 1, 0)))
    prev = jnp.moveaxis(prev, 0, 1)
    y_off = jnp.einsum('bclgn,bcgrpn,bcgrl->bclgrp', cc, prev, jnp.exp(cs))
    y = y_diag + y_off + xh * d_skip.astype(f32).reshape(G, R)[..., None]
    y = y.reshape(bsz, seq, SSD_INNER) * jax.nn.silu(z.astype(f32))
    yg = y.reshape(bsz, seq, G, SSD_INNER // G)
    yg = yg * lax.rsqrt(jnp.mean(jnp.square(yg), axis=-1, keepdims=True) + LN_EPS)
    y = yg.reshape(bsz, seq, SSD_INNER) * norm_w.astype(f32)
    return y.astype(z.dtype) @ w_proj


def conformer_conv(u, dw_w, dw_b, ln_g, ln_b, w_proj):
    a, gate = jnp.split(u, 2, axis=-1)
    h = a * jax.nn.sigmoid(gate)
    h = causal_dwconv(h, dw_w, dw_b)
    h = jax.nn.silu(layer_norm(h, ln_g, ln_b))
    return h @ w_proj


def swa_sink_attention(q, k, v, positions, sinks, w_proj):
    bsz, seq, _ = q.shape
    R = ATT_Q_HEADS // ATT_KV_HEADS
    nb = seq // ATT_BLOCK
    q = rope(q.reshape(bsz, seq, ATT_Q_HEADS, ATT_HEAD_DIM), positions)
    k = rope(k.reshape(bsz, seq, ATT_KV_HEADS, ATT_HEAD_DIM), positions)
    v = v.reshape(bsz, seq, ATT_KV_HEADS, ATT_HEAD_DIM)
    qb = q.reshape(bsz, nb, ATT_BLOCK, ATT_KV_HEADS, R, ATT_HEAD_DIM)

    def band(t):
        tb = t.reshape(bsz, nb, ATT_BLOCK, ATT_KV_HEADS, ATT_HEAD_DIM)
        prev = jnp.pad(tb[:, :-1], ((0, 0), (1, 0), (0, 0), (0, 0), (0, 0)))
        return jnp.concatenate([prev, tb], axis=2)

    kw, vw = band(k), band(v)
    blk = jnp.arange(nb)[:, None, None]
    qi = jnp.arange(ATT_BLOCK)[None, :, None]
    kj = jnp.arange(2 * ATT_BLOCK)[None, None, :]
    rel = qi + ATT_BLOCK - kj
    mask = (rel >= 0) & (rel < ATT_WINDOW) & (blk * ATT_BLOCK - ATT_BLOCK + kj >= 0)
    scale = ATT_HEAD_DIM ** -0.5
    sink = sinks.astype(jnp.float32).reshape(ATT_KV_HEADS, R)[None, :, :, None, None]

    def block_attend(args):
        qblk, kblk, vblk, m = args
        s = jnp.einsum('bqkrd,bjkd->bkrqj', qblk, kblk).astype(jnp.float32) * scale
        s = jnp.where(m[None, None, None], s, -jnp.inf)
        mx = jnp.maximum(jnp.max(s, axis=-1, keepdims=True), sink)
        e = jnp.exp(s - mx)
        probs = e / (jnp.sum(e, axis=-1, keepdims=True) + jnp.exp(sink - mx))
        return jnp.einsum('bkrqj,bjkd->bqkrd', probs.astype(vblk.dtype), vblk)

    out = lax.map(block_attend, (jnp.moveaxis(qb, 1, 0), jnp.moveaxis(kw, 1, 0),
                                 jnp.moveaxis(vw, 1, 0), mask))
    out = jnp.moveaxis(out, 0, 1).reshape(bsz, seq, ATT_Q_HEADS * ATT_HEAD_DIM)
    return out @ w_proj


def hierarchical_moe(x, w_group, b_group, w_expert, b_expert, w_gate, w_up, w_down):
    bsz, seq, d = x.shape
    T = bsz * seq
    TK = T * MOE_TOP_K
    xt = x.reshape(T, d)
    tok = jnp.arange(T, dtype=jnp.int32)
    g_logits = (xt @ w_group + b_group).astype(jnp.float32)
    g_prob = jax.nn.softmax(g_logits, axis=-1)
    g_idx = jnp.argmax(g_logits, axis=-1).astype(jnp.int32)
    g_w = g_prob[tok, g_idx][:, None]
    e_logits = (xt @ w_expert + b_expert).astype(jnp.float32).reshape(T, MOE_GROUPS, MOE_EXPERTS_PER_GROUP)
    e_in = e_logits[tok, g_idx]
    top_v, top_i = lax.top_k(e_in, MOE_TOP_K)
    e_w = jax.nn.softmax(top_v, axis=-1) * g_w
    e_id = g_idx[:, None] * MOE_EXPERTS_PER_GROUP + top_i.astype(jnp.int32)
    flat_e = e_id.reshape(TK)
    flat_w = e_w.reshape(TK)
    flat_tok = jnp.repeat(tok, MOE_TOP_K)
    order = jnp.argsort(flat_e, stable=True)
    se, stok, sw = flat_e[order], flat_tok[order], flat_w[order]
    counts = jnp.bincount(flat_e, length=MOE_EXPERTS).astype(jnp.int32)
    starts = jnp.cumsum(counts) - counts
    padded = ((counts + MOE_BLOCK - 1) // MOE_BLOCK) * MOE_BLOCK
    pends = jnp.cumsum(padded)
    pstarts = pends - padded
    dest = pstarts[se] + jnp.arange(TK, dtype=jnp.int32) - starts[se]
    n_rows = TK + MOE_EXPERTS * MOE_BLOCK
    n_blocks = n_rows // MOE_BLOCK
    row_tok = jnp.full((n_rows,), T, dtype=jnp.int32).at[dest].set(stok)
    x_rows = jnp.concatenate([xt, jnp.zeros((1, d), xt.dtype)], axis=0)[row_tok]
    x_rows = x_rows.reshape(n_blocks, MOE_BLOCK, d)
    block_e = jnp.searchsorted(pends, jnp.arange(n_blocks, dtype=jnp.int32) * MOE_BLOCK, side='right')
    block_e = jnp.minimum(block_e, MOE_EXPERTS - 1).astype(jnp.int32)

    def expert_block(args):
        xb, e = args
        h = jax.nn.silu(xb @ w_gate[e]) * (xb @ w_up[e])
        return h @ w_down[e]

    y_rows = lax.map(expert_block, (x_rows, block_e)).reshape(n_rows, d)
    y = jax.ops.segment_sum(y_rows[dest] * sw[:, None].astype(x.dtype), stok, num_segments=T)
    return y.reshape(bsz, seq, d)


def decoder_layer(x, p_i, positions, w_in, b_gate, ssd_conv_w, ssd_conv_b, ssd_dt_bias, ssd_a_log,
                  ssd_d, ssd_norm_w, ssd_w_out, conf_dw_w, conf_dw_b, conf_ln_g, conf_ln_b,
                  conf_w_out, attn_sinks, attn_w_out, w_out, ln1_g, ln1_b, moe_w_group,
                  moe_b_group, moe_w_expert, moe_b_expert, moe_w_gate, moe_w_up, moe_w_down,
                  ln2_g, ln2_b, ple_w_gate, ple_w_proj):
    bsz, seq, d = x.shape
    h = x @ w_in
    gates_raw, z, xbc, dt_raw, u_conf, q, k, v = jnp.split(h, _split_points(IN_WIDTHS), axis=-1)
    gates = jax.nn.sigmoid(gates_raw.reshape(bsz, seq, N_BRANCH, d) + b_gate)
    y_ssd = ssd_mixer(z, xbc, dt_raw, ssd_conv_w, ssd_conv_b, ssd_dt_bias, ssd_a_log, ssd_d,
                      ssd_norm_w, ssd_w_out)
    y_conf = conformer_conv(u_conf, conf_dw_w, conf_dw_b, conf_ln_g, conf_ln_b, conf_w_out)
    y_att = swa_sink_attention(q, k, v, positions, attn_sinks, attn_w_out)
    mixed = gates[:, :, 0] * y_ssd + gates[:, :, 1] * y_conf + gates[:, :, 2] * y_att
    x = layer_norm(DEEPNORM_ALPHA * x + mixed @ w_out, ln1_g, ln1_b)
    ffn = hierarchical_moe(x, moe_w_group, moe_b_group, moe_w_expert, moe_b_expert,
                           moe_w_gate, moe_w_up, moe_w_down)
    x = layer_norm(DEEPNORM_ALPHA * x + ffn, ln2_g, ln2_b)
    return x + jax.nn.sigmoid(x @ ple_w_gate) * (p_i @ ple_w_proj)


def setup_inputs(seed: int = 0) -> dict:
    key = jax.random.key(seed)
    ks = jax.random.split(key, 40)
    f32 = jnp.float32
    Lr = DEPTH

    def nrm(k, shape, scale):
        return jax.random.normal(k, shape, f32) * scale

    dt0 = jnp.exp(jax.random.uniform(ks[8], (Lr, SSD_HEADS), f32, math.log(1e-3), math.log(1e-1)))
    return {
        'x': nrm(ks[0], (BATCH, SEQ, D_MODEL), 1.0),
        'p': nrm(ks[1], (DEPTH, BATCH, SEQ, PLE_DIM), 1.0),
        'positions': jnp.arange(SEQ, dtype=jnp.int32)[None, :]
                     + jax.random.randint(ks[2], (BATCH, 1), 0, SEQ, dtype=jnp.int32),
        'w_in': nrm(ks[3], (Lr, D_MODEL, IN_WIDTH), D_MODEL ** -0.5),
        'b_gate': nrm(ks[4], (Lr, N_BRANCH, D_MODEL), 0.01),
        'ssd_conv_w': nrm(ks[5], (Lr, SSD_CONV, SSD_XBC), SSD_CONV ** -0.5),
        'ssd_conv_b': nrm(ks[6], (Lr, SSD_XBC), 0.01),
        'ssd_dt_bias': dt0 + jnp.log(-jnp.expm1(-dt0)),
        'ssd_a_log': jnp.log(jax.random.uniform(ks[9], (Lr, SSD_HEADS), f32, 1.0, 16.0)),
        'ssd_d': 1.0 + nrm(ks[10], (Lr, SSD_HEADS), 0.1),
        'ssd_norm_w': 1.0 + nrm(ks[11], (Lr, SSD_INNER), 0.05),
        'ssd_w_out': nrm(ks[12], (Lr, SSD_INNER, D_MODEL), SSD_INNER ** -0.5),
        'conf_dw_w': nrm(ks[13], (Lr, CONF_KERNEL, CONF_CH), CONF_KERNEL ** -0.5),
        'conf_dw_b': nrm(ks[14], (Lr, CONF_CH), 0.01),
        'conf_ln_g': 1.0 + nrm(ks[15], (Lr, CONF_CH), 0.05),
        'conf_ln_b': nrm(ks[16], (Lr, CONF_CH), 0.01),
        'conf_w_out': nrm(ks[17], (Lr, CONF_CH, D_MODEL), CONF_CH ** -0.5),
        'attn_sinks': nrm(ks[18], (Lr, ATT_Q_HEADS), 0.5),
        'attn_w_out': nrm(ks[19], (Lr, ATT_Q_HEADS * ATT_HEAD_DIM, D_MODEL), (ATT_Q_HEADS * ATT_HEAD_DIM) ** -0.5),
        'w_out': nrm(ks[20], (Lr, D_MODEL, D_MODEL), DEEPNORM_BETA * D_MODEL ** -0.5),
        'ln1_g': 1.0 + nrm(ks[21], (Lr, D_MODEL), 0.05),
        'ln1_b': nrm(ks[22], (Lr, D_MODEL), 0.01),
        'moe_w_group': nrm(ks[23], (Lr, D_MODEL, MOE_GROUPS), D_MODEL ** -0.5),
        'moe_b_group': nrm(ks[24], (Lr, MOE_GROUPS), 0.01),
        'moe_w_expert': nrm(ks[25], (Lr, D_MODEL, MOE_EXPERTS), D_MODEL ** -0.5),
        'moe_b_expert': nrm(ks[26], (Lr, MOE_EXPERTS), 0.01),
        'moe_w_gate': nrm(ks[27], (Lr, MOE_EXPERTS, D_MODEL, MOE_FF), D_MODEL ** -0.5),
        'moe_w_up': nrm(ks[28], (Lr, MOE_EXPERTS, D_MODEL, MOE_FF), D_MODEL ** -0.5),
        'moe_w_down': nrm(ks[29], (Lr, MOE_EXPERTS, MOE_FF, D_MODEL), DEEPNORM_BETA * MOE_FF ** -0.5),
        'ln2_g': 1.0 + nrm(ks[30], (Lr, D_MODEL), 0.05),
        'ln2_b': nrm(ks[31], (Lr, D_MODEL), 0.01),
        'ple_w_gate': nrm(ks[32], (Lr, D_MODEL, D_MODEL), D_MODEL ** -0.5),
        'ple_w_proj': nrm(ks[33], (Lr, PLE_DIM, D_MODEL), PLE_DIM ** -0.5),
    }


def reference(x, p, positions, w_in, b_gate, ssd_conv_w, ssd_conv_b, ssd_dt_bias, ssd_a_log, ssd_d,
              ssd_norm_w, ssd_w_out, conf_dw_w, conf_dw_b, conf_ln_g, conf_ln_b, conf_w_out,
              attn_sinks, attn_w_out, w_out, ln1_g, ln1_b, moe_w_group, moe_b_group, moe_w_expert,
              moe_b_expert, moe_w_gate, moe_w_up, moe_w_down, ln2_g, ln2_b, ple_w_gate, ple_w_proj):
    for i in range(DEPTH):
        x = decoder_layer(x, p[i], positions, w_in[i], b_gate[i], ssd_conv_w[i], ssd_conv_b[i],
                          ssd_dt_bias[i], ssd_a_log[i], ssd_d[i], ssd_norm_w[i], ssd_w_out[i],
                          conf_dw_w[i], conf_dw_b[i], conf_ln_g[i], conf_ln_b[i], conf_w_out[i],
                          attn_sinks[i], attn_w_out[i], w_out[i], ln1_g[i], ln1_b[i],
                          moe_w_group[i], moe_b_group[i], moe_w_expert[i], moe_b_expert[i],
                          moe_w_gate[i], moe_w_up[i], moe_w_down[i], ln2_g[i], ln2_b[i],
                          ple_w_gate[i], ple_w_proj[i])
    return x
```

```python
import functools

import jax
import jax.numpy as jnp
from jax import lax
from jax.experimental import pallas as pl
from jax.experimental.pallas import tpu as pltpu

F32 = jnp.float32
BF16 = jnp.bfloat16

D_MODEL = 1024
N_LAYERS = 2
PLE_DIM = 256
SSD_HEADS = 16
SSD_HEAD_DIM = 64
SSD_INNER = SSD_HEADS * SSD_HEAD_DIM
SSD_GROUPS = 4
SSD_STATE = 128
SSD_CONV = 4
SSD_CHUNK = 128
SSD_XBC = SSD_INNER + 2 * SSD_GROUPS * SSD_STATE
CONF_CH = 1024
CONF_KERNEL = 31
ATT_Q_HEADS = 16
ATT_KV_HEADS = 4
ATT_HEAD_DIM = 64
ATT_BLOCK = 128
ROPE_THETA = 10000.0
MOE_GROUPS = 4
MOE_EXPERTS_PER_GROUP = 8
MOE_EXPERTS = MOE_GROUPS * MOE_EXPERTS_PER_GROUP
MOE_TOP_K = 2
MOE_FF = 512
MOE_BLOCK = 256
DEEPNORM_ALPHA = (2 * N_LAYERS) ** 0.25
LN_EPS = 1e-5

LANES = 128
SUBLANES = 8
VMEM_LIMIT_BYTES = 56 * 1024 * 1024

COL_GATES = 0
COL_Z = 3 * D_MODEL
COL_XBC = COL_Z + SSD_INNER
COL_U = COL_XBC + SSD_XBC
COL_Q = COL_U + 2 * CONF_CH
COL_K = COL_Q + ATT_Q_HEADS * ATT_HEAD_DIM
COL_V = COL_K + ATT_KV_HEADS * ATT_HEAD_DIM
H_WIDTH = COL_V + ATT_KV_HEADS * ATT_HEAD_DIM
DT_COL_ORIG = 3 * D_MODEL + SSD_INNER + SSD_XBC


def _params(*semantics):
    return pltpu.CompilerParams(dimension_semantics=semantics, vmem_limit_bytes=VMEM_LIMIT_BYTES)


def _sigmoid(x):
    return jax.nn.sigmoid(x)


def _silu(x):
    return x * jax.nn.sigmoid(x)


def _layer_norm(x, g, b):
    mu = jnp.mean(x, axis=-1, keepdims=True)
    xc = x - mu
    var = jnp.mean(xc * xc, axis=-1, keepdims=True)
    return xc * lax.rsqrt(var + LN_EPS) * g + b


IN_TM = 1024
IN_TN = 512


def _inproj_kernel(x_ref, w_ref, wdt_ref, h_ref, dt_ref, xb_ref):
    @pl.when(pl.program_id(1) == 0)
    def _():
        xb = x_ref[...].astype(BF16)
        xb_ref[...] = xb
        dt_ref[...] = jnp.dot(xb, wdt_ref[...], preferred_element_type=F32)

    h_ref[...] = jnp.dot(xb_ref[...], w_ref[...], preferred_element_type=F32).astype(BF16)


def _in_projection(x2d, w_main, w_dt):
    t = x2d.shape[0]
    tm = min(IN_TM, t)
    return pl.pallas_call(
        _inproj_kernel,
        out_shape=(jax.ShapeDtypeStruct((t, H_WIDTH), BF16), jax.ShapeDtypeStruct((t, LANES), F32)),
        grid=(t // tm, H_WIDTH // IN_TN),
        in_specs=[
            pl.BlockSpec((tm, D_MODEL), lambda i, j: (i, 0)),
            pl.BlockSpec((D_MODEL, IN_TN), lambda i, j: (0, j)),
            pl.BlockSpec((D_MODEL, LANES), lambda i, j: (0, 0)),
        ],
        out_specs=(
            pl.BlockSpec((tm, IN_TN), lambda i, j: (i, j)),
            pl.BlockSpec((tm, LANES), lambda i, j: (i, 0)),
        ),
        scratch_shapes=[pltpu.VMEM((tm, D_MODEL), BF16)],
        compiler_params=_params("parallel", "arbitrary"),
        name="in_projection",
    )(x2d, w_main, w_dt)


def _rope_kernel(pos_ref, inv_ref, sign_ref, cos_ref, sin_ref):
    ang = pos_ref[...].astype(F32) * inv_ref[...]
    cos_ref[...] = jnp.cos(ang)
    sin_ref[...] = jnp.sin(ang) * sign_ref[...]


def _rope_tables(positions):
    t = positions.size
    tm = min(1024, t)
    half = ATT_HEAD_DIM // 2
    inv_freq = ROPE_THETA ** (-jnp.arange(half, dtype=F32) / half)
    inv = jnp.tile(inv_freq, LANES // half)[None, :]
    sign = jnp.tile(jnp.concatenate([-jnp.ones((half,), F32), jnp.ones((half,), F32)]), LANES // ATT_HEAD_DIM)[None, :]
    return pl.pallas_call(
        _rope_kernel,
        out_shape=(jax.ShapeDtypeStruct((t, LANES), F32), jax.ShapeDtypeStruct((t, LANES), F32)),
        grid=(t // tm,),
        in_specs=[
            pl.BlockSpec((tm, 1), lambda i: (i, 0)),
            pl.BlockSpec((1, LANES), lambda i: (0, 0)),
            pl.BlockSpec((1, LANES), lambda i: (0, 0)),
        ],
        out_specs=(pl.BlockSpec((tm, LANES), lambda i: (i, 0)), pl.BlockSpec((tm, LANES), lambda i: (i, 0))),
        compiler_params=_params("parallel"),
        name="rope_tables",
    )(positions.reshape(t, 1), inv, sign)


SSD_PAIRS = SSD_HEADS // 2
SSD_HALO = SUBLANES


def _ssd_kernel(xbc_ref, z_ref, dt_ref, cw_ref, cb_ref, dtb_ref, alog_ref, dsk_ref, nw_ref, wp_ref,
                y_ref, state_ref, ext_ref, ybuf_ref):
    L = SSD_CHUNK
    c = pl.program_id(1)

    @pl.when(c == 0)
    def _():
        state_ref[...] = jnp.zeros(state_ref.shape, F32)
        ext_ref[0:SSD_HALO, :] = jnp.zeros((SSD_HALO, SSD_XBC), F32)

    @pl.when(c > 0)
    def _():
        ext_ref[0:SSD_HALO, :] = ext_ref[L:L + SSD_HALO, :]

    ext_ref[SSD_HALO:SSD_HALO + L, :] = xbc_ref[...].astype(F32)

    acc = jnp.broadcast_to(cb_ref[...], (L, SSD_XBC))
    for k in range(SSD_CONV):
        off = SSD_HALO - (SSD_CONV - 1) + k
        acc = acc + cw_ref[k:k + 1, :] * ext_ref[off:off + L, :]
    act = _silu(acc)
    xs = act[:, :SSD_INNER]
    bm = act[:, SSD_INNER:SSD_INNER + SSD_GROUPS * SSD_STATE]
    cm = act[:, SSD_INNER + SSD_GROUPS * SSD_STATE:]

    x_dt = dt_ref[...] + dtb_ref[...]
    dt = jnp.maximum(x_dt, 0.0) + jnp.log1p(jnp.exp(-jnp.abs(x_dt)))
    a = -jnp.exp(alog_ref[...])
    row = lax.broadcasted_iota(jnp.int32, (L, L), 0)
    col = lax.broadcasted_iota(jnp.int32, (L, L), 1)
    causal = row >= col
    tril = causal.astype(F32)
    cs = jnp.dot(tril, dt * a, preferred_element_type=F32, precision=lax.Precision.HIGHEST)
    cs_t = cs.T
    lo = col < SSD_HEAD_DIM

    for g in range(SSD_GROUPS):
        bm_g = bm[:, g * SSD_STATE:(g + 1) * SSD_STATE].astype(BF16)
        cm_g = cm[:, g * SSD_STATE:(g + 1) * SSD_STATE].astype(BF16)
        cb_g = lax.dot_general(cm_g, bm_g, (((1,), (1,)), ((), ())), preferred_element_type=F32)
        pairs_per_group = SSD_PAIRS // SSD_GROUPS
        for jj in range(pairs_per_group):
            j = g * pairs_per_group + jj
            h0, h1 = 2 * j, 2 * j + 1
            sl = slice(j * LANES, (j + 1) * LANES)
            col0 = jnp.broadcast_to(cs[:, h0:h0 + 1], (L, L))
            col1 = jnp.broadcast_to(cs[:, h1:h1 + 1], (L, L))
            dec0 = jnp.where(causal, jnp.exp(col0 - cs_t[h0:h0 + 1, :]), 0.0)
            dec1 = jnp.where(causal, jnp.exp(col1 - cs_t[h1:h1 + 1, :]), 0.0)
            m = jnp.concatenate([cb_g * dec0, cb_g * dec1], axis=1).astype(BF16)
            dt_p = jnp.where(lo, jnp.broadcast_to(dt[:, h0:h0 + 1], (L, L)),
                             jnp.broadcast_to(dt[:, h1:h1 + 1], (L, L)))
            xs_p = xs[:, sl]
            xdt = xs_p * dt_p
            x2 = jnp.concatenate([jnp.where(lo, xdt, 0.0), jnp.where(lo, 0.0, xdt)], axis=0).astype(BF16)
            y_diag = jnp.dot(m, x2, preferred_element_type=F32)
            cs_p = jnp.where(lo, col0, col1)
            st = state_ref[j]
            y_off = jnp.exp(cs_p) * jnp.dot(cm_g, st.astype(BF16), preferred_element_type=F32)
            last = cs_p[L - 1:L, :]
            xdt_end = (xdt * jnp.exp(last - cs_p)).astype(BF16)
            new_st = lax.dot_general(bm_g, xdt_end, (((0,), (0,)), ((), ())), preferred_element_type=F32)
            state_ref[j] = st * jnp.exp(last) + new_st
            ybuf_ref[:, sl] = y_diag + y_off + xs_p * dsk_ref[:, sl]

    y = ybuf_ref[...] * _silu(z_ref[...].astype(F32))
    gw = SSD_INNER // SSD_GROUPS
    parts = []
    for g in range(SSD_GROUPS):
        yg = y[:, g * gw:(g + 1) * gw]
        parts.append(yg * lax.rsqrt(jnp.mean(yg * yg, axis=-1, keepdims=True) + LN_EPS))
    yn = jnp.concatenate(parts, axis=1) * nw_ref[...]
    y_ref[...] = jnp.dot(yn.astype(BF16), wp_ref[...], preferred_element_type=F32).astype(BF16)


def _ssd_mixer(h, dt_raw, bsz, seq, conv_w, conv_b, dt_bias, a_log, d_skip, norm_w, w_proj):
    t = bsz * seq
    L = SSD_CHUNK
    nc = seq // L
    pad = LANES - SSD_HEADS
    dtb = jnp.pad(dt_bias, (0, pad))[None, :]
    alog = jnp.pad(a_log, (0, pad))[None, :]
    dsk = jnp.repeat(d_skip, SSD_HEAD_DIM)[None, :]
    const = lambda b, c: (0, 0)
    return pl.pallas_call(
        _ssd_kernel,
        out_shape=jax.ShapeDtypeStruct((t, D_MODEL), BF16),
        grid=(bsz, nc),
        in_specs=[
            pl.BlockSpec((L, SSD_XBC), lambda b, c: (b * nc + c, COL_XBC // SSD_XBC)),
            pl.BlockSpec((L, SSD_INNER), lambda b, c: (b * nc + c, COL_Z // SSD_INNER)),
            pl.BlockSpec((L, LANES), lambda b, c: (b * nc + c, 0)),
            pl.BlockSpec((SSD_CONV, SSD_XBC), const),
            pl.BlockSpec((1, SSD_XBC), const),
            pl.BlockSpec((1, LANES), const),
            pl.BlockSpec((1, LANES), const),
            pl.BlockSpec((1, SSD_INNER), const),
            pl.BlockSpec((1, SSD_INNER), const),
            pl.BlockSpec((SSD_INNER, D_MODEL), const),
        ],
        out_specs=pl.BlockSpec((L, D_MODEL), lambda b, c: (b * nc + c, 0)),
        scratch_shapes=[
            pltpu.VMEM((SSD_PAIRS, SSD_STATE, LANES), F32),
            pltpu.VMEM((SSD_HALO + L, SSD_XBC), F32),
            pltpu.VMEM((L, SSD_INNER), F32),
        ],
        compiler_params=_params("parallel", "arbitrary"),
        name="ssd_mixer",
    )(h, h, dt_raw, conv_w, conv_b[None, :], dtb, alog, dsk, norm_w[None, :], w_proj.astype(BF16))


CONF_TS = 256
CONF_HALO = 32
CONF_ROW_BLK = 64
CONF_COL_BLK = 256


def _conf_kernel(u_ref, dw_ref, db_ref, g_ref, b_ref, wp_ref, y_ref, ext_ref, conv_ref):
    ts = u_ref.shape[0]
    s = pl.program_id(1)

    @pl.when(s == 0)
    def _():
        ext_ref[0:CONF_HALO, :] = jnp.zeros((CONF_HALO, CONF_CH), F32)

    @pl.when(s > 0)
    def _():
        ext_ref[0:CONF_HALO, :] = ext_ref[ts:ts + CONF_HALO, :]

    u = u_ref[...].astype(F32)
    ext_ref[CONF_HALO:CONF_HALO + ts, :] = u[:, :CONF_CH] * _sigmoid(u[:, CONF_CH:])

    for r0 in range(0, ts, CONF_ROW_BLK):
        for c0 in range(0, CONF_CH, CONF_COL_BLK):
            cols = slice(c0, c0 + CONF_COL_BLK)
            acc = jnp.broadcast_to(db_ref[:, cols], (CONF_ROW_BLK, CONF_COL_BLK))
            for k in range(CONF_KERNEL):
                off = CONF_HALO - (CONF_KERNEL - 1) + k + r0
                acc = acc + dw_ref[k:k + 1, cols] * ext_ref[off:off + CONF_ROW_BLK, cols]
            conv_ref[r0:r0 + CONF_ROW_BLK, cols] = acc

    hn = _silu(_layer_norm(conv_ref[...], g_ref[...], b_ref[...]))
    y_ref[...] = jnp.dot(hn.astype(BF16), wp_ref[...], preferred_element_type=F32).astype(BF16)


def _conformer(h, bsz, seq, dw_w, dw_b, ln_g, ln_b, w_proj):
    t = bsz * seq
    ts = min(CONF_TS, seq)
    nt = seq // ts
    const = lambda b, s: (0, 0)
    dw = jnp.pad(dw_w, ((0, CONF_HALO - CONF_KERNEL), (0, 0)))
    return pl.pallas_call(
        _conf_kernel,
        out_shape=jax.ShapeDtypeStruct((t, D_MODEL), BF16),
        grid=(bsz, nt),
        in_specs=[
            pl.BlockSpec((ts, 2 * CONF_CH), lambda b, s: (b * nt + s, COL_U // (2 * CONF_CH))),
            pl.BlockSpec((CONF_HALO, CONF_CH), const),
            pl.BlockSpec((1, CONF_CH), const),
            pl.BlockSpec((1, CONF_CH), const),
            pl.BlockSpec((1, CONF_CH), const),
            pl.BlockSpec((CONF_CH, D_MODEL), const),
        ],
        out_specs=pl.BlockSpec((ts, D_MODEL), lambda b, s: (b * nt + s, 0)),
        scratch_shapes=[
            pltpu.VMEM((CONF_HALO + ts, CONF_CH), F32),
            pltpu.VMEM((ts, CONF_CH), F32),
        ],
        compiler_params=_params("parallel", "arbitrary"),
        name="conformer_conv",
    )(h, dw, dw_b[None, :], ln_g[None, :], ln_b[None, :], w_proj.astype(BF16))


def _rotate_half(x):
    n = x.shape[1]
    lane = lax.broadcasted_iota(jnp.int32, x.shape, 1)
    first_half = (lane % ATT_HEAD_DIM) < (ATT_HEAD_DIM // 2)
    return jnp.where(first_half, pltpu.roll(x, n - ATT_HEAD_DIM // 2, 1), pltpu.roll(x, ATT_HEAD_DIM // 2, 1))


def _attn_kernel(q_ref, kc_ref, kp_ref, vc_ref, vp_ref, cosc_ref, sinc_ref, cosp_ref, sinp_ref, sink_ref,
                 wp_ref, y_ref, obuf_ref):
    blk = ATT_BLOCK
    i = pl.program_id(1)
    hd = ATT_HEAD_DIM
    rep = ATT_Q_HEADS // ATT_KV_HEADS
    scale = hd ** -0.5

    cos_q = jnp.concatenate([cosc_ref[...]] * (q_ref.shape[1] // LANES), axis=1)
    sin_q = jnp.concatenate([sinc_ref[...]] * (q_ref.shape[1] // LANES), axis=1)
    q = q_ref[...].astype(F32)
    q = ((q * cos_q + _rotate_half(q) * sin_q) * scale).astype(BF16)

    kw = kc_ref.shape[1]
    k = jnp.concatenate([kp_ref[...], kc_ref[...]], axis=0).astype(F32)
    cos_k = jnp.concatenate([cosp_ref[...], cosc_ref[...]], axis=0)
    sin_k = jnp.concatenate([sinp_ref[...], sinc_ref[...]], axis=0)
    cos_k = jnp.concatenate([cos_k] * (kw // LANES), axis=1)
    sin_k = jnp.concatenate([sin_k] * (kw // LANES), axis=1)
    k = (k * cos_k + _rotate_half(k) * sin_k).astype(BF16)
    v = jnp.concatenate([vp_ref[...], vc_ref[...]], axis=0)

    qi = lax.broadcasted_iota(jnp.int32, (blk, 2 * blk), 0)
    kj = lax.broadcasted_iota(jnp.int32, (blk, 2 * blk), 1)
    rel = qi + blk - kj
    mask = (rel >= 0) & (rel < blk) & ((i - 1) * blk + kj >= 0)

    for kh in range(ATT_KV_HEADS):
        k_h = k[:, kh * hd:(kh + 1) * hd]
        v_h = v[:, kh * hd:(kh + 1) * hd]
        for r in range(rep):
            hq = kh * rep + r
            q_h = q[:, hq * hd:(hq + 1) * hd]
            s = lax.dot_general(q_h, k_h, (((1,), (1,)), ((), ())), preferred_element_type=F32)
            s = jnp.where(mask, s, -jnp.inf)
            sink = sink_ref[:, hq:hq + 1]
            mx = jnp.maximum(jnp.max(s, axis=-1, keepdims=True), sink)
            e = jnp.exp(s - mx)
            probs = e / (jnp.sum(e, axis=-1, keepdims=True) + jnp.exp(sink - mx))
            obuf_ref[:, hq * hd:(hq + 1) * hd] = jnp.dot(probs.astype(BF16), v_h, preferred_element_type=F32)

    y_ref[...] = jnp.dot(obuf_ref[...].astype(BF16), wp_ref[...], preferred_element_type=F32).astype(BF16)


def _attention(h, cos_t, sin_t, bsz, seq, sinks, w_proj):
    t = bsz * seq
    blk = ATT_BLOCK
    nb = seq // blk
    qw = ATT_Q_HEADS * ATT_HEAD_DIM
    kw = ATT_KV_HEADS * ATT_HEAD_DIM
    cur = lambda b, i: b * nb + i
    prev = lambda b, i: b * nb + jnp.maximum(i - 1, 0)
    const = lambda b, i: (0, 0)
    sink = jnp.pad(sinks, (0, LANES - ATT_Q_HEADS))[None, :]
    return pl.pallas_call(
        _attn_kernel,
        out_shape=jax.ShapeDtypeStruct((t, D_MODEL), BF16),
        grid=(bsz, nb),
        in_specs=[
            pl.BlockSpec((blk, qw), lambda b, i: (cur(b, i), COL_Q // qw)),
            pl.BlockSpec((blk, kw), lambda b, i: (cur(b, i), COL_K // kw)),
            pl.BlockSpec((blk, kw), lambda b, i: (prev(b, i), COL_K // kw)),
            pl.BlockSpec((blk, kw), lambda b, i: (cur(b, i), COL_V // kw)),
            pl.BlockSpec((blk, kw), lambda b, i: (prev(b, i), COL_V // kw)),
            pl.BlockSpec((blk, LANES), lambda b, i: (cur(b, i), 0)),
            pl.BlockSpec((blk, LANES), lambda b, i: (cur(b, i), 0)),
            pl.BlockSpec((blk, LANES), lambda b, i: (prev(b, i), 0)),
            pl.BlockSpec((blk, LANES), lambda b, i: (prev(b, i), 0)),
            pl.BlockSpec((1, LANES), const),
            pl.BlockSpec((qw, D_MODEL), const),
        ],
        out_specs=pl.BlockSpec((blk, D_MODEL), lambda b, i: (cur(b, i), 0)),
        scratch_shapes=[pltpu.VMEM((blk, qw), F32)],
        compiler_params=_params("parallel", "arbitrary"),
        name="swa_attention",
    )(h, h, h, h, h, cos_t, sin_t, cos_t, sin_t, sink, w_proj.astype(BF16))


MERGE_TM = 512


def _merge_kernel(g0_ref, g1_ref, g2_ref, bg_ref, y0_ref, y1_ref, y2_ref, x_ref, wo_ref, lg_ref, lb_ref,
                  wr_ref, br_ref, x1_ref, logit_ref):
    mixed = None
    for n, (g_ref, y_ref) in enumerate(((g0_ref, y0_ref), (g1_ref, y1_ref), (g2_ref, y2_ref))):
        term = _sigmoid(g_ref[...].astype(F32) + bg_ref[n:n + 1, :]) * y_ref[...].astype(F32)
        mixed = term if mixed is None else mixed + term
    r = DEEPNORM_ALPHA * x_ref[...] + jnp.dot(mixed.astype(BF16), wo_ref[...], preferred_element_type=F32)
    x1 = _layer_norm(r, lg_ref[...], lb_ref[...])
    x1_ref[...] = x1
    logit_ref[...] = jnp.dot(x1, wr_ref[...], preferred_element_type=F32,
                             precision=lax.Precision.HIGHEST) + br_ref[...]


def _merge(h, b_gate, y_ssd, y_conf, y_att, x2d, w_out, ln_g, ln_b, w_router, b_router):
    t = x2d.shape[0]
    tm = min(MERGE_TM, t)
    const = lambda i: (0, 0)
    row = lambda i: (i, 0)
    return pl.pallas_call(
        _merge_kernel,
        out_shape=(jax.ShapeDtypeStruct((t, D_MODEL), F32), jax.ShapeDtypeStruct((t, LANES), F32)),
        grid=(t // tm,),
        in_specs=[
            pl.BlockSpec((tm, D_MODEL), lambda i: (i, 0)),
            pl.BlockSpec((tm, D_MODEL), lambda i: (i, 1)),
            pl.BlockSpec((tm, D_MODEL), lambda i: (i, 2)),
            pl.BlockSpec((3, D_MODEL), const),
            pl.BlockSpec((tm, D_MODEL), row),
            pl.BlockSpec((tm, D_MODEL), row),
            pl.BlockSpec((tm, D_MODEL), row),
            pl.BlockSpec((tm, D_MODEL), row),
            pl.BlockSpec((D_MODEL, D_MODEL), const),
            pl.BlockSpec((1, D_MODEL), const),
            pl.BlockSpec((1, D_MODEL), const),
            pl.BlockSpec((D_MODEL, LANES), const),
            pl.BlockSpec((1, LANES), const),
        ],
        out_specs=(pl.BlockSpec((tm, D_MODEL), row), pl.BlockSpec((tm, LANES), row)),
        compiler_params=_params("parallel"),
        name="merge_ln_router",
    )(h, h, h, b_gate, y_ssd, y_conf, y_att, x2d, w_out.astype(BF16), ln_g[None, :], ln_b[None, :],
      w_router, b_router)


def _gather_pipeline(i, n_steps, idx_hbm, src_hbm, idx_smem, buf, isem, gsem):
    n_rows = buf.shape[1]
    slot = i % 2
    nxt = 1 - slot

    def idx_copy(step, s):
        return pltpu.make_async_copy(idx_hbm.at[step], idx_smem.at[s], isem.at[s])

    def issue_rows(s):
        def body(r, carry):
            tok = idx_smem[s, r]
            pltpu.make_async_copy(src_hbm.at[pl.ds(tok, 1)], buf.at[s, pl.ds(r, 1)], gsem.at[s]).start()
            return carry

        lax.fori_loop(0, n_rows, body, 0)

    @pl.when(i == 0)
    def _():
        idx_copy(0, 0).start()
        idx_copy(0, 0).wait()
        issue_rows(0)

        @pl.when(n_steps > 1)
        def _():
            idx_copy(1, 1).start()

    @pl.when(i + 1 < n_steps)
    def _():
        idx_copy(i + 1, nxt).wait()
        issue_rows(nxt)

        @pl.when(i + 2 < n_steps)
        def _():
            idx_copy(i + 2, slot).start()


def _gather_wait(i, src_hbm, buf, gsem):
    slot = i % 2
    n_rows = buf.shape[1]
    pltpu.make_async_copy(src_hbm.at[pl.ds(0, n_rows)], buf.at[slot], gsem.at[slot]).wait()


def _moe_kernel(be_ref, nu_ref, idx_hbm, x_hbm, wg_ref, wu_ref, wd_ref, y_ref, idx_smem, xbuf, isem, gsem):
    del be_ref
    i = pl.program_id(0)
    n_used = nu_ref[0]
    _gather_pipeline(i, n_used, idx_hbm, x_hbm, idx_smem, xbuf, isem, gsem)

    @pl.when(i < n_used)
    def _():
        _gather_wait(i, x_hbm, xbuf, gsem)
        xb = xbuf[i % 2].astype(BF16)
        gate = jnp.dot(xb, wg_ref[...], preferred_element_type=F32)
        up = jnp.dot(xb, wu_ref[...], preferred_element_type=F32)
        hmid = (_silu(gate) * up).astype(BF16)
        y_ref[...] = jnp.dot(hmid, wd_ref[...], preferred_element_type=F32)

    @pl.when(i >= n_used)
    def _():
        y_ref[...] = jnp.zeros(y_ref.shape, F32)


def _moe_experts(x1, row_tok, block_e, n_used, w_gate, w_up, w_down):
    n_blocks = row_tok.shape[0]
    grid_spec = pltpu.PrefetchScalarGridSpec(
        num_scalar_prefetch=2,
        grid=(n_blocks,),
        in_specs=[
            pl.BlockSpec(memory_space=pl.ANY),
            pl.BlockSpec(memory_space=pl.ANY),
            pl.BlockSpec((None, D_MODEL, MOE_FF), lambda i, be, nu: (be[i], 0, 0)),
            pl.BlockSpec((None, D_MODEL, MOE_FF), lambda i, be, nu: (be[i], 0, 0)),
            pl.BlockSpec((None, MOE_FF, D_MODEL), lambda i, be, nu: (be[i], 0, 0)),
        ],
        out_specs=pl.BlockSpec((MOE_BLOCK, D_MODEL), lambda i, be, nu: (i, 0)),
        scratch_shapes=[
            pltpu.SMEM((2, MOE_BLOCK), jnp.int32),
            pltpu.VMEM((2, MOE_BLOCK, D_MODEL), F32),
            pltpu.SemaphoreType.DMA((2,)),
            pltpu.SemaphoreType.DMA((2,)),
        ],
    )
    return pl.pallas_call(
        _moe_kernel,
        out_shape=jax.ShapeDtypeStruct((n_blocks * MOE_BLOCK, D_MODEL), F32),
        grid_spec=grid_spec,
        compiler_params=_params("arbitrary"),
        name="moe_experts",
    )(block_e, n_used, row_tok, x1, w_gate, w_up, w_down)


COMB_TM = 256


def _combine_kernel(idx_hbm, y_hbm, ew_ref, x1_ref, p_ref, lg_ref, lb_ref, wpg_ref, wpp_ref, out_ref,
                    idx_smem, ybuf, isem, gsem):
    i = pl.program_id(0)
    tm = x1_ref.shape[0]
    _gather_pipeline(i, pl.num_programs(0), idx_hbm, y_hbm, idx_smem, ybuf, isem, gsem)
    _gather_wait(i, y_hbm, ybuf, gsem)
    slot = i % 2
    ew = ew_ref[...]
    ffn = ybuf[slot, 0:tm, :] * ew[:, 0:1] + ybuf[slot, tm:2 * tm, :] * ew[:, 1:2]
    x2 = _layer_norm(DEEPNORM_ALPHA * x1_ref[...] + ffn, lg_ref[...], lb_ref[...])
    gate = _sigmoid(jnp.dot(x2.astype(BF16), wpg_ref[...], preferred_element_type=F32))
    emb = jnp.dot(p_ref[...].astype(BF16), wpp_ref[...], preferred_element_type=F32)
    out_ref[...] = x2 + gate * emb


def _combine(y_rows, dest_blocks, e_w, x1, p2d, ln_g, ln_b, ple_w_gate, ple_w_proj):
    t = x1.shape[0]
    tm = dest_blocks.shape[1] // MOE_TOP_K
    const = lambda i: (0, 0)
    row = lambda i: (i, 0)
    return pl.pallas_call(
        _combine_kernel,
        out_shape=jax.ShapeDtypeStruct((t, D_MODEL), F32),
        grid=(t // tm,),
        in_specs=[
            pl.BlockSpec(memory_space=pl.ANY),
            pl.BlockSpec(memory_space=pl.ANY),
            pl.BlockSpec((tm, MOE_TOP_K), row),
            pl.BlockSpec((tm, D_MODEL), row),
            pl.BlockSpec((tm, PLE_DIM), row),
            pl.BlockSpec((1, D_MODEL), const),
            pl.BlockSpec((1, D_MODEL), const),
            pl.BlockSpec((D_MODEL, D_MODEL), const),
            pl.BlockSpec((PLE_DIM, D_MODEL), const),
        ],
        out_specs=pl.BlockSpec((tm, D_MODEL), row),
        scratch_shapes=[
            pltpu.SMEM((2, MOE_TOP_K * tm), jnp.int32),
            pltpu.VMEM((2, MOE_TOP_K * tm, D_MODEL), F32),
            pltpu.SemaphoreType.DMA((2,)),
            pltpu.SemaphoreType.DMA((2,)),
        ],
        compiler_params=_params("arbitrary"),
        name="moe_combine",
    )(dest_blocks, y_rows, e_w, x1, p2d, ln_g[None, :], ln_b[None, :], ple_w_gate.astype(BF16),
      ple_w_proj.astype(BF16))


def _route(logits):
    t = logits.shape[0]
    tk = t * MOE_TOP_K
    g_logits = logits[:, :MOE_GROUPS]
    e_logits = logits[:, MOE_GROUPS:MOE_GROUPS + MOE_EXPERTS].reshape(t, MOE_GROUPS, MOE_EXPERTS_PER_GROUP)
    g_prob = jax.nn.softmax(g_logits, axis=-1)
    g_idx = jnp.argmax(g_logits, axis=-1).astype(jnp.int32)
    g_w = jnp.take_along_axis(g_prob, g_idx[:, None], axis=1)
    e_in = jnp.take_along_axis(e_logits, g_idx[:, None, None], axis=1)[:, 0]
    top_v, top_i = lax.top_k(e_in, MOE_TOP_K)
    e_w = jax.nn.softmax(top_v, axis=-1) * g_w
    e_id = g_idx[:, None] * MOE_EXPERTS_PER_GROUP + top_i.astype(jnp.int32)
    flat_e = e_id.reshape(tk)

    onehot = (flat_e[:, None] == jnp.arange(MOE_EXPERTS, dtype=jnp.int32)[None, :]).astype(jnp.int32)
    csum = jnp.cumsum(onehot, axis=0)
    rank = jnp.sum(onehot * csum, axis=1) - 1
    counts = csum[-1]
    padded = ((counts + MOE_BLOCK - 1) // MOE_BLOCK) * MOE_BLOCK
    pends = jnp.cumsum(padded)
    pstarts = pends - padded
    dest = pstarts[flat_e] + rank

    n_rows = tk + MOE_EXPERTS * MOE_BLOCK
    n_blocks = n_rows // MOE_BLOCK
    flat_tok = jnp.arange(tk, dtype=jnp.int32) // MOE_TOP_K
    row_tok = jnp.zeros((n_rows,), jnp.int32).at[dest].set(flat_tok, unique_indices=True)
    block_e = jnp.searchsorted(pends, jnp.arange(n_blocks, dtype=jnp.int32) * MOE_BLOCK, side="right")
    block_e = jnp.minimum(block_e, MOE_EXPERTS - 1).astype(jnp.int32)
    n_used = (pends[-1] // MOE_BLOCK).astype(jnp.int32).reshape(1)
    return e_w, dest.reshape(t, MOE_TOP_K), row_tok.reshape(n_blocks, MOE_BLOCK), block_e, n_used


def _layer(x2d, p2d, cos_t, sin_t, bsz, seq, w_in, b_gate, ssd_conv_w, ssd_conv_b, ssd_dt_bias, ssd_a_log,
           ssd_d, ssd_norm_w, ssd_w_out, conf_dw_w, conf_dw_b, conf_ln_g, conf_ln_b, conf_w_out, attn_sinks,
           attn_w_out, w_out, ln1_g, ln1_b, moe_w_group, moe_b_group, moe_w_expert, moe_b_expert, moe_w_gate,
           moe_w_up, moe_w_down, ln2_g, ln2_b, ple_w_gate, ple_w_proj):
    t = bsz * seq
    w_main = jnp.concatenate([w_in[:, :DT_COL_ORIG], w_in[:, DT_COL_ORIG + SSD_HEADS:]], axis=1).astype(BF16)
    w_dt = jnp.pad(w_in[:, DT_COL_ORIG:DT_COL_ORIG + SSD_HEADS], ((0, 0), (0, LANES - SSD_HEADS))).astype(BF16)
    h, dt_raw = _in_projection(x2d, w_main, w_dt)

    y_ssd = _ssd_mixer(h, dt_raw, bsz, seq, ssd_conv_w, ssd_conv_b, ssd_dt_bias, ssd_a_log, ssd_d, ssd_norm_w,
                       ssd_w_out)
    y_conf = _conformer(h, bsz, seq, conf_dw_w, conf_dw_b, conf_ln_g, conf_ln_b, conf_w_out)
    y_att = _attention(h, cos_t, sin_t, bsz, seq, attn_sinks, attn_w_out)

    n_router = MOE_GROUPS + MOE_EXPERTS
    w_router = jnp.pad(jnp.concatenate([moe_w_group, moe_w_expert], axis=1), ((0, 0), (0, LANES - n_router)))
    b_router = jnp.pad(jnp.concatenate([moe_b_group, moe_b_expert]), (0, LANES - n_router))[None, :]
    x1, logits = _merge(h, b_gate, y_ssd, y_conf, y_att, x2d, w_out, ln1_g, ln1_b, w_router, b_router)

    e_w, dest, row_tok, block_e, n_used = _route(logits)
    y_rows = _moe_experts(x1, row_tok, block_e, n_used, moe_w_gate.astype(BF16), moe_w_up.astype(BF16),
                          moe_w_down.astype(BF16))
    tm = min(COMB_TM, t)
    dest_blocks = dest.reshape(t // tm, tm, MOE_TOP_K).transpose(0, 2, 1).reshape(t // tm, MOE_TOP_K * tm)
    return _combine(y_rows, dest_blocks, e_w, x1, p2d, ln2_g, ln2_b, ple_w_gate, ple_w_proj)


def kernel(x, p, positions, w_in, b_gate, ssd_conv_w, ssd_conv_b, ssd_dt_bias, ssd_a_log, ssd_d, ssd_norm_w, ssd_w_out, conf_dw_w, conf_dw_b, conf_ln_g, conf_ln_b, conf_w_out, attn_sinks, attn_w_out, w_out, ln1_g, ln1_b, moe_w_group, moe_b_group, moe_w_expert, moe_b_expert, moe_w_gate, moe_w_up, moe_w_down, ln2_g, ln2_b, ple_w_gate, ple_w_proj):
    bsz, seq, d = x.shape
    t = bsz * seq
    cos_t, sin_t = _rope_tables(positions)
    x2d = x.reshape(t, d)
    per_layer = (w_in, b_gate, ssd_conv_w, ssd_conv_b, ssd_dt_bias, ssd_a_log, ssd_d, ssd_norm_w, ssd_w_out,
                 conf_dw_w, conf_dw_b, conf_ln_g, conf_ln_b, conf_w_out, attn_sinks, attn_w_out, w_out, ln1_g,
                 ln1_b, moe_w_group, moe_b_group, moe_w_expert, moe_b_expert, moe_w_gate, moe_w_up, moe_w_down,
                 ln2_g, ln2_b, ple_w_gate, ple_w_proj)
    for layer in range(w_in.shape[0]):
        x2d = _layer(x2d, p[layer].reshape(t, -1), cos_t, sin_t, bsz, seq, *(w[layer] for w in per_layer))
    return x2d.reshape(bsz, seq, d)
```

```python
import functools

import jax
import jax.numpy as jnp
from jax import lax
from jax.experimental import pallas as pl
from jax.experimental.pallas import tpu as pltpu

F32 = jnp.float32
BF16 = jnp.bfloat16

D_MODEL = 1024
N_LAYERS = 2
PLE_DIM = 256
SSD_HEADS = 16
SSD_HEAD_DIM = 64
SSD_INNER = SSD_HEADS * SSD_HEAD_DIM
SSD_GROUPS = 4
SSD_STATE = 128
SSD_CONV = 4
SSD_CHUNK = 128
SSD_XBC = SSD_INNER + 2 * SSD_GROUPS * SSD_STATE
CONF_CH = 1024
CONF_KERNEL = 31
ATT_Q_HEADS = 16
ATT_KV_HEADS = 4
ATT_HEAD_DIM = 64
ATT_BLOCK = 128
ROPE_THETA = 10000.0
MOE_GROUPS = 4
MOE_EXPERTS_PER_GROUP = 8
MOE_EXPERTS = MOE_GROUPS * MOE_EXPERTS_PER_GROUP
MOE_TOP_K = 2
MOE_FF = 512
MOE_BLOCK = 256
DEEPNORM_ALPHA = (2 * N_LAYERS) ** 0.25
LN_EPS = 1e-5

LANES = 128
SUBLANES = 8
VMEM_LIMIT_BYTES = 56 * 1024 * 1024

COL_GATES = 0
COL_Z = 3 * D_MODEL
COL_XBC = COL_Z + SSD_INNER
COL_U = COL_XBC + SSD_XBC
COL_Q = COL_U + 2 * CONF_CH
COL_K = COL_Q + ATT_Q_HEADS * ATT_HEAD_DIM
COL_V = COL_K + ATT_KV_HEADS * ATT_HEAD_DIM
H_WIDTH = COL_V + ATT_KV_HEADS * ATT_HEAD_DIM
DT_COL_ORIG = 3 * D_MODEL + SSD_INNER + SSD_XBC


def _params(*semantics):
    return pltpu.CompilerParams(dimension_semantics=semantics, vmem_limit_bytes=VMEM_LIMIT_BYTES)


def _sigmoid(x):
    return jax.nn.sigmoid(x)


def _silu(x):
    return x * jax.nn.sigmoid(x)


def _layer_norm(x, g, b):
    mu = jnp.mean(x, axis=-1, keepdims=True)
    xc = x - mu
    var = jnp.mean(xc * xc, axis=-1, keepdims=True)
    return xc * lax.rsqrt(var + LN_EPS) * g + b


ROW_TILES = D_MODEL // LANES


def _store_row_tiles(ref, x):
    rows = x.shape[0]
    for k in range(ROW_TILES):
        ref[pl.ds(k, rows, stride=ROW_TILES), :] = x[:, k * LANES:(k + 1) * LANES]


def _load_row_tiles(ref, first_row, rows):
    return jnp.concatenate(
        [ref[pl.ds(first_row * ROW_TILES + k, rows, stride=ROW_TILES), :] for k in range(ROW_TILES)], axis=1)


IN_TM = 1024
IN_TN = 512


def _inproj_kernel(x_ref, w_ref, wdt_ref, h_ref, dt_ref, xb_ref):
    @pl.when(pl.program_id(1) == 0)
    def _():
        xb = x_ref[...].astype(BF16)
        xb_ref[...] = xb
        dt_ref[...] = jnp.dot(xb, wdt_ref[...], preferred_element_type=F32)

    h_ref[...] = jnp.dot(xb_ref[...], w_ref[...], preferred_element_type=F32).astype(BF16)


def _in_projection(x2d, w_main, w_dt):
    t = x2d.shape[0]
    tm = min(IN_TM, t)
    return pl.pallas_call(
        _inproj_kernel,
        out_shape=(jax.ShapeDtypeStruct((t, H_WIDTH), BF16), jax.ShapeDtypeStruct((t, LANES), F32)),
        grid=(t // tm, H_WIDTH // IN_TN),
        in_specs=[
            pl.BlockSpec((tm, D_MODEL), lambda i, j: (i, 0)),
            pl.BlockSpec((D_MODEL, IN_TN), lambda i, j: (0, j)),
            pl.BlockSpec((D_MODEL, LANES), lambda i, j: (0, 0)),
        ],
        out_specs=(
            pl.BlockSpec((tm, IN_TN), lambda i, j: (i, j)),
            pl.BlockSpec((tm, LANES), lambda i, j: (i, 0)),
        ),
        scratch_shapes=[pltpu.VMEM((tm, D_MODEL), BF16)],
        compiler_params=_params("parallel", "arbitrary"),
        name="in_projection",
    )(x2d, w_main, w_dt)


def _rope_kernel(pos_ref, inv_ref, sign_ref, cos_ref, sin_ref):
    ang = pos_ref[...].astype(F32) * inv_ref[...]
    cos_ref[...] = jnp.cos(ang)
    sin_ref[...] = jnp.sin(ang) * sign_ref[...]


def _rope_tables(positions):
    t = positions.size
    tm = min(1024, t)
    half = ATT_HEAD_DIM // 2
    inv_freq = ROPE_THETA ** (-jnp.arange(half, dtype=F32) / half)
    inv = jnp.tile(inv_freq, LANES // half)[None, :]
    sign = jnp.concatenate([-jnp.ones((LANES // 2,), F32), jnp.ones((LANES // 2,), F32)])[None, :]
    return pl.pallas_call(
        _rope_kernel,
        out_shape=(jax.ShapeDtypeStruct((t, LANES), F32), jax.ShapeDtypeStruct((t, LANES), F32)),
        grid=(t // tm,),
        in_specs=[
            pl.BlockSpec((tm, 1), lambda i: (i, 0)),
            pl.BlockSpec((1, LANES), lambda i: (0, 0)),
            pl.BlockSpec((1, LANES), lambda i: (0, 0)),
        ],
        out_specs=(pl.BlockSpec((tm, LANES), lambda i: (i, 0)), pl.BlockSpec((tm, LANES), lambda i: (i, 0))),
        compiler_params=_params("parallel"),
        name="rope_tables",
    )(positions.reshape(t, 1), inv, sign)


SSD_PAIRS = SSD_HEADS // 2
SSD_HALO = SUBLANES


def _ssd_kernel(xbc_ref, z_ref, dt_ref, cw_ref, cb_ref, dtb_ref, alog_ref, dsk_ref, nw_ref, wp_ref,
                y_ref, state_ref, ext_ref, ybuf_ref):
    L = SSD_CHUNK
    c = pl.program_id(1)

    @pl.when(c == 0)
    def _():
        state_ref[...] = jnp.zeros(state_ref.shape, F32)
        ext_ref[0:SSD_HALO, :] = jnp.zeros((SSD_HALO, SSD_XBC), F32)

    @pl.when(c > 0)
    def _():
        ext_ref[0:SSD_HALO, :] = ext_ref[L:L + SSD_HALO, :]

    ext_ref[SSD_HALO:SSD_HALO + L, :] = xbc_ref[...].astype(F32)

    acc = jnp.broadcast_to(cb_ref[...], (L, SSD_XBC))
    for k in range(SSD_CONV):
        off = SSD_HALO - (SSD_CONV - 1) + k
        acc = acc + cw_ref[k:k + 1, :] * ext_ref[off:off + L, :]
    act = _silu(acc)
    xs = act[:, :SSD_INNER]
    bm = act[:, SSD_INNER:SSD_INNER + SSD_GROUPS * SSD_STATE]
    cm = act[:, SSD_INNER + SSD_GROUPS * SSD_STATE:]

    x_dt = dt_ref[...] + dtb_ref[...]
    dt = jnp.maximum(x_dt, 0.0) + jnp.log1p(jnp.exp(-jnp.abs(x_dt)))
    a = -jnp.exp(alog_ref[...])
    row = lax.broadcasted_iota(jnp.int32, (L, L), 0)
    col = lax.broadcasted_iota(jnp.int32, (L, L), 1)
    causal = row >= col
    tril = causal.astype(F32)
    cs = jnp.dot(tril, dt * a, preferred_element_type=F32, precision=lax.Precision.HIGHEST)
    cs_t = cs.T
    lo = col < SSD_HEAD_DIM

    for g in range(SSD_GROUPS):
        bm_g = bm[:, g * SSD_STATE:(g + 1) * SSD_STATE].astype(BF16)
        cm_g = cm[:, g * SSD_STATE:(g + 1) * SSD_STATE].astype(BF16)
        cb_g = lax.dot_general(cm_g, bm_g, (((1,), (1,)), ((), ())), preferred_element_type=F32)
        pairs_per_group = SSD_PAIRS // SSD_GROUPS
        for jj in range(pairs_per_group):
            j = g * pairs_per_group + jj
            h0, h1 = 2 * j, 2 * j + 1
            sl = slice(j * LANES, (j + 1) * LANES)
            col0 = jnp.broadcast_to(cs[:, h0:h0 + 1], (L, L))
            col1 = jnp.broadcast_to(cs[:, h1:h1 + 1], (L, L))
            dec0 = jnp.where(causal, jnp.exp(col0 - cs_t[h0:h0 + 1, :]), 0.0)
            dec1 = jnp.where(causal, jnp.exp(col1 - cs_t[h1:h1 + 1, :]), 0.0)
            m = jnp.concatenate([cb_g * dec0, cb_g * dec1], axis=1).astype(BF16)
            dt_p = jnp.where(lo, jnp.broadcast_to(dt[:, h0:h0 + 1], (L, L)),
                             jnp.broadcast_to(dt[:, h1:h1 + 1], (L, L)))
            xs_p = xs[:, sl]
            xdt = xs_p * dt_p
            x2 = jnp.concatenate([jnp.where(lo, xdt, 0.0), jnp.where(lo, 0.0, xdt)], axis=0).astype(BF16)
            y_diag = jnp.dot(m, x2, preferred_element_type=F32)
            cs_p = jnp.where(lo, col0, col1)
            st = state_ref[j]
            y_off = jnp.exp(cs_p) * jnp.dot(cm_g, st.astype(BF16), preferred_element_type=F32)
            last = cs_p[L - 1:L, :]
            xdt_end = (xdt * jnp.exp(last - cs_p)).astype(BF16)
            new_st = lax.dot_general(bm_g, xdt_end, (((0,), (0,)), ((), ())), preferred_element_type=F32)
            state_ref[j] = st * jnp.exp(last) + new_st
            ybuf_ref[:, sl] = y_diag + y_off + xs_p * dsk_ref[:, sl]

    y = ybuf_ref[...] * _silu(z_ref[...].astype(F32))
    gw = SSD_INNER // SSD_GROUPS
    parts = []
    for g in range(SSD_GROUPS):
        yg = y[:, g * gw:(g + 1) * gw]
        parts.append(yg * lax.rsqrt(jnp.mean(yg * yg, axis=-1, keepdims=True) + LN_EPS))
    yn = jnp.concatenate(parts, axis=1) * nw_ref[...]
    y_ref[...] = jnp.dot(yn.astype(BF16), wp_ref[...], preferred_element_type=F32).astype(BF16)


def _ssd_mixer(h, dt_raw, bsz, seq, conv_w, conv_b, dt_bias, a_log, d_skip, norm_w, w_proj):
    t = bsz * seq
    L = SSD_CHUNK
    nc = seq // L
    pad = LANES - SSD_HEADS
    dtb = jnp.pad(dt_bias, (0, pad))[None, :]
    alog = jnp.pad(a_log, (0, pad))[None, :]
    dsk = jnp.repeat(d_skip, SSD_HEAD_DIM)[None, :]
    const = lambda b, c: (0, 0)
    return pl.pallas_call(
        _ssd_kernel,
        out_shape=jax.ShapeDtypeStruct((t, D_MODEL), BF16),
        grid=(bsz, nc),
        in_specs=[
            pl.BlockSpec((L, SSD_XBC), lambda b, c: (b * nc + c, COL_XBC // SSD_XBC)),
            pl.BlockSpec((L, SSD_INNER), lambda b, c: (b * nc + c, COL_Z // SSD_INNER)),
            pl.BlockSpec((L, LANES), lambda b, c: (b * nc + c, 0)),
            pl.BlockSpec((SSD_CONV, SSD_XBC), const),
            pl.BlockSpec((1, SSD_XBC), const),
            pl.BlockSpec((1, LANES), const),
            pl.BlockSpec((1, LANES), const),
            pl.BlockSpec((1, SSD_INNER), const),
            pl.BlockSpec((1, SSD_INNER), const),
            pl.BlockSpec((SSD_INNER, D_MODEL), const),
        ],
        out_specs=pl.BlockSpec((L, D_MODEL), lambda b, c: (b * nc + c, 0)),
        scratch_shapes=[
            pltpu.VMEM((SSD_PAIRS, SSD_STATE, LANES), F32),
            pltpu.VMEM((SSD_HALO + L, SSD_XBC), F32),
            pltpu.VMEM((L, SSD_INNER), F32),
        ],
        compiler_params=_params("parallel", "arbitrary"),
        name="ssd_mixer",
    )(h, h, dt_raw, conv_w, conv_b[None, :], dtb, alog, dsk, norm_w[None, :], w_proj.astype(BF16))


CONF_TS = 256
CONF_HALO = 32
CONF_ROW_BLK = 64
CONF_COL_BLK = 256


def _conf_kernel(u_ref, dw_ref, db_ref, g_ref, b_ref, wp_ref, y_ref, ext_ref, conv_ref):
    ts = u_ref.shape[0]
    s = pl.program_id(1)

    @pl.when(s == 0)
    def _():
        ext_ref[0:CONF_HALO, :] = jnp.zeros((CONF_HALO, CONF_CH), F32)

    @pl.when(s > 0)
    def _():
        ext_ref[0:CONF_HALO, :] = ext_ref[ts:ts + CONF_HALO, :]

    u = u_ref[...].astype(F32)
    ext_ref[CONF_HALO:CONF_HALO + ts, :] = u[:, :CONF_CH] * _sigmoid(u[:, CONF_CH:])

    for r0 in range(0, ts, CONF_ROW_BLK):
        for c0 in range(0, CONF_CH, CONF_COL_BLK):
            cols = slice(c0, c0 + CONF_COL_BLK)
            acc = jnp.broadcast_to(db_ref[:, cols], (CONF_ROW_BLK, CONF_COL_BLK))
            for k in range(CONF_KERNEL):
                off = CONF_HALO - (CONF_KERNEL - 1) + k + r0
                acc = acc + dw_ref[k:k + 1, cols] * ext_ref[off:off + CONF_ROW_BLK, cols]
            conv_ref[r0:r0 + CONF_ROW_BLK, cols] = acc

    hn = _silu(_layer_norm(conv_ref[...], g_ref[...], b_ref[...]))
    y_ref[...] = jnp.dot(hn.astype(BF16), wp_ref[...], preferred_element_type=F32).astype(BF16)


def _conformer(h, bsz, seq, dw_w, dw_b, ln_g, ln_b, w_proj):
    t = bsz * seq
    ts = min(CONF_TS, seq)
    nt = seq // ts
    const = lambda b, s: (0, 0)
    dw = jnp.pad(dw_w, ((0, CONF_HALO - CONF_KERNEL), (0, 0)))
    return pl.pallas_call(
        _conf_kernel,
        out_shape=jax.ShapeDtypeStruct((t, D_MODEL), BF16),
        grid=(bsz, nt),
        in_specs=[
            pl.BlockSpec((ts, 2 * CONF_CH), lambda b, s: (b * nt + s, COL_U // (2 * CONF_CH))),
            pl.BlockSpec((CONF_HALO, CONF_CH), const),
            pl.BlockSpec((1, CONF_CH), const),
            pl.BlockSpec((1, CONF_CH), const),
            pl.BlockSpec((1, CONF_CH), const),
            pl.BlockSpec((CONF_CH, D_MODEL), const),
        ],
        out_specs=pl.BlockSpec((ts, D_MODEL), lambda b, s: (b * nt + s, 0)),
        scratch_shapes=[
            pltpu.VMEM((CONF_HALO + ts, CONF_CH), F32),
            pltpu.VMEM((ts, CONF_CH), F32),
        ],
        compiler_params=_params("parallel", "arbitrary"),
        name="conformer_conv",
    )(h, dw, dw_b[None, :], ln_g[None, :], ln_b[None, :], w_proj.astype(BF16))


def _pair_interleave(width):
    half = ATT_HEAD_DIM // 2
    j = jnp.arange(width, dtype=jnp.int32)
    block, lane = j // LANES, j % LANES
    chunk, within = lane // half, lane % half
    return block * LANES + (chunk % 2) * ATT_HEAD_DIM + (chunk // 2) * half + within


def _attn_kernel(q_ref, kc_ref, kp_ref, vc_ref, vp_ref, cosc_ref, sinc_ref, cosp_ref, sinp_ref, sink_ref,
                 wp_ref, y_ref, obuf_ref):
    blk = ATT_BLOCK
    i = pl.program_id(1)
    rep = ATT_Q_HEADS // ATT_KV_HEADS
    quarter = ATT_HEAD_DIM // 2
    scale = ATT_HEAD_DIM ** -0.5

    def lane_masks(rows):
        lane = lax.broadcasted_iota(jnp.int32, (rows, LANES), 1)
        return (lane % ATT_HEAD_DIM) < quarter, lane < ATT_HEAD_DIM

    def rope(x, cos, sin):
        return x * cos + pltpu.roll(x, ATT_HEAD_DIM, 1) * sin

    first_q, lo_q = lane_masks(blk)
    first_k, lo_k = lane_masks(2 * blk)

    cos_q, sin_q = cosc_ref[...], sinc_ref[...]
    cos_k = jnp.concatenate([cosp_ref[...], cosc_ref[...]], axis=0)
    sin_k = jnp.concatenate([sinp_ref[...], sinc_ref[...]], axis=0)
    k = jnp.concatenate([kp_ref[...], kc_ref[...]], axis=0).astype(F32)
    v = jnp.concatenate([vp_ref[...], vc_ref[...]], axis=0).astype(F32)

    qi = lax.broadcasted_iota(jnp.int32, (blk, 2 * blk), 0)
    kj = lax.broadcasted_iota(jnp.int32, (blk, 2 * blk), 1)
    low = jnp.where(i == 0, blk - 1, qi)
    mask = (kj <= qi + blk) & (kj > low)

    for kh in range(ATT_KV_HEADS):
        kb = rope(k[:, (kh // 2) * LANES:(kh // 2 + 1) * LANES], cos_k, sin_k)
        vb = v[:, (kh // 2) * LANES:(kh // 2 + 1) * LANES]
        if kh % 2 == 0:
            k_dup = jnp.where(first_k, kb, pltpu.roll(kb, quarter, 1))
            v_dup = jnp.where(lo_k, vb, pltpu.roll(vb, ATT_HEAD_DIM, 1))
        else:
            k_dup = jnp.where(first_k, pltpu.roll(kb, LANES - quarter, 1), kb)
            v_dup = jnp.where(lo_k, pltpu.roll(vb, ATT_HEAD_DIM, 1), vb)
        k_dup = k_dup.astype(BF16)
        v_ext = jnp.concatenate([v_dup, jnp.ones_like(v_dup)], axis=1).astype(BF16)

        stacked = []
        for qb in range(rep // 2):
            cols = slice((kh * (rep // 2) + qb) * LANES, (kh * (rep // 2) + qb + 1) * LANES)
            qr = rope(q_ref[:, cols].astype(F32), cos_q, sin_q) * scale
            stacked.append(jnp.where(first_q, qr, 0.0).astype(BF16))
            stacked.append(jnp.where(first_q, 0.0, qr).astype(BF16))
        qs = jnp.concatenate(stacked, axis=0)
        s_all = lax.dot_general(qs, k_dup, (((1,), (1,)), ((), ())), preferred_element_type=F32)

        es, sink_terms = [], []
        for r in range(rep):
            hq = kh * rep + r
            s = jnp.where(mask, s_all[r * blk:(r + 1) * blk], -jnp.inf)
            sink = sink_ref[hq:hq + 1, :]
            mx = jnp.maximum(jnp.broadcast_to(jnp.max(s, axis=-1, keepdims=True), (blk, LANES)), sink)
            sink_terms.append(jnp.exp(sink - mx))
            es.append(jnp.exp(s - jnp.concatenate([mx, mx], axis=1)).astype(BF16))
        o = jnp.dot(jnp.concatenate(es, axis=0), v_ext, preferred_element_type=F32)
        outs = []
        for r in range(rep):
            rows = slice(r * blk, (r + 1) * blk)
            outs.append(o[rows, :LANES] * (1.0 / (o[rows, LANES:] + sink_terms[r])))
        for qb in range(rep // 2):
            cols = slice((kh * (rep // 2) + qb) * LANES, (kh * (rep // 2) + qb + 1) * LANES)
            obuf_ref[:, cols] = jnp.where(lo_q, outs[2 * qb], outs[2 * qb + 1])

    y_ref[...] = jnp.dot(obuf_ref[...].astype(BF16), wp_ref[...], preferred_element_type=F32).astype(BF16)


def _attention(h, cos_t, sin_t, bsz, seq, sinks, w_proj):
    t = bsz * seq
    blk = ATT_BLOCK
    nb = seq // blk
    qw = ATT_Q_HEADS * ATT_HEAD_DIM
    kw = ATT_KV_HEADS * ATT_HEAD_DIM
    cur = lambda b, i: b * nb + i
    prev = lambda b, i: b * nb + jnp.maximum(i - 1, 0)
    const = lambda b, i: (0, 0)
    sink = jnp.broadcast_to(sinks[:, None], (ATT_Q_HEADS, LANES))
    return pl.pallas_call(
        _attn_kernel,
        out_shape=jax.ShapeDtypeStruct((t, D_MODEL), BF16),
        grid=(bsz, nb),
        in_specs=[
            pl.BlockSpec((blk, qw), lambda b, i: (cur(b, i), COL_Q // qw)),
            pl.BlockSpec((blk, kw), lambda b, i: (cur(b, i), COL_K // kw)),
            pl.BlockSpec((blk, kw), lambda b, i: (prev(b, i), COL_K // kw)),
            pl.BlockSpec((blk, kw), lambda b, i: (cur(b, i), COL_V // kw)),
            pl.BlockSpec((blk, kw), lambda b, i: (prev(b, i), COL_V // kw)),
            pl.BlockSpec((blk, LANES), lambda b, i: (cur(b, i), 0)),
            pl.BlockSpec((blk, LANES), lambda b, i: (cur(b, i), 0)),
            pl.BlockSpec((blk, LANES), lambda b, i: (prev(b, i), 0)),
            pl.BlockSpec((blk, LANES), lambda b, i: (prev(b, i), 0)),
            pl.BlockSpec((ATT_Q_HEADS, LANES), const),
            pl.BlockSpec((qw, D_MODEL), const),
        ],
        out_specs=pl.BlockSpec((blk, D_MODEL), lambda b, i: (cur(b, i), 0)),
        scratch_shapes=[pltpu.VMEM((blk, qw), F32)],
        compiler_params=_params("parallel", "arbitrary"),
        name="swa_attention",
    )(h, h, h, h, h, cos_t, sin_t, cos_t, sin_t, sink, w_proj.astype(BF16))


MERGE_TM = 512


def _merge_kernel(g0_ref, g1_ref, g2_ref, bg_ref, y0_ref, y1_ref, y2_ref, x_ref, wo_ref, lg_ref, lb_ref,
                  wr_ref, br_ref, x1_ref, logit_ref):
    mixed = None
    for n, (g_ref, y_ref) in enumerate(((g0_ref, y0_ref), (g1_ref, y1_ref), (g2_ref, y2_ref))):
        term = _sigmoid(g_ref[...].astype(F32) + bg_ref[n:n + 1, :]) * y_ref[...].astype(F32)
        mixed = term if mixed is None else mixed + term
    r = DEEPNORM_ALPHA * x_ref[...] + jnp.dot(mixed.astype(BF16), wo_ref[...], preferred_element_type=F32)
    x1 = _layer_norm(r, lg_ref[...], lb_ref[...])
    _store_row_tiles(x1_ref, x1)
    logit_ref[...] = jnp.dot(x1, wr_ref[...], preferred_element_type=F32,
                             precision=lax.Precision.HIGHEST) + br_ref[...]


def _merge(h, b_gate, y_ssd, y_conf, y_att, x2d, w_out, ln_g, ln_b, w_router, b_router):
    t = x2d.shape[0]
    tm = min(MERGE_TM, t)
    const = lambda i: (0, 0)
    row = lambda i: (i, 0)
    return pl.pallas_call(
        _merge_kernel,
        out_shape=(jax.ShapeDtypeStruct((t * ROW_TILES, LANES), F32), jax.ShapeDtypeStruct((t, LANES), F32)),
        grid=(t // tm,),
        in_specs=[
            pl.BlockSpec((tm, D_MODEL), lambda i: (i, 0)),
            pl.BlockSpec((tm, D_MODEL), lambda i: (i, 1)),
            pl.BlockSpec((tm, D_MODEL), lambda i: (i, 2)),
            pl.BlockSpec((3, D_MODEL), const),
            pl.BlockSpec((tm, D_MODEL), row),
            pl.BlockSpec((tm, D_MODEL), row),
            pl.BlockSpec((tm, D_MODEL), row),
            pl.BlockSpec((tm, D_MODEL), row),
            pl.BlockSpec((D_MODEL, D_MODEL), const),
            pl.BlockSpec((1, D_MODEL), const),
            pl.BlockSpec((1, D_MODEL), const),
            pl.BlockSpec((D_MODEL, LANES), const),
            pl.BlockSpec((1, LANES), const),
        ],
        out_specs=(pl.BlockSpec((tm * ROW_TILES, LANES), row), pl.BlockSpec((tm, LANES), row)),
        compiler_params=_params("parallel"),
        name="merge_ln_router",
    )(h, h, h, b_gate, y_ssd, y_conf, y_att, x2d, w_out.astype(BF16), ln_g[None, :], ln_b[None, :],
      w_router, b_router)


GATHER_UNROLL = 8


def _on_slot(slot, fn):
    for s in (0, 1):
        @pl.when(slot == s)
        def _(s=s):
            fn(s)


def _gather_pipeline(i, n_steps, idx_hbm, src_hbm, idx_smem, buf, isem, gsem):
    n_rows = buf[0].shape[0] // ROW_TILES
    slot = i % 2

    def idx_copy(step, s):
        return pltpu.make_async_copy(idx_hbm.at[step], idx_smem[s], isem.at[s])

    def issue_rows(s):
        def body(it, carry):
            for u in range(GATHER_UNROLL):
                r = it * GATHER_UNROLL + u
                src_row = pl.multiple_of(idx_smem[s][r] * ROW_TILES, ROW_TILES)
                dst_row = pl.multiple_of(r * ROW_TILES, ROW_TILES)
                pltpu.make_async_copy(src_hbm.at[pl.ds(src_row, ROW_TILES)],
                                      buf[s].at[pl.ds(dst_row, ROW_TILES)], gsem.at[s]).start()
            return carry

        lax.fori_loop(0, n_rows // GATHER_UNROLL, body, 0)

    @pl.when(i == 0)
    def _():
        idx_copy(0, 0).start()
        idx_copy(0, 0).wait()
        issue_rows(0)

        @pl.when(n_steps > 1)
        def _():
            idx_copy(1, 1).start()

    @pl.when(i + 1 < n_steps)
    def _():
        def prefetch(s):
            idx_copy(i + 1, 1 - s).wait()
            issue_rows(1 - s)

            @pl.when(i + 2 < n_steps)
            def _():
                idx_copy(i + 2, s).start()

        _on_slot(slot, prefetch)


def _gather_wait(s, src_hbm, buf, gsem):
    pltpu.make_async_copy(src_hbm.at[pl.ds(0, buf[s].shape[0])], buf[s], gsem.at[s]).wait()


def _moe_kernel(be_ref, nu_ref, idx_hbm, x_hbm, wg_ref, wu_ref, wd_ref, y_ref, idx0, idx1, xbuf0, xbuf1,
                isem, gsem):
    del be_ref
    i = pl.program_id(0)
    n_used = nu_ref[0]
    xbuf = (xbuf0, xbuf1)
    _gather_pipeline(i, n_used, idx_hbm, x_hbm, (idx0, idx1), xbuf, isem, gsem)

    @pl.when(i < n_used)
    def _():
        def compute(s):
            _gather_wait(s, x_hbm, xbuf, gsem)
            xb = _load_row_tiles(xbuf[s], 0, MOE_BLOCK).astype(BF16)
            gate = jnp.dot(xb, wg_ref[...], preferred_element_type=F32)
            up = jnp.dot(xb, wu_ref[...], preferred_element_type=F32)
            hmid = (_silu(gate) * up).astype(BF16)
            _store_row_tiles(y_ref, jnp.dot(hmid, wd_ref[...], preferred_element_type=F32))

        _on_slot(i % 2, compute)

    @pl.when(i >= n_used)
    def _():
        y_ref[...] = jnp.zeros(y_ref.shape, F32)


def _moe_experts(x1, row_tok, block_e, n_used, w_gate, w_up, w_down):
    n_blocks = row_tok.shape[0]
    grid_spec = pltpu.PrefetchScalarGridSpec(
        num_scalar_prefetch=2,
        grid=(n_blocks,),
        in_specs=[
            pl.BlockSpec(memory_space=pl.ANY),
            pl.BlockSpec(memory_space=pl.ANY),
            pl.BlockSpec((None, D_MODEL, MOE_FF), lambda i, be, nu: (be[i], 0, 0)),
            pl.BlockSpec((None, D_MODEL, MOE_FF), lambda i, be, nu: (be[i], 0, 0)),
            pl.BlockSpec((None, MOE_FF, D_MODEL), lambda i, be, nu: (be[i], 0, 0)),
        ],
        out_specs=pl.BlockSpec((MOE_BLOCK * ROW_TILES, LANES), lambda i, be, nu: (i, 0)),
        scratch_shapes=[
            pltpu.SMEM((MOE_BLOCK,), jnp.int32),
            pltpu.SMEM((MOE_BLOCK,), jnp.int32),
            pltpu.VMEM((MOE_BLOCK * ROW_TILES, LANES), F32),
            pltpu.VMEM((MOE_BLOCK * ROW_TILES, LANES), F32),
            pltpu.SemaphoreType.DMA((2,)),
            pltpu.SemaphoreType.DMA((2,)),
        ],
    )
    return pl.pallas_call(
        _moe_kernel,
        out_shape=jax.ShapeDtypeStruct((n_blocks * MOE_BLOCK * ROW_TILES, LANES), F32),
        grid_spec=grid_spec,
        compiler_params=_params("arbitrary"),
        name="moe_experts",
    )(block_e, n_used, row_tok, x1, w_gate, w_up, w_down)


COMB_TM = 256


def _combine_kernel(idx_hbm, y_hbm, ew_ref, x1_ref, p_ref, lg_ref, lb_ref, wpg_ref, wpp_ref, out_ref,
                    idx0, idx1, ybuf0, ybuf1, ffn_ref, isem, gsem):
    i = pl.program_id(0)
    tm = out_ref.shape[0]
    ybuf = (ybuf0, ybuf1)
    _gather_pipeline(i, pl.num_programs(0), idx_hbm, y_hbm, (idx0, idx1), ybuf, isem, gsem)

    def weighted_sum(s):
        _gather_wait(s, y_hbm, ybuf, gsem)
        ew = ew_ref[...]
        ffn_ref[...] = (_load_row_tiles(ybuf[s], 0, tm) * ew[:, 0:1]
                        + _load_row_tiles(ybuf[s], tm, tm) * ew[:, 1:2])

    _on_slot(i % 2, weighted_sum)
    x1 = _load_row_tiles(x1_ref, 0, tm)
    x2 = _layer_norm(DEEPNORM_ALPHA * x1 + ffn_ref[...], lg_ref[...], lb_ref[...])
    gate = _sigmoid(jnp.dot(x2.astype(BF16), wpg_ref[...], preferred_element_type=F32))
    emb = jnp.dot(p_ref[...].astype(BF16), wpp_ref[...], preferred_element_type=F32)
    out_ref[...] = x2 + gate * emb


def _combine(y_rows, dest_blocks, e_w, x1, p2d, ln_g, ln_b, ple_w_gate, ple_w_proj):
    t = x1.shape[0] // ROW_TILES
    tm = dest_blocks.shape[1] // MOE_TOP_K
    const = lambda i: (0, 0)
    row = lambda i: (i, 0)
    return pl.pallas_call(
        _combine_kernel,
        out_shape=jax.ShapeDtypeStruct((t, D_MODEL), F32),
        grid=(t // tm,),
        in_specs=[
            pl.BlockSpec(memory_space=pl.ANY),
            pl.BlockSpec(memory_space=pl.ANY),
            pl.BlockSpec((tm, MOE_TOP_K), row),
            pl.BlockSpec((tm * ROW_TILES, LANES), row),
            pl.BlockSpec((tm, PLE_DIM), row),
            pl.BlockSpec((1, D_MODEL), const),
            pl.BlockSpec((1, D_MODEL), const),
            pl.BlockSpec((D_MODEL, D_MODEL), const),
            pl.BlockSpec((PLE_DIM, D_MODEL), const),
        ],
        out_specs=pl.BlockSpec((tm, D_MODEL), row),
        scratch_shapes=[
            pltpu.SMEM((MOE_TOP_K * tm,), jnp.int32),
            pltpu.SMEM((MOE_TOP_K * tm,), jnp.int32),
            pltpu.VMEM((MOE_TOP_K * tm * ROW_TILES, LANES), F32),
            pltpu.VMEM((MOE_TOP_K * tm * ROW_TILES, LANES), F32),
            pltpu.VMEM((tm, D_MODEL), F32),
            pltpu.SemaphoreType.DMA((2,)),
            pltpu.SemaphoreType.DMA((2,)),
        ],
        compiler_params=_params("arbitrary"),
        name="moe_combine",
    )(dest_blocks, y_rows, e_w, x1, p2d, ln_g[None, :], ln_b[None, :], ple_w_gate.astype(BF16),
      ple_w_proj.astype(BF16))


def _route(logits):
    t = logits.shape[0]
    tk = t * MOE_TOP_K
    g_logits = logits[:, :MOE_GROUPS]
    e_logits = logits[:, MOE_GROUPS:MOE_GROUPS + MOE_EXPERTS].reshape(t, MOE_GROUPS, MOE_EXPERTS_PER_GROUP)
    g_prob = jax.nn.softmax(g_logits, axis=-1)
    g_idx = jnp.argmax(g_logits, axis=-1).astype(jnp.int32)
    g_w = jnp.take_along_axis(g_prob, g_idx[:, None], axis=1)
    e_in = jnp.take_along_axis(e_logits, g_idx[:, None, None], axis=1)[:, 0]
    top_v, top_i = lax.top_k(e_in, MOE_TOP_K)
    e_w = jax.nn.softmax(top_v, axis=-1) * g_w
    e_id = g_idx[:, None] * MOE_EXPERTS_PER_GROUP + top_i.astype(jnp.int32)
    flat_e = e_id.reshape(tk)

    onehot = (flat_e[:, None] == jnp.arange(MOE_EXPERTS, dtype=jnp.int32)[None, :]).astype(jnp.int32)
    csum = jnp.cumsum(onehot, axis=0)
    rank = jnp.sum(onehot * csum, axis=1) - 1
    counts = csum[-1]
    padded = ((counts + MOE_BLOCK - 1) // MOE_BLOCK) * MOE_BLOCK
    pends = jnp.cumsum(padded)
    pstarts = pends - padded
    dest = pstarts[flat_e] + rank

    n_rows = tk + MOE_EXPERTS * MOE_BLOCK
    n_blocks = n_rows // MOE_BLOCK
    flat_tok = jnp.arange(tk, dtype=jnp.int32) // MOE_TOP_K
    row_tok = jnp.zeros((n_rows,), jnp.int32).at[dest].set(flat_tok, unique_indices=True)
    block_e = jnp.searchsorted(pends, jnp.arange(n_blocks, dtype=jnp.int32) * MOE_BLOCK, side="right")
    block_e = jnp.minimum(block_e, MOE_EXPERTS - 1).astype(jnp.int32)
    n_used = (pends[-1] // MOE_BLOCK).astype(jnp.int32).reshape(1)
    return e_w, dest.reshape(t, MOE_TOP_K), row_tok.reshape(n_blocks, MOE_BLOCK), block_e, n_used


def _layer(x2d, p2d, cos_t, sin_t, bsz, seq, w_in, b_gate, ssd_conv_w, ssd_conv_b, ssd_dt_bias, ssd_a_log,
           ssd_d, ssd_norm_w, ssd_w_out, conf_dw_w, conf_dw_b, conf_ln_g, conf_ln_b, conf_w_out, attn_sinks,
           attn_w_out, w_out, ln1_g, ln1_b, moe_w_group, moe_b_group, moe_w_expert, moe_b_expert, moe_w_gate,
           moe_w_up, moe_w_down, ln2_g, ln2_b, ple_w_gate, ple_w_proj):
    t = bsz * seq
    w_main = jnp.concatenate([w_in[:, :DT_COL_ORIG], w_in[:, DT_COL_ORIG + SSD_HEADS:]], axis=1)
    w_main = jnp.concatenate([w_main[:, :COL_Q], w_main[:, COL_Q:COL_V][:, _pair_interleave(COL_V - COL_Q)],
                              w_main[:, COL_V:]], axis=1).astype(BF16)
    w_dt = jnp.pad(w_in[:, DT_COL_ORIG:DT_COL_ORIG + SSD_HEADS], ((0, 0), (0, LANES - SSD_HEADS))).astype(BF16)
    h, dt_raw = _in_projection(x2d, w_main, w_dt)

    y_ssd = _ssd_mixer(h, dt_raw, bsz, seq, ssd_conv_w, ssd_conv_b, ssd_dt_bias, ssd_a_log, ssd_d, ssd_norm_w,
                       ssd_w_out)
    y_conf = _conformer(h, bsz, seq, conf_dw_w, conf_dw_b, conf_ln_g, conf_ln_b, conf_w_out)
    y_att = _attention(h, cos_t, sin_t, bsz, seq, attn_sinks, attn_w_out)

    n_router = MOE_GROUPS + MOE_EXPERTS
    w_router = jnp.pad(jnp.concatenate([moe_w_group, moe_w_expert], axis=1), ((0, 0), (0, LANES - n_router)))
    b_router = jnp.pad(jnp.concatenate([moe_b_group, moe_b_expert]), (0, LANES - n_router))[None, :]
    x1, logits = _merge(h, b_gate, y_ssd, y_conf, y_att, x2d, w_out, ln1_g, ln1_b, w_router, b_router)

    e_w, dest, row_tok, block_e, n_used = _route(logits)
    y_rows = _moe_experts(x1, row_tok, block_e, n_used, moe_w_gate.astype(BF16), moe_w_up.astype(BF16),
                          moe_w_down.astype(BF16))
    tm = min(COMB_TM, t)
    dest_blocks = dest.reshape(t // tm, tm, MOE_TOP_K).transpose(0, 2, 1).reshape(t // tm, MOE_TOP_K * tm)
    return _combine(y_rows, dest_blocks, e_w, x1, p2d, ln2_g, ln2_b, ple_w_gate, ple_w_proj)


def kernel(x, p, positions, w_in, b_gate, ssd_conv_w, ssd_conv_b, ssd_dt_bias, ssd_a_log, ssd_d, ssd_norm_w, ssd_w_out, conf_dw_w, conf_dw_b, conf_ln_g, conf_ln_b, conf_w_out, attn_sinks, attn_w_out, w_out, ln1_g, ln1_b, moe_w_group, moe_b_group, moe_w_expert, moe_b_expert, moe_w_gate, moe_w_up, moe_w_down, ln2_g, ln2_b, ple_w_gate, ple_w_proj):
    bsz, seq, d = x.shape
    t = bsz * seq
    cos_t, sin_t = _rope_tables(positions)
    x2d = x.reshape(t, d)
    per_layer = (w_in, b_gate, ssd_conv_w, ssd_conv_b, ssd_dt_bias, ssd_a_log, ssd_d, ssd_norm_w, ssd_w_out,
                 conf_dw_w, conf_dw_b, conf_ln_g, conf_ln_b, conf_w_out, attn_sinks, attn_w_out, w_out, ln1_g,
                 ln1_b, moe_w_group, moe_b_group, moe_w_expert, moe_b_expert, moe_w_gate, moe_w_up, moe_w_down,
                 ln2_g, ln2_b, ple_w_gate, ple_w_proj)
    for layer in range(w_in.shape[0]):
        x2d = _layer(x2d, p[layer].reshape(t, -1), cos_t, sin_t, bsz, seq, *(w[layer] for w in per_layer))
    return x2d.reshape(bsz, seq, d)
```

```python
import functools

import jax
import jax.numpy as jnp
from jax import lax
from jax.experimental import pallas as pl
from jax.experimental.pallas import tpu as pltpu

F32 = jnp.float32
BF16 = jnp.bfloat16

D_MODEL = 1024
N_LAYERS = 2
PLE_DIM = 256
SSD_HEADS = 16
SSD_HEAD_DIM = 64
SSD_INNER = SSD_HEADS * SSD_HEAD_DIM
SSD_GROUPS = 4
SSD_STATE = 128
SSD_CONV = 4
SSD_CHUNK = 128
SSD_XBC = SSD_INNER + 2 * SSD_GROUPS * SSD_STATE
CONF_CH = 1024
CONF_KERNEL = 31
ATT_Q_HEADS = 16
ATT_KV_HEADS = 4
ATT_HEAD_DIM = 64
ATT_BLOCK = 128
ROPE_THETA = 10000.0
MOE_GROUPS = 4
MOE_EXPERTS_PER_GROUP = 8
MOE_EXPERTS = MOE_GROUPS * MOE_EXPERTS_PER_GROUP
MOE_TOP_K = 2
MOE_FF = 512
MOE_BLOCK = 256
DEEPNORM_ALPHA = (2 * N_LAYERS) ** 0.25
LN_EPS = 1e-5

LANES = 128
SUBLANES = 8
VMEM_LIMIT_BYTES = 56 * 1024 * 1024

COL_GATES = 0
COL_Z = 3 * D_MODEL
COL_XBC = COL_Z + SSD_INNER
COL_U = COL_XBC + SSD_XBC
COL_Q = COL_U + 2 * CONF_CH
COL_K = COL_Q + ATT_Q_HEADS * ATT_HEAD_DIM
COL_V = COL_K + ATT_KV_HEADS * ATT_HEAD_DIM
H_WIDTH = COL_V + ATT_KV_HEADS * ATT_HEAD_DIM
DT_COL_ORIG = 3 * D_MODEL + SSD_INNER + SSD_XBC


def _params(*semantics):
    return pltpu.CompilerParams(dimension_semantics=semantics, vmem_limit_bytes=VMEM_LIMIT_BYTES)


def _sigmoid(x):
    return jax.nn.sigmoid(x)


def _silu(x):
    return x * jax.nn.sigmoid(x)


def _layer_norm(x, g, b):
    mu = jnp.mean(x, axis=-1, keepdims=True)
    xc = x - mu
    var = jnp.mean(xc * xc, axis=-1, keepdims=True)
    return xc * lax.rsqrt(var + LN_EPS) * g + b


ROW_TILES = D_MODEL // LANES


def _store_row_tiles(ref, x):
    rows = x.shape[0]
    for k in range(ROW_TILES):
        ref[pl.ds(k, rows, stride=ROW_TILES), :] = x[:, k * LANES:(k + 1) * LANES]


def _load_row_tiles(ref, first_row, rows):
    return jnp.concatenate(
        [ref[pl.ds(first_row * ROW_TILES + k, rows, stride=ROW_TILES), :] for k in range(ROW_TILES)], axis=1)


IN_TM = 1024
IN_TN = 512


def _inproj_kernel(x_ref, w_ref, wdt_ref, h_ref, dt_ref, xb_ref):
    @pl.when(pl.program_id(1) == 0)
    def _():
        xb = x_ref[...].astype(BF16)
        xb_ref[...] = xb
        dt_ref[...] = jnp.dot(xb, wdt_ref[...], preferred_element_type=F32)

    h_ref[...] = jnp.dot(xb_ref[...], w_ref[...], preferred_element_type=F32).astype(BF16)


def _in_projection(x2d, w_main, w_dt):
    t = x2d.shape[0]
    tm = min(IN_TM, t)
    return pl.pallas_call(
        _inproj_kernel,
        out_shape=(jax.ShapeDtypeStruct((t, H_WIDTH), BF16), jax.ShapeDtypeStruct((t, LANES), F32)),
        grid=(t // tm, H_WIDTH // IN_TN),
        in_specs=[
            pl.BlockSpec((tm, D_MODEL), lambda i, j: (i, 0)),
            pl.BlockSpec((D_MODEL, IN_TN), lambda i, j: (0, j)),
            pl.BlockSpec((D_MODEL, LANES), lambda i, j: (0, 0)),
        ],
        out_specs=(
            pl.BlockSpec((tm, IN_TN), lambda i, j: (i, j)),
            pl.BlockSpec((tm, LANES), lambda i, j: (i, 0)),
        ),
        scratch_shapes=[pltpu.VMEM((tm, D_MODEL), BF16)],
        compiler_params=_params("parallel", "arbitrary"),
        name="in_projection",
    )(x2d, w_main, w_dt)


def _rope_kernel(pos_ref, inv_ref, sign_ref, cos_ref, sin_ref):
    ang = pos_ref[...].astype(F32) * inv_ref[...]
    cos_ref[...] = jnp.cos(ang)
    sin_ref[...] = jnp.sin(ang) * sign_ref[...]


def _rope_tables(positions):
    t = positions.size
    tm = min(1024, t)
    half = ATT_HEAD_DIM // 2
    inv_freq = ROPE_THETA ** (-jnp.arange(half, dtype=F32) / half)
    inv = jnp.tile(inv_freq, LANES // half)[None, :]
    sign = jnp.concatenate([-jnp.ones((LANES // 2,), F32), jnp.ones((LANES // 2,), F32)])[None, :]
    return pl.pallas_call(
        _rope_kernel,
        out_shape=(jax.ShapeDtypeStruct((t, LANES), F32), jax.ShapeDtypeStruct((t, LANES), F32)),
        grid=(t // tm,),
        in_specs=[
            pl.BlockSpec((tm, 1), lambda i: (i, 0)),
            pl.BlockSpec((1, LANES), lambda i: (0, 0)),
            pl.BlockSpec((1, LANES), lambda i: (0, 0)),
        ],
        out_specs=(pl.BlockSpec((tm, LANES), lambda i: (i, 0)), pl.BlockSpec((tm, LANES), lambda i: (i, 0))),
        compiler_params=_params("parallel"),
        name="rope_tables",
    )(positions.reshape(t, 1), inv, sign)


SSD_PAIRS = SSD_HEADS // 2
SSD_HALO = SUBLANES


def _ssd_kernel(xbc_ref, z_ref, dt_ref, cw_ref, cb_ref, dtb_ref, alog_ref, dsk_ref, nw_ref, wp_ref,
                y_ref, state_ref, ext_ref, ybuf_ref):
    L = SSD_CHUNK
    c = pl.program_id(1)

    @pl.when(c == 0)
    def _():
        state_ref[...] = jnp.zeros(state_ref.shape, F32)
        ext_ref[0:SSD_HALO, :] = jnp.zeros((SSD_HALO, SSD_XBC), F32)

    @pl.when(c > 0)
    def _():
        ext_ref[0:SSD_HALO, :] = ext_ref[L:L + SSD_HALO, :]

    ext_ref[SSD_HALO:SSD_HALO + L, :] = xbc_ref[...].astype(F32)

    acc = jnp.broadcast_to(cb_ref[...], (L, SSD_XBC))
    for k in range(SSD_CONV):
        off = SSD_HALO - (SSD_CONV - 1) + k
        acc = acc + cw_ref[k:k + 1, :] * ext_ref[off:off + L, :]
    act = _silu(acc)
    xs = act[:, :SSD_INNER]
    bm = act[:, SSD_INNER:SSD_INNER + SSD_GROUPS * SSD_STATE]
    cm = act[:, SSD_INNER + SSD_GROUPS * SSD_STATE:]

    x_dt = dt_ref[...] + dtb_ref[...]
    dt = jnp.maximum(x_dt, 0.0) + jnp.log1p(jnp.exp(-jnp.abs(x_dt)))
    a = -jnp.exp(alog_ref[...])
    row = lax.broadcasted_iota(jnp.int32, (L, L), 0)
    col = lax.broadcasted_iota(jnp.int32, (L, L), 1)
    causal = row >= col
    tril = causal.astype(F32)
    cs = jnp.dot(tril, dt * a, preferred_element_type=F32, precision=lax.Precision.HIGHEST)
    cs_t = cs.T
    lo = col < SSD_HEAD_DIM

    for g in range(SSD_GROUPS):
        bm_g = bm[:, g * SSD_STATE:(g + 1) * SSD_STATE].astype(BF16)
        cm_g = cm[:, g * SSD_STATE:(g + 1) * SSD_STATE].astype(BF16)
        cb_g = lax.dot_general(cm_g, bm_g, (((1,), (1,)), ((), ())), preferred_element_type=F32)
        pairs_per_group = SSD_PAIRS // SSD_GROUPS
        for jj in range(pairs_per_group):
            j = g * pairs_per_group + jj
            h0, h1 = 2 * j, 2 * j + 1
            sl = slice(j * LANES, (j + 1) * LANES)
            col0 = jnp.broadcast_to(cs[:, h0:h0 + 1], (L, L))
            col1 = jnp.broadcast_to(cs[:, h1:h1 + 1], (L, L))
            dec0 = jnp.where(causal, jnp.exp(col0 - cs_t[h0:h0 + 1, :]), 0.0)
            dec1 = jnp.where(causal, jnp.exp(col1 - cs_t[h1:h1 + 1, :]), 0.0)
            m = jnp.concatenate([cb_g * dec0, cb_g * dec1], axis=1).astype(BF16)
            dt_p = jnp.where(lo, jnp.broadcast_to(dt[:, h0:h0 + 1], (L, L)),
                             jnp.broadcast_to(dt[:, h1:h1 + 1], (L, L)))
            xs_p = xs[:, sl]
            xdt = xs_p * dt_p
            x2 = jnp.concatenate([jnp.where(lo, xdt, 0.0), jnp.where(lo, 0.0, xdt)], axis=0).astype(BF16)
            y_diag = jnp.dot(m, x2, preferred_element_type=F32)
            cs_p = jnp.where(lo, col0, col1)
            st = state_ref[j]
            y_off = jnp.exp(cs_p) * jnp.dot(cm_g, st.astype(BF16), preferred_element_type=F32)
            last = cs_p[L - 1:L, :]
            xdt_end = (xdt * jnp.exp(last - cs_p)).astype(BF16)
            new_st = lax.dot_general(bm_g, xdt_end, (((0,), (0,)), ((), ())), preferred_element_type=F32)
            state_ref[j] = st * jnp.exp(last) + new_st
            ybuf_ref[:, sl] = y_diag + y_off + xs_p * dsk_ref[:, sl]

    y = ybuf_ref[...] * _silu(z_ref[...].astype(F32))
    gw = SSD_INNER // SSD_GROUPS
    parts = []
    for g in range(SSD_GROUPS):
        yg = y[:, g * gw:(g + 1) * gw]
        parts.append(yg * lax.rsqrt(jnp.mean(yg * yg, axis=-1, keepdims=True) + LN_EPS))
    yn = jnp.concatenate(parts, axis=1) * nw_ref[...]
    y_ref[...] = jnp.dot(yn.astype(BF16), wp_ref[...], preferred_element_type=F32).astype(BF16)


def _ssd_mixer(h, dt_raw, bsz, seq, conv_w, conv_b, dt_bias, a_log, d_skip, norm_w, w_proj):
    t = bsz * seq
    L = SSD_CHUNK
    nc = seq // L
    pad = LANES - SSD_HEADS
    dtb = jnp.pad(dt_bias, (0, pad))[None, :]
    alog = jnp.pad(a_log, (0, pad))[None, :]
    dsk = jnp.repeat(d_skip, SSD_HEAD_DIM)[None, :]
    const = lambda b, c: (0, 0)
    return pl.pallas_call(
        _ssd_kernel,
        out_shape=jax.ShapeDtypeStruct((t, D_MODEL), BF16),
        grid=(bsz, nc),
        in_specs=[
            pl.BlockSpec((L, SSD_XBC), lambda b, c: (b * nc + c, COL_XBC // SSD_XBC)),
            pl.BlockSpec((L, SSD_INNER), lambda b, c: (b * nc + c, COL_Z // SSD_INNER)),
            pl.BlockSpec((L, LANES), lambda b, c: (b * nc + c, 0)),
            pl.BlockSpec((SSD_CONV, SSD_XBC), const),
            pl.BlockSpec((1, SSD_XBC), const),
            pl.BlockSpec((1, LANES), const),
            pl.BlockSpec((1, LANES), const),
            pl.BlockSpec((1, SSD_INNER), const),
            pl.BlockSpec((1, SSD_INNER), const),
            pl.BlockSpec((SSD_INNER, D_MODEL), const),
        ],
        out_specs=pl.BlockSpec((L, D_MODEL), lambda b, c: (b * nc + c, 0)),
        scratch_shapes=[
            pltpu.VMEM((SSD_PAIRS, SSD_STATE, LANES), F32),
            pltpu.VMEM((SSD_HALO + L, SSD_XBC), F32),
            pltpu.VMEM((L, SSD_INNER), F32),
        ],
        compiler_params=_params("parallel", "arbitrary"),
        name="ssd_mixer",
    )(h, h, dt_raw, conv_w, conv_b[None, :], dtb, alog, dsk, norm_w[None, :], w_proj.astype(BF16))


CONF_TS = 256
CONF_HALO = 32
CONF_ROW_BLK = 64
CONF_COL_BLK = 256


def _conf_kernel(u_ref, dw_ref, db_ref, g_ref, b_ref, wp_ref, y_ref, ext_ref, conv_ref):
    ts = u_ref.shape[0]
    s = pl.program_id(1)

    @pl.when(s == 0)
    def _():
        ext_ref[:, 0:CONF_HALO, :] = jnp.zeros((SUBLANES, CONF_HALO, CONF_CH), F32)
        ext_ref[:, ts + CONF_HALO - SUBLANES:ts + CONF_HALO, :] = jnp.zeros((SUBLANES, SUBLANES, CONF_CH), F32)

    @pl.when(s > 0)
    def _():
        ext_ref[:, 0:CONF_HALO, :] = ext_ref[:, ts:ts + CONF_HALO, :]

    u = u_ref[...].astype(F32)
    glu = u[:, :CONF_CH] * _sigmoid(u[:, CONF_CH:])
    for b in range(SUBLANES):
        ext_ref[b, CONF_HALO - b:CONF_HALO - b + ts, :] = glu

    for r0 in range(0, ts, CONF_ROW_BLK):
        for c0 in range(0, CONF_CH, CONF_COL_BLK):
            cols = slice(c0, c0 + CONF_COL_BLK)
            acc = jnp.broadcast_to(db_ref[:, cols], (CONF_ROW_BLK, CONF_COL_BLK))
            for k in range(CONF_KERNEL):
                shift = CONF_HALO - (CONF_KERNEL - 1) + k
                off = shift - shift % SUBLANES + r0
                acc = acc + dw_ref[k:k + 1, cols] * ext_ref[shift % SUBLANES, off:off + CONF_ROW_BLK, cols]
            conv_ref[r0:r0 + CONF_ROW_BLK, cols] = acc

    hn = _silu(_layer_norm(conv_ref[...], g_ref[...], b_ref[...]))
    y_ref[...] = jnp.dot(hn.astype(BF16), wp_ref[...], preferred_element_type=F32).astype(BF16)


def _conformer(h, bsz, seq, dw_w, dw_b, ln_g, ln_b, w_proj):
    t = bsz * seq
    ts = min(CONF_TS, seq)
    nt = seq // ts
    const = lambda b, s: (0, 0)
    dw = jnp.pad(dw_w, ((0, CONF_HALO - CONF_KERNEL), (0, 0)))
    return pl.pallas_call(
        _conf_kernel,
        out_shape=jax.ShapeDtypeStruct((t, D_MODEL), BF16),
        grid=(bsz, nt),
        in_specs=[
            pl.BlockSpec((ts, 2 * CONF_CH), lambda b, s: (b * nt + s, COL_U // (2 * CONF_CH))),
            pl.BlockSpec((CONF_HALO, CONF_CH), const),
            pl.BlockSpec((1, CONF_CH), const),
            pl.BlockSpec((1, CONF_CH), const),
            pl.BlockSpec((1, CONF_CH), const),
            pl.BlockSpec((CONF_CH, D_MODEL), const),
        ],
        out_specs=pl.BlockSpec((ts, D_MODEL), lambda b, s: (b * nt + s, 0)),
        scratch_shapes=[
            pltpu.VMEM((SUBLANES, CONF_HALO + ts, CONF_CH), F32),
            pltpu.VMEM((ts, CONF_CH), F32),
        ],
        compiler_params=_params("parallel", "arbitrary"),
        name="conformer_conv",
    )(h, dw, dw_b[None, :], ln_g[None, :], ln_b[None, :], w_proj.astype(BF16))


def _pair_interleave(width):
    half = ATT_HEAD_DIM // 2
    j = jnp.arange(width, dtype=jnp.int32)
    block, lane = j // LANES, j % LANES
    chunk, within = lane // half, lane % half
    return block * LANES + (chunk % 2) * ATT_HEAD_DIM + (chunk // 2) * half + within


def _attn_kernel(q_ref, kc_ref, kp_ref, vc_ref, vp_ref, cosc_ref, sinc_ref, cosp_ref, sinp_ref, sink_ref,
                 wp_ref, y_ref, obuf_ref):
    blk = ATT_BLOCK
    i = pl.program_id(1)
    rep = ATT_Q_HEADS // ATT_KV_HEADS
    quarter = ATT_HEAD_DIM // 2
    scale = ATT_HEAD_DIM ** -0.5

    def lane_masks(rows):
        lane = lax.broadcasted_iota(jnp.int32, (rows, LANES), 1)
        return (lane % ATT_HEAD_DIM) < quarter, lane < ATT_HEAD_DIM

    def rope(x, cos, sin):
        return x * cos + pltpu.roll(x, ATT_HEAD_DIM, 1) * sin

    first_q, lo_q = lane_masks(blk)
    first_k, lo_k = lane_masks(2 * blk)

    cos_q, sin_q = cosc_ref[...], sinc_ref[...]
    cos_k = jnp.concatenate([cosp_ref[...], cosc_ref[...]], axis=0)
    sin_k = jnp.concatenate([sinp_ref[...], sinc_ref[...]], axis=0)
    k = jnp.concatenate([kp_ref[...], kc_ref[...]], axis=0).astype(F32)
    v = jnp.concatenate([vp_ref[...], vc_ref[...]], axis=0).astype(F32)

    qi = lax.broadcasted_iota(jnp.int32, (blk, 2 * blk), 0)
    kj = lax.broadcasted_iota(jnp.int32, (blk, 2 * blk), 1)
    low = jnp.where(i == 0, blk - 1, qi)
    mask = (kj <= qi + blk) & (kj > low)

    for kh in range(ATT_KV_HEADS):
        kb = rope(k[:, (kh // 2) * LANES:(kh // 2 + 1) * LANES], cos_k, sin_k)
        vb = v[:, (kh // 2) * LANES:(kh // 2 + 1) * LANES]
        if kh % 2 == 0:
            k_dup = jnp.where(first_k, kb, pltpu.roll(kb, quarter, 1))
            v_dup = jnp.where(lo_k, vb, pltpu.roll(vb, ATT_HEAD_DIM, 1))
        else:
            k_dup = jnp.where(first_k, pltpu.roll(kb, LANES - quarter, 1), kb)
            v_dup = jnp.where(lo_k, pltpu.roll(vb, ATT_HEAD_DIM, 1), vb)
        k_dup = k_dup.astype(BF16)
        v_ext = jnp.concatenate([v_dup, jnp.ones_like(v_dup)], axis=1).astype(BF16)

        stacked = []
        for qb in range(rep // 2):
            cols = slice((kh * (rep // 2) + qb) * LANES, (kh * (rep // 2) + qb + 1) * LANES)
            qr = rope(q_ref[:, cols].astype(F32), cos_q, sin_q) * scale
            stacked.append(jnp.where(first_q, qr, 0.0).astype(BF16))
            stacked.append(jnp.where(first_q, 0.0, qr).astype(BF16))
        qs = jnp.concatenate(stacked, axis=0)
        s_all = lax.dot_general(qs, k_dup, (((1,), (1,)), ((), ())), preferred_element_type=F32)

        es, sink_terms = [], []
        for r in range(rep):
            hq = kh * rep + r
            s = jnp.where(mask, s_all[r * blk:(r + 1) * blk], -jnp.inf)
            sink = sink_ref[hq:hq + 1, :]
            mx = jnp.maximum(jnp.broadcast_to(jnp.max(s, axis=-1, keepdims=True), (blk, LANES)), sink)
            sink_terms.append(jnp.exp(sink - mx))
            es.append(jnp.exp(s - jnp.concatenate([mx, mx], axis=1)).astype(BF16))
        o = jnp.dot(jnp.concatenate(es, axis=0), v_ext, preferred_element_type=F32)
        outs = []
        for r in range(rep):
            rows = slice(r * blk, (r + 1) * blk)
            outs.append(o[rows, :LANES] * (1.0 / (o[rows, LANES:] + sink_terms[r])))
        for qb in range(rep // 2):
            cols = slice((kh * (rep // 2) + qb) * LANES, (kh * (rep // 2) + qb + 1) * LANES)
            obuf_ref[:, cols] = jnp.where(lo_q, outs[2 * qb], outs[2 * qb + 1])

    y_ref[...] = jnp.dot(obuf_ref[...].astype(BF16), wp_ref[...], preferred_element_type=F32).astype(BF16)


def _attention(h, cos_t, sin_t, bsz, seq, sinks, w_proj):
    t = bsz * seq
    blk = ATT_BLOCK
    nb = seq // blk
    qw = ATT_Q_HEADS * ATT_HEAD_DIM
    kw = ATT_KV_HEADS * ATT_HEAD_DIM
    cur = lambda b, i: b * nb + i
    prev = lambda b, i: b * nb + jnp.maximum(i - 1, 0)
    const = lambda b, i: (0, 0)
    sink = jnp.broadcast_to(sinks[:, None], (ATT_Q_HEADS, LANES))
    return pl.pallas_call(
        _attn_kernel,
        out_shape=jax.ShapeDtypeStruct((t, D_MODEL), BF16),
        grid=(bsz, nb),
        in_specs=[
            pl.BlockSpec((blk, qw), lambda b, i: (cur(b, i), COL_Q // qw)),
            pl.BlockSpec((blk, kw), lambda b, i: (cur(b, i), COL_K // kw)),
            pl.BlockSpec((blk, kw), lambda b, i: (prev(b, i), COL_K // kw)),
            pl.BlockSpec((blk, kw), lambda b, i: (cur(b, i), COL_V // kw)),
            pl.BlockSpec((blk, kw), lambda b, i: (prev(b, i), COL_V // kw)),
            pl.BlockSpec((blk, LANES), lambda b, i: (cur(b, i), 0)),
            pl.BlockSpec((blk, LANES), lambda b, i: (cur(b, i), 0)),
            pl.BlockSpec((blk, LANES), lambda b, i: (prev(b, i), 0)),
            pl.BlockSpec((blk, LANES), lambda b, i: (prev(b, i), 0)),
            pl.BlockSpec((ATT_Q_HEADS, LANES), const),
            pl.BlockSpec((qw, D_MODEL), const),
        ],
        out_specs=pl.BlockSpec((blk, D_MODEL), lambda b, i: (cur(b, i), 0)),
        scratch_shapes=[pltpu.VMEM((blk, qw), F32)],
        compiler_params=_params("parallel", "arbitrary"),
        name="swa_attention",
    )(h, h, h, h, h, cos_t, sin_t, cos_t, sin_t, sink, w_proj.astype(BF16))


MERGE_TM = 512


def _merge_kernel(g0_ref, g1_ref, g2_ref, bg_ref, y0_ref, y1_ref, y2_ref, x_ref, wo_ref, lg_ref, lb_ref,
                  wr_ref, br_ref, x1_ref, logit_ref):
    mixed = None
    for n, (g_ref, y_ref) in enumerate(((g0_ref, y0_ref), (g1_ref, y1_ref), (g2_ref, y2_ref))):
        term = _sigmoid(g_ref[...].astype(F32) + bg_ref[n:n + 1, :]) * y_ref[...].astype(F32)
        mixed = term if mixed is None else mixed + term
    r = DEEPNORM_ALPHA * x_ref[...] + jnp.dot(mixed.astype(BF16), wo_ref[...], preferred_element_type=F32)
    x1 = _layer_norm(r, lg_ref[...], lb_ref[...])
    _store_row_tiles(x1_ref, x1)
    logit_ref[...] = jnp.dot(x1, wr_ref[...], preferred_element_type=F32,
                             precision=lax.Precision.HIGHEST) + br_ref[...]


def _merge(h, b_gate, y_ssd, y_conf, y_att, x2d, w_out, ln_g, ln_b, w_router, b_router):
    t = x2d.shape[0]
    tm = min(MERGE_TM, t)
    const = lambda i: (0, 0)
    row = lambda i: (i, 0)
    return pl.pallas_call(
        _merge_kernel,
        out_shape=(jax.ShapeDtypeStruct((t * ROW_TILES, LANES), F32), jax.ShapeDtypeStruct((t, LANES), F32)),
        grid=(t // tm,),
        in_specs=[
            pl.BlockSpec((tm, D_MODEL), lambda i: (i, 0)),
            pl.BlockSpec((tm, D_MODEL), lambda i: (i, 1)),
            pl.BlockSpec((tm, D_MODEL), lambda i: (i, 2)),
            pl.BlockSpec((3, D_MODEL), const),
            pl.BlockSpec((tm, D_MODEL), row),
            pl.BlockSpec((tm, D_MODEL), row),
            pl.BlockSpec((tm, D_MODEL), row),
            pl.BlockSpec((tm, D_MODEL), row),
            pl.BlockSpec((D_MODEL, D_MODEL), const),
            pl.BlockSpec((1, D_MODEL), const),
            pl.BlockSpec((1, D_MODEL), const),
            pl.BlockSpec((D_MODEL, LANES), const),
            pl.BlockSpec((1, LANES), const),
        ],
        out_specs=(pl.BlockSpec((tm * ROW_TILES, LANES), row), pl.BlockSpec((tm, LANES), row)),
        compiler_params=_params("parallel"),
        name="merge_ln_router",
    )(h, h, h, b_gate, y_ssd, y_conf, y_att, x2d, w_out.astype(BF16), ln_g[None, :], ln_b[None, :],
      w_router, b_router)


GATHER_UNROLL = 8


def _on_slot(slot, fn):
    for s in (0, 1):
        @pl.when(slot == s)
        def _(s=s):
            fn(s)


def _gather_pipeline(i, n_steps, idx_hbm, src_hbm, idx_smem, buf, isem, gsem):
    n_rows = buf[0].shape[0] // ROW_TILES
    slot = i % 2

    def idx_copy(step, s):
        return pltpu.make_async_copy(idx_hbm.at[step], idx_smem[s], isem.at[s])

    def issue_rows(s):
        def body(it, carry):
            for u in range(GATHER_UNROLL):
                r = it * GATHER_UNROLL + u
                src_row = pl.multiple_of(idx_smem[s][r] * ROW_TILES, ROW_TILES)
                dst_row = pl.multiple_of(r * ROW_TILES, ROW_TILES)
                pltpu.make_async_copy(src_hbm.at[pl.ds(src_row, ROW_TILES)],
                                      buf[s].at[pl.ds(dst_row, ROW_TILES)], gsem.at[s]).start()
            return carry

        lax.fori_loop(0, n_rows // GATHER_UNROLL, body, 0)

    @pl.when(i == 0)
    def _():
        idx_copy(0, 0).start()
        idx_copy(0, 0).wait()
        issue_rows(0)

        @pl.when(n_steps > 1)
        def _():
            idx_copy(1, 1).start()

    @pl.when(i + 1 < n_steps)
    def _():
        def prefetch(s):
            idx_copy(i + 1, 1 - s).wait()
            issue_rows(1 - s)

            @pl.when(i + 2 < n_steps)
            def _():
                idx_copy(i + 2, s).start()

        _on_slot(slot, prefetch)


def _gather_wait(s, src_hbm, buf, gsem):
    pltpu.make_async_copy(src_hbm.at[pl.ds(0, buf[s].shape[0])], buf[s], gsem.at[s]).wait()


def _dispatch_kernel(idx_hbm, x1_hbm, init_hbm, rows_hbm, idx0, idx1, isem, dsem):
    del init_hbm
    i = pl.program_id(0)
    n_steps = pl.num_programs(0)
    idx = (idx0, idx1)
    n_idx = idx0.shape[0]
    tm = n_idx // MOE_TOP_K

    def idx_copy(step, s):
        return pltpu.make_async_copy(idx_hbm.at[step], idx[s], isem.at[s])

    def wait_rows(s):
        pltpu.make_async_copy(x1_hbm.at[pl.ds(0, n_idx * ROW_TILES)], rows_hbm.at[pl.ds(0, n_idx * ROW_TILES)],
                              dsem.at[s]).wait()

    @pl.when(i == 0)
    def _():
        idx_copy(0, 0).start()

    def step(s):
        idx_copy(i, s).wait()

        @pl.when(i + 1 < n_steps)
        def _():
            idx_copy(i + 1, 1 - s).start()

        def body(it, carry):
            for u in range(GATHER_UNROLL):
                r = it * GATHER_UNROLL + u
                src_row = pl.multiple_of((i * tm + r % tm) * ROW_TILES, ROW_TILES)
                dst_row = pl.multiple_of(idx[s][r] * ROW_TILES, ROW_TILES)
                pltpu.make_async_copy(x1_hbm.at[pl.ds(src_row, ROW_TILES)],
                                      rows_hbm.at[pl.ds(dst_row, ROW_TILES)], dsem.at[s]).start()
            return carry

        lax.fori_loop(0, n_idx // GATHER_UNROLL, body, 0)

        @pl.when(i > 0)
        def _():
            wait_rows(1 - s)

        @pl.when(i == n_steps - 1)
        def _():
            wait_rows(s)

    _on_slot(i % 2, step)


def _moe_dispatch(x1, dest_blocks, n_rows):
    n_steps, n_idx = dest_blocks.shape
    any_spec = pl.BlockSpec(memory_space=pl.ANY)
    return pl.pallas_call(
        _dispatch_kernel,
        out_shape=jax.ShapeDtypeStruct((n_rows * ROW_TILES, LANES), F32),
        grid=(n_steps,),
        in_specs=[any_spec, any_spec, any_spec],
        out_specs=any_spec,
        scratch_shapes=[
            pltpu.SMEM((n_idx,), jnp.int32),
            pltpu.SMEM((n_idx,), jnp.int32),
            pltpu.SemaphoreType.DMA((2,)),
            pltpu.SemaphoreType.DMA((2,)),
        ],
        input_output_aliases={2: 0},
        compiler_params=_params("arbitrary"),
        name="moe_dispatch",
    )(dest_blocks, x1, jnp.zeros((n_rows * ROW_TILES, LANES), F32))


def _moe_kernel(be_ref, nu_ref, x_ref, wg_ref, wu_ref, wd_ref, y_ref):
    del be_ref
    i = pl.program_id(0)

    @pl.when(i < nu_ref[0])
    def _():
        xb = _load_row_tiles(x_ref, 0, MOE_BLOCK).astype(BF16)
        gate = jnp.dot(xb, wg_ref[...], preferred_element_type=F32)
        up = jnp.dot(xb, wu_ref[...], preferred_element_type=F32)
        hmid = (_silu(gate) * up).astype(BF16)
        _store_row_tiles(y_ref, jnp.dot(hmid, wd_ref[...], preferred_element_type=F32))

    @pl.when(i >= nu_ref[0])
    def _():
        y_ref[...] = jnp.zeros(y_ref.shape, F32)


def _moe_experts(x_rows, block_e, n_used, w_gate, w_up, w_down):
    n_blocks = block_e.shape[0]
    grid_spec = pltpu.PrefetchScalarGridSpec(
        num_scalar_prefetch=2,
        grid=(n_blocks,),
        in_specs=[
            pl.BlockSpec((MOE_BLOCK * ROW_TILES, LANES), lambda i, be, nu: (jnp.minimum(i, nu[0] - 1), 0)),
            pl.BlockSpec((None, D_MODEL, MOE_FF), lambda i, be, nu: (be[i], 0, 0)),
            pl.BlockSpec((None, D_MODEL, MOE_FF), lambda i, be, nu: (be[i], 0, 0)),
            pl.BlockSpec((None, MOE_FF, D_MODEL), lambda i, be, nu: (be[i], 0, 0)),
        ],
        out_specs=pl.BlockSpec((MOE_BLOCK * ROW_TILES, LANES), lambda i, be, nu: (i, 0)),
    )
    return pl.pallas_call(
        _moe_kernel,
        out_shape=jax.ShapeDtypeStruct((n_blocks * MOE_BLOCK * ROW_TILES, LANES), F32),
        grid_spec=grid_spec,
        compiler_params=_params("arbitrary"),
        name="moe_experts",
    )(block_e, n_used, x_rows, w_gate, w_up, w_down)


COMB_TM = 256


def _combine_kernel(idx_hbm, y_hbm, ew_ref, x1_ref, p_ref, lg_ref, lb_ref, wpg_ref, wpp_ref, out_ref,
                    idx0, idx1, ybuf0, ybuf1, ffn_ref, isem, gsem):
    i = pl.program_id(0)
    tm = out_ref.shape[0]
    ybuf = (ybuf0, ybuf1)
    _gather_pipeline(i, pl.num_programs(0), idx_hbm, y_hbm, (idx0, idx1), ybuf, isem, gsem)

    def weighted_sum(s):
        _gather_wait(s, y_hbm, ybuf, gsem)
        ew = ew_ref[...]
        ffn_ref[...] = (_load_row_tiles(ybuf[s], 0, tm) * ew[:, 0:1]
                        + _load_row_tiles(ybuf[s], tm, tm) * ew[:, 1:2])

    _on_slot(i % 2, weighted_sum)
    x1 = _load_row_tiles(x1_ref, 0, tm)
    x2 = _layer_norm(DEEPNORM_ALPHA * x1 + ffn_ref[...], lg_ref[...], lb_ref[...])
    gate = _sigmoid(jnp.dot(x2.astype(BF16), wpg_ref[...], preferred_element_type=F32))
    emb = jnp.dot(p_ref[...].astype(BF16), wpp_ref[...], preferred_element_type=F32)
    out_ref[...] = x2 + gate * emb


def _combine(y_rows, dest_blocks, e_w, x1, p2d, ln_g, ln_b, ple_w_gate, ple_w_proj):
    t = x1.shape[0] // ROW_TILES
    tm = dest_blocks.shape[1] // MOE_TOP_K
    const = lambda i: (0, 0)
    row = lambda i: (i, 0)
    return pl.pallas_call(
        _combine_kernel,
        out_shape=jax.ShapeDtypeStruct((t, D_MODEL), F32),
        grid=(t // tm,),
        in_specs=[
            pl.BlockSpec(memory_space=pl.ANY),
            pl.BlockSpec(memory_space=pl.ANY),
            pl.BlockSpec((tm, MOE_TOP_K), row),
            pl.BlockSpec((tm * ROW_TILES, LANES), row),
            pl.BlockSpec((tm, PLE_DIM), row),
            pl.BlockSpec((1, D_MODEL), const),
            pl.BlockSpec((1, D_MODEL), const),
            pl.BlockSpec((D_MODEL, D_MODEL), const),
            pl.BlockSpec((PLE_DIM, D_MODEL), const),
        ],
        out_specs=pl.BlockSpec((tm, D_MODEL), row),
        scratch_shapes=[
            pltpu.SMEM((MOE_TOP_K * tm,), jnp.int32),
            pltpu.SMEM((MOE_TOP_K * tm,), jnp.int32),
            pltpu.VMEM((MOE_TOP_K * tm * ROW_TILES, LANES), F32),
            pltpu.VMEM((MOE_TOP_K * tm * ROW_TILES, LANES), F32),
            pltpu.VMEM((tm, D_MODEL), F32),
            pltpu.SemaphoreType.DMA((2,)),
            pltpu.SemaphoreType.DMA((2,)),
        ],
        compiler_params=_params("arbitrary"),
        name="moe_combine",
    )(dest_blocks, y_rows, e_w, x1, p2d, ln_g[None, :], ln_b[None, :], ple_w_gate.astype(BF16),
      ple_w_proj.astype(BF16))


ROUTE_CHUNK = 256


def _route(logits):
    t = logits.shape[0]
    tk = t * MOE_TOP_K
    g_logits = logits[:, :MOE_GROUPS]
    e_logits = logits[:, MOE_GROUPS:MOE_GROUPS + MOE_EXPERTS].reshape(t, MOE_GROUPS, MOE_EXPERTS_PER_GROUP)
    g_prob = jax.nn.softmax(g_logits, axis=-1)
    g_idx = jnp.argmax(g_logits, axis=-1).astype(jnp.int32)
    g_w = jnp.take_along_axis(g_prob, g_idx[:, None], axis=1)
    e_in = jnp.take_along_axis(e_logits, g_idx[:, None, None], axis=1)[:, 0]
    top_v, top_i = lax.top_k(e_in, MOE_TOP_K)
    e_w = jax.nn.softmax(top_v, axis=-1) * g_w
    e_id = g_idx[:, None] * MOE_EXPERTS_PER_GROUP + top_i.astype(jnp.int32)
    flat_e = e_id.reshape(tk)

    chunk = min(ROUTE_CHUNK, tk)
    onehot = flat_e[:, None] == jnp.arange(MOE_EXPERTS, dtype=jnp.int32)[None, :]
    oh = onehot.reshape(tk // chunk, chunk, MOE_EXPERTS)
    tril = jnp.tril(jnp.ones((chunk, chunk), BF16))
    within = jnp.einsum("ij,cje->cie", tril, oh.astype(BF16), preferred_element_type=F32).astype(jnp.int32)
    chunk_counts = within[:, -1, :]
    chunk_ends = jnp.cumsum(chunk_counts, axis=0)
    csum = within + (chunk_ends - chunk_counts)[:, None, :]
    rank = jnp.sum(jnp.where(oh, csum, 0), axis=-1).reshape(tk) - 1
    counts = chunk_ends[-1]
    padded = ((counts + MOE_BLOCK - 1) // MOE_BLOCK) * MOE_BLOCK
    pends = jnp.cumsum(padded)
    pstarts = pends - padded
    dest = pstarts[flat_e] + rank

    n_rows = tk + MOE_EXPERTS * MOE_BLOCK
    n_blocks = n_rows // MOE_BLOCK
    block_e = jnp.searchsorted(pends, jnp.arange(n_blocks, dtype=jnp.int32) * MOE_BLOCK, side="right")
    block_e = jnp.minimum(block_e, MOE_EXPERTS - 1).astype(jnp.int32)
    n_used = (pends[-1] // MOE_BLOCK).astype(jnp.int32).reshape(1)
    return e_w, dest.reshape(t, MOE_TOP_K), block_e, n_used, n_rows


def _layer(x2d, p2d, cos_t, sin_t, bsz, seq, w_in, b_gate, ssd_conv_w, ssd_conv_b, ssd_dt_bias, ssd_a_log,
           ssd_d, ssd_norm_w, ssd_w_out, conf_dw_w, conf_dw_b, conf_ln_g, conf_ln_b, conf_w_out, attn_sinks,
           attn_w_out, w_out, ln1_g, ln1_b, moe_w_group, moe_b_group, moe_w_expert, moe_b_expert, moe_w_gate,
           moe_w_up, moe_w_down, ln2_g, ln2_b, ple_w_gate, ple_w_proj):
    t = bsz * seq
    w_main = jnp.concatenate([w_in[:, :DT_COL_ORIG], w_in[:, DT_COL_ORIG + SSD_HEADS:]], axis=1)
    w_main = jnp.concatenate([w_main[:, :COL_Q], w_main[:, COL_Q:COL_V][:, _pair_interleave(COL_V - COL_Q)],
                              w_main[:, COL_V:]], axis=1).astype(BF16)
    w_dt = jnp.pad(w_in[:, DT_COL_ORIG:DT_COL_ORIG + SSD_HEADS], ((0, 0), (0, LANES - SSD_HEADS))).astype(BF16)
    h, dt_raw = _in_projection(x2d, w_main, w_dt)

    y_ssd = _ssd_mixer(h, dt_raw, bsz, seq, ssd_conv_w, ssd_conv_b, ssd_dt_bias, ssd_a_log, ssd_d, ssd_norm_w,
                       ssd_w_out)
    y_conf = _conformer(h, bsz, seq, conf_dw_w, conf_dw_b, conf_ln_g, conf_ln_b, conf_w_out)
    y_att = _attention(h, cos_t, sin_t, bsz, seq, attn_sinks, attn_w_out)

    n_router = MOE_GROUPS + MOE_EXPERTS
    w_router = jnp.pad(jnp.concatenate([moe_w_group, moe_w_expert], axis=1), ((0, 0), (0, LANES - n_router)))
    b_router = jnp.pad(jnp.concatenate([moe_b_group, moe_b_expert]), (0, LANES - n_router))[None, :]
    x1, logits = _merge(h, b_gate, y_ssd, y_conf, y_att, x2d, w_out, ln1_g, ln1_b, w_router, b_router)

    e_w, dest, block_e, n_used, n_rows = _route(logits)
    tm = min(COMB_TM, t)
    dest_blocks = dest.reshape(t // tm, tm, MOE_TOP_K).transpose(0, 2, 1).reshape(t // tm, MOE_TOP_K * tm)
    x_rows = _moe_dispatch(x1, dest_blocks, n_rows)
    y_rows = _moe_experts(x_rows, block_e, n_used, moe_w_gate.astype(BF16), moe_w_up.astype(BF16),
                          moe_w_down.astype(BF16))
    return _combine(y_rows, dest_blocks, e_w, x1, p2d, ln2_g, ln2_b, ple_w_gate, ple_w_proj)


def kernel(x, p, positions, w_in, b_gate, ssd_conv_w, ssd_conv_b, ssd_dt_bias, ssd_a_log, ssd_d, ssd_norm_w, ssd_w_out, conf_dw_w, conf_dw_b, conf_ln_g, conf_ln_b, conf_w_out, attn_sinks, attn_w_out, w_out, ln1_g, ln1_b, moe_w_group, moe_b_group, moe_w_expert, moe_b_expert, moe_w_gate, moe_w_up, moe_w_down, ln2_g, ln2_b, ple_w_gate, ple_w_proj):
    bsz, seq, d = x.shape
    t = bsz * seq
    cos_t, sin_t = _rope_tables(positions)
    x2d = x.reshape(t, d)
    per_layer = (w_in, b_gate, ssd_conv_w, ssd_conv_b, ssd_dt_bias, ssd_a_log, ssd_d, ssd_norm_w, ssd_w_out,
                 conf_dw_w, conf_dw_b, conf_ln_g, conf_ln_b, conf_w_out, attn_sinks, attn_w_out, w_out, ln1_g,
                 ln1_b, moe_w_group, moe_b_group, moe_w_expert, moe_b_expert, moe_w_gate, moe_w_up, moe_w_down,
                 ln2_g, ln2_b, ple_w_gate, ple_w_proj)
    for layer in range(w_in.shape[0]):
        x2d = _layer(x2d, p[layer].reshape(t, -1), cos_t, sin_t, bsz, seq, *(w[layer] for w in per_layer))
    return x2d.reshape(bsz, seq, d)
```

```python
import functools

import jax
import jax.numpy as jnp
from jax import lax
from jax.experimental import pallas as pl
from jax.experimental.pallas import tpu as pltpu

F32 = jnp.float32
BF16 = jnp.bfloat16

D_MODEL = 1024
N_LAYERS = 2
PLE_DIM = 256
SSD_HEADS = 16
SSD_HEAD_DIM = 64
SSD_INNER = SSD_HEADS * SSD_HEAD_DIM
SSD_GROUPS = 4
SSD_STATE = 128
SSD_CONV = 4
SSD_CHUNK = 128
SSD_XBC = SSD_INNER + 2 * SSD_GROUPS * SSD_STATE
CONF_CH = 1024
CONF_KERNEL = 31
ATT_Q_HEADS = 16
ATT_KV_HEADS = 4
ATT_HEAD_DIM = 64
ATT_BLOCK = 128
ROPE_THETA = 10000.0
MOE_GROUPS = 4
MOE_EXPERTS_PER_GROUP = 8
MOE_EXPERTS = MOE_GROUPS * MOE_EXPERTS_PER_GROUP
MOE_TOP_K = 2
MOE_FF = 512
MOE_BLOCK = 256
DEEPNORM_ALPHA = (2 * N_LAYERS) ** 0.25
LN_EPS = 1e-5

LANES = 128
SUBLANES = 8
VMEM_LIMIT_BYTES = 56 * 1024 * 1024

COL_GATES = 0
COL_Z = 3 * D_MODEL
COL_XBC = COL_Z + SSD_INNER
COL_U = COL_XBC + SSD_XBC
COL_Q = COL_U + 2 * CONF_CH
COL_K = COL_Q + ATT_Q_HEADS * ATT_HEAD_DIM
COL_V = COL_K + ATT_KV_HEADS * ATT_HEAD_DIM
H_WIDTH = COL_V + ATT_KV_HEADS * ATT_HEAD_DIM
DT_COL_ORIG = 3 * D_MODEL + SSD_INNER + SSD_XBC


def _params(*semantics):
    return pltpu.CompilerParams(dimension_semantics=semantics, vmem_limit_bytes=VMEM_LIMIT_BYTES)


def _sigmoid(x):
    return 0.5 * (jnp.tanh(0.5 * x) + 1.0)


def _silu(x):
    return x * jax.nn.sigmoid(x)


def _layer_norm(x, g, b):
    mu = jnp.mean(x, axis=-1, keepdims=True)
    xc = x - mu
    var = jnp.mean(xc * xc, axis=-1, keepdims=True)
    return xc * lax.rsqrt(var + LN_EPS) * g + b


ROW_TILES = D_MODEL // LANES


def _store_row_tiles(ref, x):
    rows = x.shape[0]
    for k in range(ROW_TILES):
        ref[pl.ds(k, rows, stride=ROW_TILES), :] = x[:, k * LANES:(k + 1) * LANES]


def _load_row_tiles(ref, first_row, rows):
    return jnp.concatenate(
        [ref[pl.ds(first_row * ROW_TILES + k, rows, stride=ROW_TILES), :] for k in range(ROW_TILES)], axis=1)


IN_TM = 512
IN_TN = 512


def _inproj_kernel(x_ref, w_ref, wdt_ref, h_ref, dt_ref):
    xb = x_ref[...].astype(BF16)
    dt_ref[...] = jnp.dot(xb, wdt_ref[...], preferred_element_type=F32)
    for j in range(H_WIDTH // IN_TN):
        cols = slice(j * IN_TN, (j + 1) * IN_TN)
        h_ref[:, cols] = jnp.dot(xb, w_ref[:, cols], preferred_element_type=F32).astype(BF16)


def _in_projection(x2d, w_main, w_dt):
    t = x2d.shape[0]
    tm = min(IN_TM, t)
    resident = pl.Buffered(1)
    return pl.pallas_call(
        _inproj_kernel,
        out_shape=(jax.ShapeDtypeStruct((t, H_WIDTH), BF16), jax.ShapeDtypeStruct((t, LANES), F32)),
        grid=(t // tm,),
        in_specs=[
            pl.BlockSpec((tm, D_MODEL), lambda i: (i, 0)),
            pl.BlockSpec((D_MODEL, H_WIDTH), lambda i: (0, 0), pipeline_mode=resident),
            pl.BlockSpec((D_MODEL, LANES), lambda i: (0, 0), pipeline_mode=resident),
        ],
        out_specs=(
            pl.BlockSpec((tm, H_WIDTH), lambda i: (i, 0)),
            pl.BlockSpec((tm, LANES), lambda i: (i, 0)),
        ),
        compiler_params=_params("parallel"),
        name="in_projection",
    )(x2d, w_main, w_dt)


def _rope_kernel(pos_ref, inv_ref, sign_ref, cos_ref, sin_ref):
    ang = pos_ref[...].astype(F32) * inv_ref[...]
    cos_ref[...] = jnp.cos(ang)
    sin_ref[...] = jnp.sin(ang) * sign_ref[...]


def _rope_tables(positions):
    t = positions.size
    tm = min(1024, t)
    half = ATT_HEAD_DIM // 2
    inv_freq = ROPE_THETA ** (-jnp.arange(half, dtype=F32) / half)
    inv = jnp.tile(inv_freq, LANES // half)[None, :]
    sign = jnp.concatenate([-jnp.ones((LANES // 2,), F32), jnp.ones((LANES // 2,), F32)])[None, :]
    return pl.pallas_call(
        _rope_kernel,
        out_shape=(jax.ShapeDtypeStruct((t, LANES), F32), jax.ShapeDtypeStruct((t, LANES), F32)),
        grid=(t // tm,),
        in_specs=[
            pl.BlockSpec((tm, 1), lambda i: (i, 0)),
            pl.BlockSpec((1, LANES), lambda i: (0, 0)),
            pl.BlockSpec((1, LANES), lambda i: (0, 0)),
        ],
        out_specs=(pl.BlockSpec((tm, LANES), lambda i: (i, 0)), pl.BlockSpec((tm, LANES), lambda i: (i, 0))),
        compiler_params=_params("parallel"),
        name="rope_tables",
    )(positions.reshape(t, 1), inv, sign)


SSD_PAIRS = SSD_HEADS // 2
SSD_HALO = SUBLANES


def _ssd_kernel(xbc_ref, z_ref, dt_ref, cw_ref, cb_ref, dtb_ref, alog_ref, dsk_ref, nw_ref, wp_ref,
                y_ref, state_ref, ext_ref, ybuf_ref):
    L = SSD_CHUNK
    c = pl.program_id(1)

    @pl.when(c == 0)
    def _():
        state_ref[...] = jnp.zeros(state_ref.shape, F32)
        ext_ref[0:SSD_HALO, :] = jnp.zeros((SSD_HALO, SSD_XBC), F32)

    @pl.when(c > 0)
    def _():
        ext_ref[0:SSD_HALO, :] = ext_ref[L:L + SSD_HALO, :]

    ext_ref[SSD_HALO:SSD_HALO + L, :] = xbc_ref[...].astype(F32)

    acc = jnp.broadcast_to(cb_ref[...], (L, SSD_XBC))
    for k in range(SSD_CONV):
        off = SSD_HALO - (SSD_CONV - 1) + k
        acc = acc + cw_ref[k:k + 1, :] * ext_ref[off:off + L, :]
    act = _silu(acc)
    xs = act[:, :SSD_INNER]
    bm = act[:, SSD_INNER:SSD_INNER + SSD_GROUPS * SSD_STATE]
    cm = act[:, SSD_INNER + SSD_GROUPS * SSD_STATE:]

    x_dt = dt_ref[...] + dtb_ref[...]
    dt = jnp.maximum(x_dt, 0.0) + jnp.log1p(jnp.exp(-jnp.abs(x_dt)))
    a = -jnp.exp(alog_ref[...])
    row = lax.broadcasted_iota(jnp.int32, (L, L), 0)
    col = lax.broadcasted_iota(jnp.int32, (L, L), 1)
    causal = row >= col
    tril = causal.astype(F32)
    cs = jnp.dot(tril, dt * a, preferred_element_type=F32, precision=lax.Precision.HIGHEST)
    cs_t = cs.T
    lo = col < SSD_HEAD_DIM

    for g in range(SSD_GROUPS):
        bm_g = bm[:, g * SSD_STATE:(g + 1) * SSD_STATE].astype(BF16)
        cm_g = cm[:, g * SSD_STATE:(g + 1) * SSD_STATE].astype(BF16)
        cb_g = lax.dot_general(cm_g, bm_g, (((1,), (1,)), ((), ())), preferred_element_type=F32)
        pairs_per_group = SSD_PAIRS // SSD_GROUPS
        for jj in range(pairs_per_group):
            j = g * pairs_per_group + jj
            h0, h1 = 2 * j, 2 * j + 1
            sl = slice(j * LANES, (j + 1) * LANES)
            col0 = jnp.broadcast_to(cs[:, h0:h0 + 1], (L, L))
            col1 = jnp.broadcast_to(cs[:, h1:h1 + 1], (L, L))
            dec0 = jnp.where(causal, jnp.exp(col0 - cs_t[h0:h0 + 1, :]), 0.0)
            dec1 = jnp.where(causal, jnp.exp(col1 - cs_t[h1:h1 + 1, :]), 0.0)
            m = jnp.concatenate([cb_g * dec0, cb_g * dec1], axis=1).astype(BF16)
            dt_p = jnp.where(lo, jnp.broadcast_to(dt[:, h0:h0 + 1], (L, L)),
                             jnp.broadcast_to(dt[:, h1:h1 + 1], (L, L)))
            xs_p = xs[:, sl]
            xdt = xs_p * dt_p
            x2 = jnp.concatenate([jnp.where(lo, xdt, 0.0), jnp.where(lo, 0.0, xdt)], axis=0).astype(BF16)
            y_diag = jnp.dot(m, x2, preferred_element_type=F32)
            cs_p = jnp.where(lo, col0, col1)
            st = state_ref[j]
            y_off = jnp.exp(cs_p) * jnp.dot(cm_g, st.astype(BF16), preferred_element_type=F32)
            last = cs_p[L - 1:L, :]
            xdt_end = (xdt * jnp.exp(last - cs_p)).astype(BF16)
            new_st = lax.dot_general(bm_g, xdt_end, (((0,), (0,)), ((), ())), preferred_element_type=F32)
            state_ref[j] = st * jnp.exp(last) + new_st
            ybuf_ref[:, sl] = y_diag + y_off + xs_p * dsk_ref[:, sl]

    y = ybuf_ref[...] * _silu(z_ref[...].astype(F32))
    gw = SSD_INNER // SSD_GROUPS
    parts = []
    for g in range(SSD_GROUPS):
        yg = y[:, g * gw:(g + 1) * gw]
        parts.append(yg * lax.rsqrt(jnp.mean(yg * yg, axis=-1, keepdims=True) + LN_EPS))
    yn = jnp.concatenate(parts, axis=1) * nw_ref[...]
    y_ref[...] = jnp.dot(yn.astype(BF16), wp_ref[...], preferred_element_type=F32).astype(BF16)


def _ssd_mixer(h, dt_raw, bsz, seq, conv_w, conv_b, dt_bias, a_log, d_skip, norm_w, w_proj):
    t = bsz * seq
    L = SSD_CHUNK
    nc = seq // L
    pad = LANES - SSD_HEADS
    dtb = jnp.pad(dt_bias, (0, pad))[None, :]
    alog = jnp.pad(a_log, (0, pad))[None, :]
    dsk = jnp.repeat(d_skip, SSD_HEAD_DIM)[None, :]
    const = lambda b, c: (0, 0)
    return pl.pallas_call(
        _ssd_kernel,
        out_shape=jax.ShapeDtypeStruct((t, D_MODEL), BF16),
        grid=(bsz, nc),
        in_specs=[
            pl.BlockSpec((L, SSD_XBC), lambda b, c: (b * nc + c, COL_XBC // SSD_XBC)),
            pl.BlockSpec((L, SSD_INNER), lambda b, c: (b * nc + c, COL_Z // SSD_INNER)),
            pl.BlockSpec((L, LANES), lambda b, c: (b * nc + c, 0)),
            pl.BlockSpec((SSD_CONV, SSD_XBC), const),
            pl.BlockSpec((1, SSD_XBC), const),
            pl.BlockSpec((1, LANES), const),
            pl.BlockSpec((1, LANES), const),
            pl.BlockSpec((1, SSD_INNER), const),
            pl.BlockSpec((1, SSD_INNER), const),
            pl.BlockSpec((SSD_INNER, D_MODEL), const),
        ],
        out_specs=pl.BlockSpec((L, D_MODEL), lambda b, c: (b * nc + c, 0)),
        scratch_shapes=[
            pltpu.VMEM((SSD_PAIRS, SSD_STATE, LANES), F32),
            pltpu.VMEM((SSD_HALO + L, SSD_XBC), F32),
            pltpu.VMEM((L, SSD_INNER), F32),
        ],
        compiler_params=_params("parallel", "arbitrary"),
        name="ssd_mixer",
    )(h, h, dt_raw, conv_w, conv_b[None, :], dtb, alog, dsk, norm_w[None, :], w_proj.astype(BF16))


CONF_TS = 256
CONF_HALO = 32
CONF_ROW_BLK = 64
CONF_COL_BLK = 256


def _conf_kernel(u_ref, dw_ref, db_ref, g_ref, b_ref, wp_ref, y_ref, ext_ref, conv_ref):
    ts = u_ref.shape[0]
    s = pl.program_id(1)

    @pl.when(s == 0)
    def _():
        ext_ref[:, 0:CONF_HALO, :] = jnp.zeros((SUBLANES, CONF_HALO, CONF_CH), F32)
        ext_ref[:, ts + CONF_HALO - SUBLANES:ts + CONF_HALO, :] = jnp.zeros((SUBLANES, SUBLANES, CONF_CH), F32)

    @pl.when(s > 0)
    def _():
        ext_ref[:, 0:CONF_HALO, :] = ext_ref[:, ts:ts + CONF_HALO, :]

    u = u_ref[...].astype(F32)
    glu = u[:, :CONF_CH] * _sigmoid(u[:, CONF_CH:])
    for b in range(SUBLANES):
        ext_ref[b, CONF_HALO - b:CONF_HALO - b + ts, :] = glu

    for r0 in range(0, ts, CONF_ROW_BLK):
        for c0 in range(0, CONF_CH, CONF_COL_BLK):
            cols = slice(c0, c0 + CONF_COL_BLK)
            acc = jnp.broadcast_to(db_ref[:, cols], (CONF_ROW_BLK, CONF_COL_BLK))
            for k in range(CONF_KERNEL):
                shift = CONF_HALO - (CONF_KERNEL - 1) + k
                off = shift - shift % SUBLANES + r0
                acc = acc + dw_ref[k:k + 1, cols] * ext_ref[shift % SUBLANES, off:off + CONF_ROW_BLK, cols]
            conv_ref[r0:r0 + CONF_ROW_BLK, cols] = acc

    hn = _silu(_layer_norm(conv_ref[...], g_ref[...], b_ref[...]))
    y_ref[...] = jnp.dot(hn.astype(BF16), wp_ref[...], preferred_element_type=F32).astype(BF16)


def _conformer(h, bsz, seq, dw_w, dw_b, ln_g, ln_b, w_proj):
    t = bsz * seq
    ts = min(CONF_TS, seq)
    nt = seq // ts
    const = lambda b, s: (0, 0)
    dw = jnp.pad(dw_w, ((0, CONF_HALO - CONF_KERNEL), (0, 0)))
    return pl.pallas_call(
        _conf_kernel,
        out_shape=jax.ShapeDtypeStruct((t, D_MODEL), BF16),
        grid=(bsz, nt),
        in_specs=[
            pl.BlockSpec((ts, 2 * CONF_CH), lambda b, s: (b * nt + s, COL_U // (2 * CONF_CH))),
            pl.BlockSpec((CONF_HALO, CONF_CH), const),
            pl.BlockSpec((1, CONF_CH), const),
            pl.BlockSpec((1, CONF_CH), const),
            pl.BlockSpec((1, CONF_CH), const),
            pl.BlockSpec((CONF_CH, D_MODEL), const),
        ],
        out_specs=pl.BlockSpec((ts, D_MODEL), lambda b, s: (b * nt + s, 0)),
        scratch_shapes=[
            pltpu.VMEM((SUBLANES, CONF_HALO + ts, CONF_CH), F32),
            pltpu.VMEM((ts, CONF_CH), F32),
        ],
        compiler_params=_params("parallel", "arbitrary"),
        name="conformer_conv",
    )(h, dw, dw_b[None, :], ln_g[None, :], ln_b[None, :], w_proj.astype(BF16))


def _pair_interleave(width):
    half = ATT_HEAD_DIM // 2
    j = jnp.arange(width, dtype=jnp.int32)
    block, lane = j // LANES, j % LANES
    chunk, within = lane // half, lane % half
    return block * LANES + (chunk % 2) * ATT_HEAD_DIM + (chunk // 2) * half + within


def _attn_kernel(q_ref, kc_ref, kp_ref, vc_ref, vp_ref, cosc_ref, sinc_ref, cosp_ref, sinp_ref, sink_ref,
                 wp_ref, y_ref, obuf_ref):
    blk = ATT_BLOCK
    i = pl.program_id(1)
    rep = ATT_Q_HEADS // ATT_KV_HEADS
    quarter = ATT_HEAD_DIM // 2
    scale = ATT_HEAD_DIM ** -0.5

    def lane_masks(rows):
        lane = lax.broadcasted_iota(jnp.int32, (rows, LANES), 1)
        return (lane % ATT_HEAD_DIM) < quarter, lane < ATT_HEAD_DIM

    def rope(x, cos, sin):
        return x * cos + pltpu.roll(x, ATT_HEAD_DIM, 1) * sin

    first_q, lo_q = lane_masks(blk)
    first_k, lo_k = lane_masks(2 * blk)

    cos_q, sin_q = cosc_ref[...], sinc_ref[...]
    cos_k = jnp.concatenate([cosp_ref[...], cosc_ref[...]], axis=0)
    sin_k = jnp.concatenate([sinp_ref[...], sinc_ref[...]], axis=0)
    k = jnp.concatenate([kp_ref[...], kc_ref[...]], axis=0).astype(F32)
    v = jnp.concatenate([vp_ref[...], vc_ref[...]], axis=0).astype(F32)

    qi = lax.broadcasted_iota(jnp.int32, (blk, 2 * blk), 0)
    kj = lax.broadcasted_iota(jnp.int32, (blk, 2 * blk), 1)
    low = jnp.where(i == 0, blk - 1, qi)
    mask = (kj <= qi + blk) & (kj > low)

    for kh in range(ATT_KV_HEADS):
        kb = rope(k[:, (kh // 2) * LANES:(kh // 2 + 1) * LANES], cos_k, sin_k)
        vb = v[:, (kh // 2) * LANES:(kh // 2 + 1) * LANES]
        if kh % 2 == 0:
            k_dup = jnp.where(first_k, kb, pltpu.roll(kb, quarter, 1))
            v_dup = jnp.where(lo_k, vb, pltpu.roll(vb, ATT_HEAD_DIM, 1))
        else:
            k_dup = jnp.where(first_k, pltpu.roll(kb, LANES - quarter, 1), kb)
            v_dup = jnp.where(lo_k, pltpu.roll(vb, ATT_HEAD_DIM, 1), vb)
        k_dup = k_dup.astype(BF16)
        v_ext = jnp.concatenate([v_dup, jnp.ones_like(v_dup)], axis=1).astype(BF16)

        stacked = []
        for qb in range(rep // 2):
            cols = slice((kh * (rep // 2) + qb) * LANES, (kh * (rep // 2) + qb + 1) * LANES)
            qr = rope(q_ref[:, cols].astype(F32), cos_q, sin_q) * scale
            stacked.append(jnp.where(first_q, qr, 0.0).astype(BF16))
            stacked.append(jnp.where(first_q, 0.0, qr).astype(BF16))
        qs = jnp.concatenate(stacked, axis=0)
        s_all = lax.dot_general(qs, k_dup, (((1,), (1,)), ((), ())), preferred_element_type=F32)

        es, sink_terms = [], []
        for r in range(rep):
            hq = kh * rep + r
            s = jnp.where(mask, s_all[r * blk:(r + 1) * blk], -jnp.inf)
            sink = sink_ref[hq:hq + 1, :]
            mx = jnp.maximum(jnp.broadcast_to(jnp.max(s, axis=-1, keepdims=True), (blk, LANES)), sink)
            sink_terms.append(jnp.exp(sink - mx))
            es.append(jnp.exp(s - jnp.concatenate([mx, mx], axis=1)).astype(BF16))
        o = jnp.dot(jnp.concatenate(es, axis=0), v_ext, preferred_element_type=F32)
        outs = []
        for r in range(rep):
            rows = slice(r * blk, (r + 1) * blk)
            outs.append(o[rows, :LANES] * (1.0 / (o[rows, LANES:] + sink_terms[r])))
        for qb in range(rep // 2):
            cols = slice((kh * (rep // 2) + qb) * LANES, (kh * (rep // 2) + qb + 1) * LANES)
            obuf_ref[:, cols] = jnp.where(lo_q, outs[2 * qb], outs[2 * qb + 1])

    y_ref[...] = jnp.dot(obuf_ref[...].astype(BF16), wp_ref[...], preferred_element_type=F32).astype(BF16)


def _attention(h, cos_t, sin_t, bsz, seq, sinks, w_proj):
    t = bsz * seq
    blk = ATT_BLOCK
    nb = seq // blk
    qw = ATT_Q_HEADS * ATT_HEAD_DIM
    kw = ATT_KV_HEADS * ATT_HEAD_DIM
    cur = lambda b, i: b * nb + i
    prev = lambda b, i: b * nb + jnp.maximum(i - 1, 0)
    const = lambda b, i: (0, 0)
    sink = jnp.broadcast_to(sinks[:, None], (ATT_Q_HEADS, LANES))
    return pl.pallas_call(
        _attn_kernel,
        out_shape=jax.ShapeDtypeStruct((t, D_MODEL), BF16),
        grid=(bsz, nb),
        in_specs=[
            pl.BlockSpec((blk, qw), lambda b, i: (cur(b, i), COL_Q // qw)),
            pl.BlockSpec((blk, kw), lambda b, i: (cur(b, i), COL_K // kw)),
            pl.BlockSpec((blk, kw), lambda b, i: (prev(b, i), COL_K // kw)),
            pl.BlockSpec((blk, kw), lambda b, i: (cur(b, i), COL_V // kw)),
            pl.BlockSpec((blk, kw), lambda b, i: (prev(b, i), COL_V // kw)),
            pl.BlockSpec((blk, LANES), lambda b, i: (cur(b, i), 0)),
            pl.BlockSpec((blk, LANES), lambda b, i: (cur(b, i), 0)),
            pl.BlockSpec((blk, LANES), lambda b, i: (prev(b, i), 0)),
            pl.BlockSpec((blk, LANES), lambda b, i: (prev(b, i), 0)),
            pl.BlockSpec((ATT_Q_HEADS, LANES), const),
            pl.BlockSpec((qw, D_MODEL), const),
        ],
        out_specs=pl.BlockSpec((blk, D_MODEL), lambda b, i: (cur(b, i), 0)),
        scratch_shapes=[pltpu.VMEM((blk, qw), F32)],
        compiler_params=_params("parallel", "arbitrary"),
        name="swa_attention",
    )(h, h, h, h, h, cos_t, sin_t, cos_t, sin_t, sink, w_proj.astype(BF16))


MERGE_TM = 512


def _merge_kernel(g0_ref, g1_ref, g2_ref, bg_ref, y0_ref, y1_ref, y2_ref, x_ref, wo_ref, lg_ref, lb_ref,
                  wrh_ref, wrl_ref, br_ref, x1_ref, logit_ref):
    mixed = None
    for n, (g_ref, y_ref) in enumerate(((g0_ref, y0_ref), (g1_ref, y1_ref), (g2_ref, y2_ref))):
        term = _sigmoid(g_ref[...].astype(F32) + bg_ref[n:n + 1, :]) * y_ref[...].astype(F32)
        mixed = term if mixed is None else mixed + term
    r = DEEPNORM_ALPHA * x_ref[...] + jnp.dot(mixed.astype(BF16), wo_ref[...], preferred_element_type=F32)
    x1 = _layer_norm(r, lg_ref[...], lb_ref[...])
    _store_row_tiles(x1_ref, x1)
    x_hi = x1.astype(BF16)
    x_lo = (x1 - x_hi.astype(F32)).astype(BF16)
    logit_ref[...] = (jnp.dot(x_hi, wrh_ref[...], preferred_element_type=F32)
                      + (jnp.dot(x_hi, wrl_ref[...], preferred_element_type=F32)
                         + jnp.dot(x_lo, wrh_ref[...], preferred_element_type=F32))) + br_ref[...]


def _merge(h, b_gate, y_ssd, y_conf, y_att, x2d, w_out, ln_g, ln_b, w_router, b_router):
    t = x2d.shape[0]
    tm = min(MERGE_TM, t)
    const = lambda i: (0, 0)
    row = lambda i: (i, 0)
    w_router_hi = w_router.astype(BF16)
    w_router_lo = (w_router - w_router_hi.astype(F32)).astype(BF16)
    return pl.pallas_call(
        _merge_kernel,
        out_shape=(jax.ShapeDtypeStruct((t * ROW_TILES, LANES), F32), jax.ShapeDtypeStruct((t, LANES), F32)),
        grid=(t // tm,),
        in_specs=[
            pl.BlockSpec((tm, D_MODEL), lambda i: (i, 0)),
            pl.BlockSpec((tm, D_MODEL), lambda i: (i, 1)),
            pl.BlockSpec((tm, D_MODEL), lambda i: (i, 2)),
            pl.BlockSpec((3, D_MODEL), const),
            pl.BlockSpec((tm, D_MODEL), row),
            pl.BlockSpec((tm, D_MODEL), row),
            pl.BlockSpec((tm, D_MODEL), row),
            pl.BlockSpec((tm, D_MODEL), row),
            pl.BlockSpec((D_MODEL, D_MODEL), const),
            pl.BlockSpec((1, D_MODEL), const),
            pl.BlockSpec((1, D_MODEL), const),
            pl.BlockSpec((D_MODEL, LANES), const),
            pl.BlockSpec((D_MODEL, LANES), const),
            pl.BlockSpec((1, LANES), const),
        ],
        out_specs=(pl.BlockSpec((tm * ROW_TILES, LANES), row), pl.BlockSpec((tm, LANES), row)),
        compiler_params=_params("parallel"),
        name="merge_ln_router",
    )(h, h, h, b_gate, y_ssd, y_conf, y_att, x2d, w_out.astype(BF16), ln_g[None, :], ln_b[None, :],
      w_router_hi, w_router_lo, b_router)


GATHER_UNROLL = 8


def _on_slot(slot, fn, n_slots=2):
    for s in range(n_slots):
        @pl.when(slot == s)
        def _(s=s):
            fn(s)


def _gather_pipeline(i, n_steps, idx_hbm, src_hbm, idx_smem, buf, isem, gsem):
    n_rows = buf[0].shape[0] // ROW_TILES
    slot = i % 2

    def idx_copy(step, s):
        return pltpu.make_async_copy(idx_hbm.at[step], idx_smem[s], isem.at[s])

    def issue_rows(s):
        def body(it, carry):
            for u in range(GATHER_UNROLL):
                r = it * GATHER_UNROLL + u
                src_row = pl.multiple_of(idx_smem[s][r] * ROW_TILES, ROW_TILES)
                dst_row = pl.multiple_of(r * ROW_TILES, ROW_TILES)
                pltpu.make_async_copy(src_hbm.at[pl.ds(src_row, ROW_TILES)],
                                      buf[s].at[pl.ds(dst_row, ROW_TILES)], gsem.at[s]).start()
            return carry

        lax.fori_loop(0, n_rows // GATHER_UNROLL, body, 0)

    @pl.when(i == 0)
    def _():
        idx_copy(0, 0).start()
        idx_copy(0, 0).wait()
        issue_rows(0)

        @pl.when(n_steps > 1)
        def _():
            idx_copy(1, 1).start()

    @pl.when(i + 1 < n_steps)
    def _():
        def prefetch(s):
            idx_copy(i + 1, 1 - s).wait()
            issue_rows(1 - s)

            @pl.when(i + 2 < n_steps)
            def _():
                idx_copy(i + 2, s).start()

        _on_slot(slot, prefetch)


def _gather_wait(s, src_hbm, buf, gsem):
    pltpu.make_async_copy(src_hbm.at[pl.ds(0, buf[s].shape[0])], buf[s], gsem.at[s]).wait()


DISPATCH_SLOTS = 3


def _dispatch_kernel(idx_hbm, x1_hbm, init_hbm, rows_hbm, idx0, idx1, idx2, st0, st1, st2, isem, xsem, dsem):
    del init_hbm
    i = pl.program_id(0)
    n_steps = pl.num_programs(0)
    idx = (idx0, idx1, idx2)
    stage = (st0, st1, st2)
    n_idx = idx0.shape[0]
    tm = n_idx // MOE_TOP_K

    def loads(step, s):
        first = pl.multiple_of(step * (tm * ROW_TILES), tm * ROW_TILES)
        return (pltpu.make_async_copy(idx_hbm.at[step], idx[s], isem.at[s]),
                pltpu.make_async_copy(x1_hbm.at[pl.ds(first, tm * ROW_TILES)], stage[s], xsem.at[s]))

    def start_loads(step, s):
        for c in loads(step, s):
            c.start()

    def wait_rows(s):
        pltpu.make_async_copy(x1_hbm.at[pl.ds(0, n_idx * ROW_TILES)], rows_hbm.at[pl.ds(0, n_idx * ROW_TILES)],
                              dsem.at[s]).wait()

    @pl.when(i == 0)
    def _():
        start_loads(0, 0)

        @pl.when(n_steps > 1)
        def _():
            start_loads(1, 1)

    def step(s):
        freed = (s + DISPATCH_SLOTS - 1) % DISPATCH_SLOTS
        for c in loads(i, s):
            c.wait()

        def body(it, carry):
            for u in range(GATHER_UNROLL):
                r = it * GATHER_UNROLL + u
                src_row = pl.multiple_of((r % tm) * ROW_TILES, ROW_TILES)
                dst_row = pl.multiple_of(idx[s][r] * ROW_TILES, ROW_TILES)
                pltpu.make_async_copy(stage[s].at[pl.ds(src_row, ROW_TILES)],
                                      rows_hbm.at[pl.ds(dst_row, ROW_TILES)], dsem.at[s]).start()
            return carry

        lax.fori_loop(0, n_idx // GATHER_UNROLL, body, 0)

        @pl.when(i > 0)
        def _():
            wait_rows(freed)

        @pl.when(i + 2 < n_steps)
        def _():
            start_loads(i + 2, freed)

        @pl.when(i == n_steps - 1)
        def _():
            wait_rows(s)

    _on_slot(i % DISPATCH_SLOTS, step, DISPATCH_SLOTS)


def _moe_dispatch(x1, dest_blocks, n_rows):
    n_steps, n_idx = dest_blocks.shape
    any_spec = pl.BlockSpec(memory_space=pl.ANY)
    return pl.pallas_call(
        _dispatch_kernel,
        out_shape=jax.ShapeDtypeStruct((n_rows * ROW_TILES, LANES), F32),
        grid=(n_steps,),
        in_specs=[any_spec, any_spec, any_spec],
        out_specs=any_spec,
        scratch_shapes=(
            [pltpu.SMEM((n_idx,), jnp.int32)] * DISPATCH_SLOTS
            + [pltpu.VMEM((n_idx // MOE_TOP_K * ROW_TILES, LANES), F32)] * DISPATCH_SLOTS
            + [pltpu.SemaphoreType.DMA((DISPATCH_SLOTS,))] * 3
        ),
        input_output_aliases={2: 0},
        compiler_params=_params("arbitrary"),
        name="moe_dispatch",
    )(dest_blocks, x1, jnp.zeros((n_rows * ROW_TILES, LANES), F32))


def _moe_kernel(be_ref, nu_ref, x_ref, wg_ref, wu_ref, wd_ref, y_ref):
    del be_ref
    i = pl.program_id(0)

    @pl.when(i < nu_ref[0])
    def _():
        xb = _load_row_tiles(x_ref, 0, MOE_BLOCK).astype(BF16)
        gate = jnp.dot(xb, wg_ref[...], preferred_element_type=F32)
        up = jnp.dot(xb, wu_ref[...], preferred_element_type=F32)
        hmid = (_silu(gate) * up).astype(BF16)
        _store_row_tiles(y_ref, jnp.dot(hmid, wd_ref[...], preferred_element_type=F32))

    @pl.when(i >= nu_ref[0])
    def _():
        y_ref[...] = jnp.zeros(y_ref.shape, F32)


def _moe_experts(x_rows, block_e, n_used, w_gate, w_up, w_down):
    n_blocks = block_e.shape[0]
    grid_spec = pltpu.PrefetchScalarGridSpec(
        num_scalar_prefetch=2,
        grid=(n_blocks,),
        in_specs=[
            pl.BlockSpec((MOE_BLOCK * ROW_TILES, LANES), lambda i, be, nu: (jnp.minimum(i, nu[0] - 1), 0)),
            pl.BlockSpec((None, D_MODEL, MOE_FF), lambda i, be, nu: (be[i], 0, 0)),
            pl.BlockSpec((None, D_MODEL, MOE_FF), lambda i, be, nu: (be[i], 0, 0)),
            pl.BlockSpec((None, MOE_FF, D_MODEL), lambda i, be, nu: (be[i], 0, 0)),
        ],
        out_specs=pl.BlockSpec((MOE_BLOCK * ROW_TILES, LANES), lambda i, be, nu: (i, 0)),
    )
    return pl.pallas_call(
        _moe_kernel,
        out_shape=jax.ShapeDtypeStruct((n_blocks * MOE_BLOCK * ROW_TILES, LANES), F32),
        grid_spec=grid_spec,
        compiler_params=_params("arbitrary"),
        name="moe_experts",
    )(block_e, n_used, x_rows, w_gate, w_up, w_down)


COMB_TM = 256


def _combine_kernel(idx_hbm, y_hbm, ew_ref, x1_ref, p_ref, lg_ref, lb_ref, wpg_ref, wpp_ref, out_ref,
                    idx0, idx1, ybuf0, ybuf1, ffn_ref, isem, gsem):
    i = pl.program_id(0)
    tm = out_ref.shape[0]
    ybuf = (ybuf0, ybuf1)
    _gather_pipeline(i, pl.num_programs(0), idx_hbm, y_hbm, (idx0, idx1), ybuf, isem, gsem)

    def weighted_sum(s):
        _gather_wait(s, y_hbm, ybuf, gsem)
        ew = ew_ref[...]
        ffn_ref[...] = (_load_row_tiles(ybuf[s], 0, tm) * ew[:, 0:1]
                        + _load_row_tiles(ybuf[s], tm, tm) * ew[:, 1:2])

    _on_slot(i % 2, weighted_sum)
    x1 = _load_row_tiles(x1_ref, 0, tm)
    x2 = _layer_norm(DEEPNORM_ALPHA * x1 + ffn_ref[...], lg_ref[...], lb_ref[...])
    gate = _sigmoid(jnp.dot(x2.astype(BF16), wpg_ref[...], preferred_element_type=F32))
    emb = jnp.dot(p_ref[...].astype(BF16), wpp_ref[...], preferred_element_type=F32)
    out_ref[...] = x2 + gate * emb


def _combine(y_rows, dest_blocks, e_w, x1, p2d, ln_g, ln_b, ple_w_gate, ple_w_proj):
    t = x1.shape[0] // ROW_TILES
    tm = dest_blocks.shape[1] // MOE_TOP_K
    const = lambda i: (0, 0)
    row = lambda i: (i, 0)
    return pl.pallas_call(
        _combine_kernel,
        out_shape=jax.ShapeDtypeStruct((t, D_MODEL), F32),
        grid=(t // tm,),
        in_specs=[
            pl.BlockSpec(memory_space=pl.ANY),
            pl.BlockSpec(memory_space=pl.ANY),
            pl.BlockSpec((tm, MOE_TOP_K), row),
            pl.BlockSpec((tm * ROW_TILES, LANES), row),
            pl.BlockSpec((tm, PLE_DIM), row),
            pl.BlockSpec((1, D_MODEL), const),
            pl.BlockSpec((1, D_MODEL), const),
            pl.BlockSpec((D_MODEL, D_MODEL), const),
            pl.BlockSpec((PLE_DIM, D_MODEL), const),
        ],
        out_specs=pl.BlockSpec((tm, D_MODEL), row),
        scratch_shapes=[
            pltpu.SMEM((MOE_TOP_K * tm,), jnp.int32),
            pltpu.SMEM((MOE_TOP_K * tm,), jnp.int32),
            pltpu.VMEM((MOE_TOP_K * tm * ROW_TILES, LANES), F32),
            pltpu.VMEM((MOE_TOP_K * tm * ROW_TILES, LANES), F32),
            pltpu.VMEM((tm, D_MODEL), F32),
            pltpu.SemaphoreType.DMA((2,)),
            pltpu.SemaphoreType.DMA((2,)),
        ],
        compiler_params=_params("arbitrary"),
        name="moe_combine",
    )(dest_blocks, y_rows, e_w, x1, p2d, ln_g[None, :], ln_b[None, :], ple_w_gate.astype(BF16),
      ple_w_proj.astype(BF16))


ROUTE_CHUNK = 256


def _route(logits):
    t = logits.shape[0]
    tk = t * MOE_TOP_K
    g_logits = logits[:, :MOE_GROUPS]
    e_logits = logits[:, MOE_GROUPS:MOE_GROUPS + MOE_EXPERTS].reshape(t, MOE_GROUPS, MOE_EXPERTS_PER_GROUP)
    g_prob = jax.nn.softmax(g_logits, axis=-1)
    g_idx = jnp.argmax(g_logits, axis=-1).astype(jnp.int32)
    g_w = jnp.take_along_axis(g_prob, g_idx[:, None], axis=1)
    e_in = jnp.take_along_axis(e_logits, g_idx[:, None, None], axis=1)[:, 0]
    lane = jnp.arange(MOE_EXPERTS_PER_GROUP, dtype=jnp.int32)[None, :]
    i1 = jnp.argmax(e_in, axis=-1).astype(jnp.int32)
    rest = jnp.where(lane == i1[:, None], -jnp.inf, e_in)
    i2 = jnp.argmax(rest, axis=-1).astype(jnp.int32)
    top_i = jnp.stack([i1, i2], axis=-1)
    top_v = jnp.stack([jnp.max(e_in, axis=-1), jnp.max(rest, axis=-1)], axis=-1)
    e_w = jax.nn.softmax(top_v, axis=-1) * g_w
    e_id = g_idx[:, None] * MOE_EXPERTS_PER_GROUP + top_i
    flat_e = e_id.reshape(tk)

    chunk = min(ROUTE_CHUNK, tk)
    onehot = flat_e[:, None] == jnp.arange(MOE_EXPERTS, dtype=jnp.int32)[None, :]
    oh = onehot.reshape(tk // chunk, chunk, MOE_EXPERTS)
    tril = jnp.tril(jnp.ones((chunk, chunk), BF16))
    within = jnp.einsum("ij,cje->cie", tril, oh.astype(BF16), preferred_element_type=F32).astype(jnp.int32)
    chunk_counts = within[:, -1, :]
    chunk_ends = jnp.cumsum(chunk_counts, axis=0)
    csum = within + (chunk_ends - chunk_counts)[:, None, :]
    rank = jnp.sum(jnp.where(oh, csum, 0), axis=-1).reshape(tk) - 1
    counts = chunk_ends[-1]
    padded = ((counts + MOE_BLOCK - 1) // MOE_BLOCK) * MOE_BLOCK
    pends = jnp.cumsum(padded)
    pstarts = pends - padded
    dest = pstarts[flat_e] + rank

    n_rows = tk + MOE_EXPERTS * MOE_BLOCK
    n_blocks = n_rows // MOE_BLOCK
    block_start = jnp.arange(n_blocks, dtype=jnp.int32) * MOE_BLOCK
    block_e = jnp.sum((pends[None, :] <= block_start[:, None]).astype(jnp.int32), axis=1)
    block_e = jnp.minimum(block_e, MOE_EXPERTS - 1).astype(jnp.int32)
    n_used = (pends[-1] // MOE_BLOCK).astype(jnp.int32).reshape(1)
    return e_w, dest.reshape(t, MOE_TOP_K), block_e, n_used, n_rows


def _layer(x2d, p2d, cos_t, sin_t, bsz, seq, w_in, b_gate, ssd_conv_w, ssd_conv_b, ssd_dt_bias, ssd_a_log,
           ssd_d, ssd_norm_w, ssd_w_out, conf_dw_w, conf_dw_b, conf_ln_g, conf_ln_b, conf_w_out, attn_sinks,
           attn_w_out, w_out, ln1_g, ln1_b, moe_w_group, moe_b_group, moe_w_expert, moe_b_expert, moe_w_gate,
           moe_w_up, moe_w_down, ln2_g, ln2_b, ple_w_gate, ple_w_proj):
    t = bsz * seq
    w_main = jnp.concatenate([w_in[:, :DT_COL_ORIG], w_in[:, DT_COL_ORIG + SSD_HEADS:]], axis=1)
    w_main = jnp.concatenate([w_main[:, :COL_Q], w_main[:, COL_Q:COL_V][:, _pair_interleave(COL_V - COL_Q)],
                              w_main[:, COL_V:]], axis=1).astype(BF16)
    w_dt = jnp.pad(w_in[:, DT_COL_ORIG:DT_COL_ORIG + SSD_HEADS], ((0, 0), (0, LANES - SSD_HEADS))).astype(BF16)
    h, dt_raw = _in_projection(x2d, w_main, w_dt)

    y_ssd = _ssd_mixer(h, dt_raw, bsz, seq, ssd_conv_w, ssd_conv_b, ssd_dt_bias, ssd_a_log, ssd_d, ssd_norm_w,
                       ssd_w_out)
    y_conf = _conformer(h, bsz, seq, conf_dw_w, conf_dw_b, conf_ln_g, conf_ln_b, conf_w_out)
    y_att = _attention(h, cos_t, sin_t, bsz, seq, attn_sinks, attn_w_out)

    n_router = MOE_GROUPS + MOE_EXPERTS
    w_router = jnp.pad(jnp.concatenate([moe_w_group, moe_w_expert], axis=1), ((0, 0), (0, LANES - n_router)))
    b_router = jnp.pad(jnp.concatenate([moe_b_group, moe_b_expert]), (0, LANES - n_router))[None, :]
    x1, logits = _merge(h, b_gate, y_ssd, y_conf, y_att, x2d, w_out, ln1_g, ln1_b, w_router, b_router)

    e_w, dest, block_e, n_used, n_rows = _route(logits)
    tm = min(COMB_TM, t)
    dest_blocks = dest.reshape(t // tm, tm, MOE_TOP_K).transpose(0, 2, 1).reshape(t // tm, MOE_TOP_K * tm)
    x_rows = _moe_dispatch(x1, dest_blocks, n_rows)
    y_rows = _moe_experts(x_rows, block_e, n_used, moe_w_gate.astype(BF16), moe_w_up.astype(BF16),
                          moe_w_down.astype(BF16))
    return _combine(y_rows, dest_blocks, e_w, x1, p2d, ln2_g, ln2_b, ple_w_gate, ple_w_proj)


def kernel(x, p, positions, w_in, b_gate, ssd_conv_w, ssd_conv_b, ssd_dt_bias, ssd_a_log, ssd_d, ssd_norm_w, ssd_w_out, conf_dw_w, conf_dw_b, conf_ln_g, conf_ln_b, conf_w_out, attn_sinks, attn_w_out, w_out, ln1_g, ln1_b, moe_w_group, moe_b_group, moe_w_expert, moe_b_expert, moe_w_gate, moe_w_up, moe_w_down, ln2_g, ln2_b, ple_w_gate, ple_w_proj):
    bsz, seq, d = x.shape
    t = bsz * seq
    cos_t, sin_t = _rope_tables(positions)
    x2d = x.reshape(t, d)
    per_layer = (w_in, b_gate, ssd_conv_w, ssd_conv_b, ssd_dt_bias, ssd_a_log, ssd_d, ssd_norm_w, ssd_w_out,
                 conf_dw_w, conf_dw_b, conf_ln_g, conf_ln_b, conf_w_out, attn_sinks, attn_w_out, w_out, ln1_g,
                 ln1_b, moe_w_group, moe_b_group, moe_w_expert, moe_b_expert, moe_w_gate, moe_w_up, moe_w_down,
                 ln2_g, ln2_b, ple_w_gate, ple_w_proj)
    for layer in range(w_in.shape[0]):
        x2d = _layer(x2d, p[layer].reshape(t, -1), cos_t, sin_t, bsz, seq, *(w[layer] for w in per_layer))
    return x2d.reshape(bsz, seq, d)
```

```python
import functools

import jax
import jax.numpy as jnp
from jax import lax
from jax.experimental import pallas as pl
from jax.experimental.pallas import tpu as pltpu

F32 = jnp.float32
BF16 = jnp.bfloat16

D_MODEL = 1024
N_LAYERS = 2
PLE_DIM = 256
SSD_HEADS = 16
SSD_HEAD_DIM = 64
SSD_INNER = SSD_HEADS * SSD_HEAD_DIM
SSD_GROUPS = 4
SSD_STATE = 128
SSD_CONV = 4
SSD_CHUNK = 128
SSD_XBC = SSD_INNER + 2 * SSD_GROUPS * SSD_STATE
CONF_CH = 1024
CONF_KERNEL = 31
ATT_Q_HEADS = 16
ATT_KV_HEADS = 4
ATT_HEAD_DIM = 64
ATT_BLOCK = 128
ROPE_THETA = 10000.0
MOE_GROUPS = 4
MOE_EXPERTS_PER_GROUP = 8
MOE_EXPERTS = MOE_GROUPS * MOE_EXPERTS_PER_GROUP
MOE_TOP_K = 2
MOE_FF = 512
MOE_BLOCK = 256
DEEPNORM_ALPHA = (2 * N_LAYERS) ** 0.25
LN_EPS = 1e-5

LANES = 128
SUBLANES = 8
VMEM_LIMIT_BYTES = 56 * 1024 * 1024

COL_GATES = 0
COL_Z = 3 * D_MODEL
COL_XBC = COL_Z + SSD_INNER
COL_U = COL_XBC + SSD_XBC
COL_Q = COL_U + 2 * CONF_CH
COL_K = COL_Q + ATT_Q_HEADS * ATT_HEAD_DIM
COL_V = COL_K + ATT_KV_HEADS * ATT_HEAD_DIM
H_WIDTH = COL_V + ATT_KV_HEADS * ATT_HEAD_DIM
DT_COL_ORIG = 3 * D_MODEL + SSD_INNER + SSD_XBC


def _params(*semantics):
    return pltpu.CompilerParams(dimension_semantics=semantics, vmem_limit_bytes=VMEM_LIMIT_BYTES)


def _sigmoid(x):
    return 0.5 * (jnp.tanh(0.5 * x) + 1.0)


def _silu(x):
    return x * jax.nn.sigmoid(x)


def _layer_norm(x, g, b):
    mu = jnp.mean(x, axis=-1, keepdims=True)
    xc = x - mu
    var = jnp.mean(xc * xc, axis=-1, keepdims=True)
    return xc * lax.rsqrt(var + LN_EPS) * g + b


ROW_TILES = D_MODEL // LANES


def _store_row_tiles(ref, x):
    rows = x.shape[0]
    for k in range(ROW_TILES):
        ref[pl.ds(k, rows, stride=ROW_TILES), :] = x[:, k * LANES:(k + 1) * LANES]


def _load_row_tiles(ref, first_row, rows):
    return jnp.concatenate(
        [ref[pl.ds(first_row * ROW_TILES + k, rows, stride=ROW_TILES), :] for k in range(ROW_TILES)], axis=1)


IN_TM = 512
IN_TN = 512


def _inproj_kernel(x_ref, w_ref, wdt_ref, h_ref, dt_ref):
    xb = x_ref[...].astype(BF16)
    dt_ref[...] = jnp.dot(xb, wdt_ref[...], preferred_element_type=F32)
    for j in range(H_WIDTH // IN_TN):
        cols = slice(j * IN_TN, (j + 1) * IN_TN)
        h_ref[:, cols] = jnp.dot(xb, w_ref[:, cols], preferred_element_type=F32).astype(BF16)


def _in_projection(x2d, w_main, w_dt):
    t = x2d.shape[0]
    tm = min(IN_TM, t)
    resident = pl.Buffered(1)
    return pl.pallas_call(
        _inproj_kernel,
        out_shape=(jax.ShapeDtypeStruct((t, H_WIDTH), BF16), jax.ShapeDtypeStruct((t, LANES), F32)),
        grid=(t // tm,),
        in_specs=[
            pl.BlockSpec((tm, D_MODEL), lambda i: (i, 0)),
            pl.BlockSpec((D_MODEL, H_WIDTH), lambda i: (0, 0), pipeline_mode=resident),
            pl.BlockSpec((D_MODEL, LANES), lambda i: (0, 0), pipeline_mode=resident),
        ],
        out_specs=(
            pl.BlockSpec((tm, H_WIDTH), lambda i: (i, 0)),
            pl.BlockSpec((tm, LANES), lambda i: (i, 0)),
        ),
        compiler_params=_params("parallel"),
        name="in_projection",
    )(x2d, w_main, w_dt)


def _rope_kernel(pos_ref, inv_ref, sign_ref, cos_ref, sin_ref):
    ang = pos_ref[...].astype(F32) * inv_ref[...]
    cos_ref[...] = jnp.cos(ang)
    sin_ref[...] = jnp.sin(ang) * sign_ref[...]


def _rope_tables(positions):
    t = positions.size
    tm = min(1024, t)
    half = ATT_HEAD_DIM // 2
    inv_freq = ROPE_THETA ** (-jnp.arange(half, dtype=F32) / half)
    inv = jnp.tile(inv_freq, LANES // half)[None, :]
    sign = jnp.concatenate([-jnp.ones((LANES // 2,), F32), jnp.ones((LANES // 2,), F32)])[None, :]
    return pl.pallas_call(
        _rope_kernel,
        out_shape=(jax.ShapeDtypeStruct((t, LANES), F32), jax.ShapeDtypeStruct((t, LANES), F32)),
        grid=(t // tm,),
        in_specs=[
            pl.BlockSpec((tm, 1), lambda i: (i, 0)),
            pl.BlockSpec((1, LANES), lambda i: (0, 0)),
            pl.BlockSpec((1, LANES), lambda i: (0, 0)),
        ],
        out_specs=(pl.BlockSpec((tm, LANES), lambda i: (i, 0)), pl.BlockSpec((tm, LANES), lambda i: (i, 0))),
        compiler_params=_params("parallel"),
        name="rope_tables",
    )(positions.reshape(t, 1), inv, sign)


SSD_PAIRS = SSD_HEADS // 2


def _ssd_kernel(xbc_ref, z_ref, dt_ref, shift_ref, cw_ref, cb_ref, dtb_ref, alog_ref, dsk_ref, nw_ref, wp_ref,
                y_ref, state_ref, ext_ref, ybuf_ref):
    L = SSD_CHUNK
    c = pl.program_id(1)

    @pl.when(c == 0)
    def _():
        state_ref[...] = jnp.zeros(state_ref.shape, F32)
        ext_ref[0:L, :] = jnp.zeros((L, SSD_XBC), BF16)

    @pl.when(c > 0)
    def _():
        ext_ref[0:L, :] = ext_ref[L:2 * L, :]

    ext_ref[L:2 * L, :] = xbc_ref[...]

    taps = jnp.dot(shift_ref[...], ext_ref[...], preferred_element_type=F32)
    acc = cb_ref[...] + cw_ref[0:1, :] * taps[0:L]
    for k in range(1, SSD_CONV):
        acc = acc + cw_ref[k:k + 1, :] * taps[k * L:(k + 1) * L]
    act = _silu(acc)
    xs = act[:, :SSD_INNER]
    bm = act[:, SSD_INNER:SSD_INNER + SSD_GROUPS * SSD_STATE]
    cm = act[:, SSD_INNER + SSD_GROUPS * SSD_STATE:]

    x_dt = dt_ref[...] + dtb_ref[...]
    dt = jnp.maximum(x_dt, 0.0) + jnp.log1p(jnp.exp(-jnp.abs(x_dt)))
    a = -jnp.exp(alog_ref[...])
    row = lax.broadcasted_iota(jnp.int32, (L, L), 0)
    col = lax.broadcasted_iota(jnp.int32, (L, L), 1)
    causal = row >= col
    tril = causal.astype(F32)
    cs = jnp.dot(tril, dt * a, preferred_element_type=F32, precision=lax.Precision.HIGHEST)
    cs_t = cs.T
    lo = col < SSD_HEAD_DIM

    for g in range(SSD_GROUPS):
        bm_g = bm[:, g * SSD_STATE:(g + 1) * SSD_STATE].astype(BF16)
        cm_g = cm[:, g * SSD_STATE:(g + 1) * SSD_STATE].astype(BF16)
        cb_g = lax.dot_general(cm_g, bm_g, (((1,), (1,)), ((), ())), preferred_element_type=F32)
        pairs_per_group = SSD_PAIRS // SSD_GROUPS
        for jj in range(pairs_per_group):
            j = g * pairs_per_group + jj
            h0, h1 = 2 * j, 2 * j + 1
            sl = slice(j * LANES, (j + 1) * LANES)
            col0 = jnp.broadcast_to(cs[:, h0:h0 + 1], (L, L))
            col1 = jnp.broadcast_to(cs[:, h1:h1 + 1], (L, L))
            dec0 = jnp.where(causal, jnp.exp(col0 - cs_t[h0:h0 + 1, :]), 0.0)
            dec1 = jnp.where(causal, jnp.exp(col1 - cs_t[h1:h1 + 1, :]), 0.0)
            m = jnp.concatenate([cb_g * dec0, cb_g * dec1], axis=1).astype(BF16)
            dt_p = jnp.where(lo, jnp.broadcast_to(dt[:, h0:h0 + 1], (L, L)),
                             jnp.broadcast_to(dt[:, h1:h1 + 1], (L, L)))
            xs_p = xs[:, sl]
            xdt = xs_p * dt_p
            x2 = jnp.concatenate([jnp.where(lo, xdt, 0.0), jnp.where(lo, 0.0, xdt)], axis=0).astype(BF16)
            y_diag = jnp.dot(m, x2, preferred_element_type=F32)
            cs_p = jnp.where(lo, col0, col1)
            st = state_ref[j]
            y_off = jnp.exp(cs_p) * jnp.dot(cm_g, st.astype(BF16), preferred_element_type=F32)
            last = cs_p[L - 1:L, :]
            xdt_end = (xdt * jnp.exp(last - cs_p)).astype(BF16)
            new_st = lax.dot_general(bm_g, xdt_end, (((0,), (0,)), ((), ())), preferred_element_type=F32)
            state_ref[j] = st * jnp.exp(last) + new_st
            ybuf_ref[:, sl] = y_diag + y_off + xs_p * dsk_ref[:, sl]

    y = ybuf_ref[...] * _silu(z_ref[...].astype(F32))
    gw = SSD_INNER // SSD_GROUPS
    parts = []
    for g in range(SSD_GROUPS):
        yg = y[:, g * gw:(g + 1) * gw]
        parts.append(yg * lax.rsqrt(jnp.mean(yg * yg, axis=-1, keepdims=True) + LN_EPS))
    yn = jnp.concatenate(parts, axis=1) * nw_ref[...]
    y_ref[...] = jnp.dot(yn.astype(BF16), wp_ref[...], preferred_element_type=F32).astype(BF16)


def _ssd_mixer(h, dt_raw, bsz, seq, conv_w, conv_b, dt_bias, a_log, d_skip, norm_w, w_proj):
    t = bsz * seq
    L = SSD_CHUNK
    nc = seq // L
    pad = LANES - SSD_HEADS
    dtb = jnp.pad(dt_bias, (0, pad))[None, :]
    alog = jnp.pad(a_log, (0, pad))[None, :]
    dsk = jnp.repeat(d_skip, SSD_HEAD_DIM)[None, :]
    r = jnp.arange(SSD_CONV * L, dtype=jnp.int32)
    shift = (jnp.arange(2 * L, dtype=jnp.int32)[None, :]
             == (L + r % L - (SSD_CONV - 1) + r // L)[:, None]).astype(BF16)
    const = lambda b, c: (0, 0)
    return pl.pallas_call(
        _ssd_kernel,
        out_shape=jax.ShapeDtypeStruct((t, D_MODEL), BF16),
        grid=(bsz, nc),
        in_specs=[
            pl.BlockSpec((L, SSD_XBC), lambda b, c: (b * nc + c, COL_XBC // SSD_XBC)),
            pl.BlockSpec((L, SSD_INNER), lambda b, c: (b * nc + c, COL_Z // SSD_INNER)),
            pl.BlockSpec((L, LANES), lambda b, c: (b * nc + c, 0)),
            pl.BlockSpec((SSD_CONV * L, 2 * L), const),
            pl.BlockSpec((SSD_CONV, SSD_XBC), const),
            pl.BlockSpec((1, SSD_XBC), const),
            pl.BlockSpec((1, LANES), const),
            pl.BlockSpec((1, LANES), const),
            pl.BlockSpec((1, SSD_INNER), const),
            pl.BlockSpec((1, SSD_INNER), const),
            pl.BlockSpec((SSD_INNER, D_MODEL), const),
        ],
        out_specs=pl.BlockSpec((L, D_MODEL), lambda b, c: (b * nc + c, 0)),
        scratch_shapes=[
            pltpu.VMEM((SSD_PAIRS, SSD_STATE, LANES), F32),
            pltpu.VMEM((2 * L, SSD_XBC), BF16),
            pltpu.VMEM((L, SSD_INNER), F32),
        ],
        compiler_params=_params("parallel", "arbitrary"),
        name="ssd_mixer",
    )(h, h, dt_raw, shift, conv_w, conv_b[None, :], dtb, alog, dsk, norm_w[None, :], w_proj.astype(BF16))


CONF_TS = 256
CONF_HALO = 32
CONF_ROW_BLK = 64
CONF_COL_BLK = 256


def _conf_kernel(u_ref, dw_ref, db_ref, g_ref, b_ref, wp_ref, y_ref, ext_ref, conv_ref):
    ts = u_ref.shape[0]
    s = pl.program_id(1)

    @pl.when(s == 0)
    def _():
        ext_ref[:, 0:CONF_HALO, :] = jnp.zeros((SUBLANES, CONF_HALO, CONF_CH), F32)
        ext_ref[:, ts + CONF_HALO - SUBLANES:ts + CONF_HALO, :] = jnp.zeros((SUBLANES, SUBLANES, CONF_CH), F32)

    @pl.when(s > 0)
    def _():
        ext_ref[:, 0:CONF_HALO, :] = ext_ref[:, ts:ts + CONF_HALO, :]

    u = u_ref[...].astype(F32)
    glu = u[:, :CONF_CH] * _sigmoid(u[:, CONF_CH:])
    for b in range(SUBLANES):
        ext_ref[b, CONF_HALO - b:CONF_HALO - b + ts, :] = glu

    for r0 in range(0, ts, CONF_ROW_BLK):
        for c0 in range(0, CONF_CH, CONF_COL_BLK):
            cols = slice(c0, c0 + CONF_COL_BLK)
            acc = jnp.broadcast_to(db_ref[:, cols], (CONF_ROW_BLK, CONF_COL_BLK))
            for k in range(CONF_KERNEL):
                shift = CONF_HALO - (CONF_KERNEL - 1) + k
                off = shift - shift % SUBLANES + r0
                acc = acc + dw_ref[k:k + 1, cols] * ext_ref[shift % SUBLANES, off:off + CONF_ROW_BLK, cols]
            conv_ref[r0:r0 + CONF_ROW_BLK, cols] = acc

    hn = _silu(_layer_norm(conv_ref[...], g_ref[...], b_ref[...]))
    y_ref[...] = jnp.dot(hn.astype(BF16), wp_ref[...], preferred_element_type=F32).astype(BF16)


def _conformer(h, bsz, seq, dw_w, dw_b, ln_g, ln_b, w_proj):
    t = bsz * seq
    ts = min(CONF_TS, seq)
    nt = seq // ts
    const = lambda b, s: (0, 0)
    dw = jnp.pad(dw_w, ((0, CONF_HALO - CONF_KERNEL), (0, 0)))
    return pl.pallas_call(
        _conf_kernel,
        out_shape=jax.ShapeDtypeStruct((t, D_MODEL), BF16),
        grid=(bsz, nt),
        in_specs=[
            pl.BlockSpec((ts, 2 * CONF_CH), lambda b, s: (b * nt + s, COL_U // (2 * CONF_CH))),
            pl.BlockSpec((CONF_HALO, CONF_CH), const),
            pl.BlockSpec((1, CONF_CH), const),
            pl.BlockSpec((1, CONF_CH), const),
            pl.BlockSpec((1, CONF_CH), const),
            pl.BlockSpec((CONF_CH, D_MODEL), const),
        ],
        out_specs=pl.BlockSpec((ts, D_MODEL), lambda b, s: (b * nt + s, 0)),
        scratch_shapes=[
            pltpu.VMEM((SUBLANES, CONF_HALO + ts, CONF_CH), F32),
            pltpu.VMEM((ts, CONF_CH), F32),
        ],
        compiler_params=_params("parallel", "arbitrary"),
        name="conformer_conv",
    )(h, dw, dw_b[None, :], ln_g[None, :], ln_b[None, :], w_proj.astype(BF16))


def _pair_interleave(width):
    half = ATT_HEAD_DIM // 2
    j = jnp.arange(width, dtype=jnp.int32)
    block, lane = j // LANES, j % LANES
    chunk, within = lane // half, lane % half
    return block * LANES + (chunk % 2) * ATT_HEAD_DIM + (chunk // 2) * half + within


def _attn_kernel(q_ref, kc_ref, kp_ref, vc_ref, vp_ref, cosc_ref, sinc_ref, cosp_ref, sinp_ref, sink_ref,
                 wp_ref, y_ref, obuf_ref):
    blk = ATT_BLOCK
    i = pl.program_id(1)
    rep = ATT_Q_HEADS // ATT_KV_HEADS
    quarter = ATT_HEAD_DIM // 2
    scale = ATT_HEAD_DIM ** -0.5

    def lane_masks(rows):
        lane = lax.broadcasted_iota(jnp.int32, (rows, LANES), 1)
        return (lane % ATT_HEAD_DIM) < quarter, lane < ATT_HEAD_DIM

    def rope(x, cos, sin):
        return x * cos + pltpu.roll(x, ATT_HEAD_DIM, 1) * sin

    first_q, lo_q = lane_masks(blk)
    first_k, lo_k = lane_masks(2 * blk)

    cos_q, sin_q = cosc_ref[...], sinc_ref[...]
    cos_k = jnp.concatenate([cosp_ref[...], cosc_ref[...]], axis=0)
    sin_k = jnp.concatenate([sinp_ref[...], sinc_ref[...]], axis=0)
    k = jnp.concatenate([kp_ref[...], kc_ref[...]], axis=0).astype(F32)
    v = jnp.concatenate([vp_ref[...], vc_ref[...]], axis=0).astype(F32)

    qi = lax.broadcasted_iota(jnp.int32, (blk, 2 * blk), 0)
    kj = lax.broadcasted_iota(jnp.int32, (blk, 2 * blk), 1)
    low = jnp.where(i == 0, blk - 1, qi)
    mask = (kj <= qi + blk) & (kj > low)

    for kh in range(ATT_KV_HEADS):
        kb = rope(k[:, (kh // 2) * LANES:(kh // 2 + 1) * LANES], cos_k, sin_k)
        vb = v[:, (kh // 2) * LANES:(kh // 2 + 1) * LANES]
        if kh % 2 == 0:
            k_dup = jnp.where(first_k, kb, pltpu.roll(kb, quarter, 1))
            v_dup = jnp.where(lo_k, vb, pltpu.roll(vb, ATT_HEAD_DIM, 1))
        else:
            k_dup = jnp.where(first_k, pltpu.roll(kb, LANES - quarter, 1), kb)
            v_dup = jnp.where(lo_k, pltpu.roll(vb, ATT_HEAD_DIM, 1), vb)
        k_dup = k_dup.astype(BF16)
        v_ext = jnp.concatenate([v_dup, jnp.ones_like(v_dup)], axis=1).astype(BF16)

        stacked = []
        for qb in range(rep // 2):
            cols = slice((kh * (rep // 2) + qb) * LANES, (kh * (rep // 2) + qb + 1) * LANES)
            qr = rope(q_ref[:, cols].astype(F32), cos_q, sin_q) * scale
            stacked.append(jnp.where(first_q, qr, 0.0).astype(BF16))
            stacked.append(jnp.where(first_q, 0.0, qr).astype(BF16))
        qs = jnp.concatenate(stacked, axis=0)
        s_all = lax.dot_general(qs, k_dup, (((1,), (1,)), ((), ())), preferred_element_type=F32)

        es, sink_terms = [], []
        for r in range(rep):
            hq = kh * rep + r
            s = jnp.where(mask, s_all[r * blk:(r + 1) * blk], -jnp.inf)
            sink = sink_ref[hq:hq + 1, :]
            mx = jnp.maximum(jnp.broadcast_to(jnp.max(s, axis=-1, keepdims=True), (blk, LANES)), sink)
            sink_terms.append(jnp.exp(sink - mx))
            es.append(jnp.exp(s - jnp.concatenate([mx, mx], axis=1)).astype(BF16))
        o = jnp.dot(jnp.concatenate(es, axis=0), v_ext, preferred_element_type=F32)
        outs = []
        for r in range(rep):
            rows = slice(r * blk, (r + 1) * blk)
            outs.append(o[rows, :LANES] * (1.0 / (o[rows, LANES:] + sink_terms[r])))
        for qb in range(rep // 2):
            cols = slice((kh * (rep // 2) + qb) * LANES, (kh * (rep // 2) + qb + 1) * LANES)
            obuf_ref[:, cols] = jnp.where(lo_q, outs[2 * qb], outs[2 * qb + 1])

    y_ref[...] = jnp.dot(obuf_ref[...].astype(BF16), wp_ref[...], preferred_element_type=F32).astype(BF16)


def _attention(h, cos_t, sin_t, bsz, seq, sinks, w_proj):
    t = bsz * seq
    blk = ATT_BLOCK
    nb = seq // blk
    qw = ATT_Q_HEADS * ATT_HEAD_DIM
    kw = ATT_KV_HEADS * ATT_HEAD_DIM
    cur = lambda b, i: b * nb + i
    prev = lambda b, i: b * nb + jnp.maximum(i - 1, 0)
    const = lambda b, i: (0, 0)
    sink = jnp.broadcast_to(sinks[:, None], (ATT_Q_HEADS, LANES))
    return pl.pallas_call(
        _attn_kernel,
        out_shape=jax.ShapeDtypeStruct((t, D_MODEL), BF16),
        grid=(bsz, nb),
        in_specs=[
            pl.BlockSpec((blk, qw), lambda b, i: (cur(b, i), COL_Q // qw)),
            pl.BlockSpec((blk, kw), lambda b, i: (cur(b, i), COL_K // kw)),
            pl.BlockSpec((blk, kw), lambda b, i: (prev(b, i), COL_K // kw)),
            pl.BlockSpec((blk, kw), lambda b, i: (cur(b, i), COL_V // kw)),
            pl.BlockSpec((blk, kw), lambda b, i: (prev(b, i), COL_V // kw)),
            pl.BlockSpec((blk, LANES), lambda b, i: (cur(b, i), 0)),
            pl.BlockSpec((blk, LANES), lambda b, i: (cur(b, i), 0)),
            pl.BlockSpec((blk, LANES), lambda b, i: (prev(b, i), 0)),
            pl.BlockSpec((blk, LANES), lambda b, i: (prev(b, i), 0)),
            pl.BlockSpec((ATT_Q_HEADS, LANES), const),
            pl.BlockSpec((qw, D_MODEL), const),
        ],
        out_specs=pl.BlockSpec((blk, D_MODEL), lambda b, i: (cur(b, i), 0)),
        scratch_shapes=[pltpu.VMEM((blk, qw), F32)],
        compiler_params=_params("parallel", "arbitrary"),
        name="swa_attention",
    )(h, h, h, h, h, cos_t, sin_t, cos_t, sin_t, sink, w_proj.astype(BF16))


MERGE_TM = 512


def _merge_kernel(g0_ref, g1_ref, g2_ref, bg_ref, y0_ref, y1_ref, y2_ref, x_ref, wo_ref, lg_ref, lb_ref,
                  wrh_ref, wrl_ref, br_ref, x1_ref, logit_ref):
    mixed = None
    for n, (g_ref, y_ref) in enumerate(((g0_ref, y0_ref), (g1_ref, y1_ref), (g2_ref, y2_ref))):
        term = _sigmoid(g_ref[...].astype(F32) + bg_ref[n:n + 1, :]) * y_ref[...].astype(F32)
        mixed = term if mixed is None else mixed + term
    r = DEEPNORM_ALPHA * x_ref[...] + jnp.dot(mixed.astype(BF16), wo_ref[...], preferred_element_type=F32)
    x1 = _layer_norm(r, lg_ref[...], lb_ref[...])
    _store_row_tiles(x1_ref, x1)
    x_hi = x1.astype(BF16)
    x_lo = (x1 - x_hi.astype(F32)).astype(BF16)
    logit_ref[...] = (jnp.dot(x_hi, wrh_ref[...], preferred_element_type=F32)
                      + (jnp.dot(x_hi, wrl_ref[...], preferred_element_type=F32)
                         + jnp.dot(x_lo, wrh_ref[...], preferred_element_type=F32))) + br_ref[...]


def _merge(h, b_gate, y_ssd, y_conf, y_att, x2d, w_out, ln_g, ln_b, w_router, b_router):
    t = x2d.shape[0]
    tm = min(MERGE_TM, t)
    const = lambda i: (0, 0)
    row = lambda i: (i, 0)
    w_router_hi = w_router.astype(BF16)
    w_router_lo = (w_router - w_router_hi.astype(F32)).astype(BF16)
    return pl.pallas_call(
        _merge_kernel,
        out_shape=(jax.ShapeDtypeStruct((t * ROW_TILES, LANES), F32), jax.ShapeDtypeStruct((t, LANES), F32)),
        grid=(t // tm,),
        in_specs=[
            pl.BlockSpec((tm, D_MODEL), lambda i: (i, 0)),
            pl.BlockSpec((tm, D_MODEL), lambda i: (i, 1)),
            pl.BlockSpec((tm, D_MODEL), lambda i: (i, 2)),
            pl.BlockSpec((3, D_MODEL), const),
            pl.BlockSpec((tm, D_MODEL), row),
            pl.BlockSpec((tm, D_MODEL), row),
            pl.BlockSpec((tm, D_MODEL), row),
            pl.BlockSpec((tm, D_MODEL), row),
            pl.BlockSpec((D_MODEL, D_MODEL), const),
            pl.BlockSpec((1, D_MODEL), const),
            pl.BlockSpec((1, D_MODEL), const),
            pl.BlockSpec((D_MODEL, LANES), const),
            pl.BlockSpec((D_MODEL, LANES), const),
            pl.BlockSpec((1, LANES), const),
        ],
        out_specs=(pl.BlockSpec((tm * ROW_TILES, LANES), row), pl.BlockSpec((tm, LANES), row)),
        compiler_params=_params("parallel"),
        name="merge_ln_router",
    )(h, h, h, b_gate, y_ssd, y_conf, y_att, x2d, w_out.astype(BF16), ln_g[None, :], ln_b[None, :],
      w_router_hi, w_router_lo, b_router)


GATHER_UNROLL = 8


def _on_slot(slot, fn, n_slots=2):
    for s in range(n_slots):
        @pl.when(slot == s)
        def _(s=s):
            fn(s)


def _gather_pipeline(i, n_steps, idx_hbm, src_hbm, idx_smem, buf, isem, gsem):
    n_rows = buf[0].shape[0] // ROW_TILES
    slot = i % 2

    def idx_copy(step, s):
        return pltpu.make_async_copy(idx_hbm.at[step], idx_smem[s], isem.at[s])

    def issue_rows(s):
        def body(it, carry):
            for u in range(GATHER_UNROLL):
                r = it * GATHER_UNROLL + u
                src_row = pl.multiple_of(idx_smem[s][r] * ROW_TILES, ROW_TILES)
                dst_row = pl.multiple_of(r * ROW_TILES, ROW_TILES)
                pltpu.make_async_copy(src_hbm.at[pl.ds(src_row, ROW_TILES)],
                                      buf[s].at[pl.ds(dst_row, ROW_TILES)], gsem.at[s]).start(priority=u % 2)
            return carry

        lax.fori_loop(0, n_rows // GATHER_UNROLL, body, 0)

    @pl.when(i == 0)
    def _():
        idx_copy(0, 0).start()
        idx_copy(0, 0).wait()
        issue_rows(0)

        @pl.when(n_steps > 1)
        def _():
            idx_copy(1, 1).start()

    @pl.when(i + 1 < n_steps)
    def _():
        def prefetch(s):
            idx_copy(i + 1, 1 - s).wait()
            issue_rows(1 - s)

            @pl.when(i + 2 < n_steps)
            def _():
                idx_copy(i + 2, s).start()

        _on_slot(slot, prefetch)


def _gather_wait(s, src_hbm, buf, gsem):
    pltpu.make_async_copy(src_hbm.at[pl.ds(0, buf[s].shape[0])], buf[s], gsem.at[s]).wait()


DISPATCH_SLOTS = 3


def _dispatch_kernel(idx_hbm, x1_hbm, init_hbm, rows_hbm, idx0, idx1, idx2, st0, st1, st2, isem, xsem, dsem):
    del init_hbm
    i = pl.program_id(0)
    n_steps = pl.num_programs(0)
    idx = (idx0, idx1, idx2)
    stage = (st0, st1, st2)
    n_idx = idx0.shape[0]
    tm = n_idx // MOE_TOP_K

    def loads(step, s):
        first = pl.multiple_of(step * (tm * ROW_TILES), tm * ROW_TILES)
        return (pltpu.make_async_copy(idx_hbm.at[step], idx[s], isem.at[s]),
                pltpu.make_async_copy(x1_hbm.at[pl.ds(first, tm * ROW_TILES)], stage[s], xsem.at[s]))

    def start_loads(step, s):
        for c in loads(step, s):
            c.start()

    def wait_rows(s):
        pltpu.make_async_copy(x1_hbm.at[pl.ds(0, n_idx * ROW_TILES)], rows_hbm.at[pl.ds(0, n_idx * ROW_TILES)],
                              dsem.at[s]).wait()

    @pl.when(i == 0)
    def _():
        start_loads(0, 0)

        @pl.when(n_steps > 1)
        def _():
            start_loads(1, 1)

    def step(s):
        freed = (s + DISPATCH_SLOTS - 1) % DISPATCH_SLOTS
        for c in loads(i, s):
            c.wait()

        def body(it, carry):
            for u in range(GATHER_UNROLL):
                r = it * GATHER_UNROLL + u
                src_row = pl.multiple_of((r % tm) * ROW_TILES, ROW_TILES)
                dst_row = pl.multiple_of(idx[s][r] * ROW_TILES, ROW_TILES)
                pltpu.make_async_copy(stage[s].at[pl.ds(src_row, ROW_TILES)],
                                      rows_hbm.at[pl.ds(dst_row, ROW_TILES)], dsem.at[s]).start(priority=u % 2)
            return carry

        lax.fori_loop(0, n_idx // GATHER_UNROLL, body, 0)

        @pl.when(i > 0)
        def _():
            wait_rows(freed)

        @pl.when(i + 2 < n_steps)
        def _():
            start_loads(i + 2, freed)

        @pl.when(i == n_steps - 1)
        def _():
            wait_rows(s)

    _on_slot(i % DISPATCH_SLOTS, step, DISPATCH_SLOTS)


def _moe_dispatch(x1, dest_blocks, n_rows):
    n_steps, n_idx = dest_blocks.shape
    any_spec = pl.BlockSpec(memory_space=pl.ANY)
    return pl.pallas_call(
        _dispatch_kernel,
        out_shape=jax.ShapeDtypeStruct((n_rows * ROW_TILES, LANES), F32),
        grid=(n_steps,),
        in_specs=[any_spec, any_spec, any_spec],
        out_specs=any_spec,
        scratch_shapes=(
            [pltpu.SMEM((n_idx,), jnp.int32)] * DISPATCH_SLOTS
            + [pltpu.VMEM((n_idx // MOE_TOP_K * ROW_TILES, LANES), F32)] * DISPATCH_SLOTS
            + [pltpu.SemaphoreType.DMA((DISPATCH_SLOTS,))] * 3
        ),
        input_output_aliases={2: 0},
        compiler_params=_params("arbitrary"),
        name="moe_dispatch",
    )(dest_blocks, x1, jnp.zeros((n_rows * ROW_TILES, LANES), F32))


def _moe_kernel(be_ref, nu_ref, x_ref, wg_ref, wu_ref, wd_ref, y_ref):
    del be_ref
    i = pl.program_id(0)

    @pl.when(i < nu_ref[0])
    def _():
        xb = _load_row_tiles(x_ref, 0, MOE_BLOCK).astype(BF16)
        gate = jnp.dot(xb, wg_ref[...], preferred_element_type=F32)
        up = jnp.dot(xb, wu_ref[...], preferred_element_type=F32)
        hmid = (_silu(gate) * up).astype(BF16)
        _store_row_tiles(y_ref, jnp.dot(hmid, wd_ref[...], preferred_element_type=F32))

    @pl.when(i >= nu_ref[0])
    def _():
        y_ref[...] = jnp.zeros(y_ref.shape, F32)


def _moe_experts(x_rows, block_e, n_used, w_gate, w_up, w_down):
    n_blocks = block_e.shape[0]
    grid_spec = pltpu.PrefetchScalarGridSpec(
        num_scalar_prefetch=2,
        grid=(n_blocks,),
        in_specs=[
            pl.BlockSpec((MOE_BLOCK * ROW_TILES, LANES), lambda i, be, nu: (jnp.minimum(i, nu[0] - 1), 0)),
            pl.BlockSpec((None, D_MODEL, MOE_FF), lambda i, be, nu: (be[i], 0, 0)),
            pl.BlockSpec((None, D_MODEL, MOE_FF), lambda i, be, nu: (be[i], 0, 0)),
            pl.BlockSpec((None, MOE_FF, D_MODEL), lambda i, be, nu: (be[i], 0, 0)),
        ],
        out_specs=pl.BlockSpec((MOE_BLOCK * ROW_TILES, LANES), lambda i, be, nu: (i, 0)),
    )
    return pl.pallas_call(
        _moe_kernel,
        out_shape=jax.ShapeDtypeStruct((n_blocks * MOE_BLOCK * ROW_TILES, LANES), F32),
        grid_spec=grid_spec,
        compiler_params=_params("arbitrary"),
        name="moe_experts",
    )(block_e, n_used, x_rows, w_gate, w_up, w_down)


COMB_TM = 256


def _combine_kernel(idx_hbm, y_hbm, ew_ref, x1_ref, p_ref, lg_ref, lb_ref, wpg_ref, wpp_ref, out_ref,
                    idx0, idx1, ybuf0, ybuf1, ffn_ref, isem, gsem):
    i = pl.program_id(0)
    tm = out_ref.shape[0]
    ybuf = (ybuf0, ybuf1)
    _gather_pipeline(i, pl.num_programs(0), idx_hbm, y_hbm, (idx0, idx1), ybuf, isem, gsem)

    def weighted_sum(s):
        _gather_wait(s, y_hbm, ybuf, gsem)
        ew = ew_ref[...]
        ffn_ref[...] = (_load_row_tiles(ybuf[s], 0, tm) * ew[:, 0:1]
                        + _load_row_tiles(ybuf[s], tm, tm) * ew[:, 1:2])

    _on_slot(i % 2, weighted_sum)
    x1 = _load_row_tiles(x1_ref, 0, tm)
    x2 = _layer_norm(DEEPNORM_ALPHA * x1 + ffn_ref[...], lg_ref[...], lb_ref[...])
    gate = _sigmoid(jnp.dot(x2.astype(BF16), wpg_ref[...], preferred_element_type=F32))
    emb = jnp.dot(p_ref[...].astype(BF16), wpp_ref[...], preferred_element_type=F32)
    out_ref[...] = x2 + gate * emb


def _combine(y_rows, dest_blocks, e_w, x1, p2d, ln_g, ln_b, ple_w_gate, ple_w_proj):
    t = x1.shape[0] // ROW_TILES
    tm = dest_blocks.shape[1] // MOE_TOP_K
    const = lambda i: (0, 0)
    row = lambda i: (i, 0)
    return pl.pallas_call(
        _combine_kernel,
        out_shape=jax.ShapeDtypeStruct((t, D_MODEL), F32),
        grid=(t // tm,),
        in_specs=[
            pl.BlockSpec(memory_space=pl.ANY),
            pl.BlockSpec(memory_space=pl.ANY),
            pl.BlockSpec((tm, MOE_TOP_K), row),
            pl.BlockSpec((tm * ROW_TILES, LANES), row),
            pl.BlockSpec((tm, PLE_DIM), row),
            pl.BlockSpec((1, D_MODEL), const),
            pl.BlockSpec((1, D_MODEL), const),
            pl.BlockSpec((D_MODEL, D_MODEL), const),
            pl.BlockSpec((PLE_DIM, D_MODEL), const),
        ],
        out_specs=pl.BlockSpec((tm, D_MODEL), row),
        scratch_shapes=[
            pltpu.SMEM((MOE_TOP_K * tm,), jnp.int32),
            pltpu.SMEM((MOE_TOP_K * tm,), jnp.int32),
            pltpu.VMEM((MOE_TOP_K * tm * ROW_TILES, LANES), F32),
            pltpu.VMEM((MOE_TOP_K * tm * ROW_TILES, LANES), F32),
            pltpu.VMEM((tm, D_MODEL), F32),
            pltpu.SemaphoreType.DMA((2,)),
            pltpu.SemaphoreType.DMA((2,)),
        ],
        compiler_params=_params("arbitrary"),
        name="moe_combine",
    )(dest_blocks, y_rows, e_w, x1, p2d, ln_g[None, :], ln_b[None, :], ple_w_gate.astype(BF16),
      ple_w_proj.astype(BF16))


ROUTE_CHUNK = 256


def _route(logits):
    t = logits.shape[0]
    tk = t * MOE_TOP_K
    g_logits = logits[:, :MOE_GROUPS]
    e_logits = logits[:, MOE_GROUPS:MOE_GROUPS + MOE_EXPERTS].reshape(t, MOE_GROUPS, MOE_EXPERTS_PER_GROUP)
    g_prob = jax.nn.softmax(g_logits, axis=-1)
    g_idx = jnp.argmax(g_logits, axis=-1).astype(jnp.int32)
    g_w = jnp.take_along_axis(g_prob, g_idx[:, None], axis=1)
    e_in = jnp.take_along_axis(e_logits, g_idx[:, None, None], axis=1)[:, 0]
    lane = jnp.arange(MOE_EXPERTS_PER_GROUP, dtype=jnp.int32)[None, :]
    i1 = jnp.argmax(e_in, axis=-1).astype(jnp.int32)
    rest = jnp.where(lane == i1[:, None], -jnp.inf, e_in)
    i2 = jnp.argmax(rest, axis=-1).astype(jnp.int32)
    top_i = jnp.stack([i1, i2], axis=-1)
    top_v = jnp.stack([jnp.max(e_in, axis=-1), jnp.max(rest, axis=-1)], axis=-1)
    e_w = jax.nn.softmax(top_v, axis=-1) * g_w
    e_id = g_idx[:, None] * MOE_EXPERTS_PER_GROUP + top_i
    flat_e = e_id.reshape(tk)

    chunk = min(ROUTE_CHUNK, tk)
    onehot = flat_e[:, None] == jnp.arange(MOE_EXPERTS, dtype=jnp.int32)[None, :]
    oh = onehot.reshape(tk // chunk, chunk, MOE_EXPERTS)
    tril = jnp.tril(jnp.ones((chunk, chunk), BF16))
    within = jnp.einsum("ij,cje->cie", tril, oh.astype(BF16), preferred_element_type=F32).astype(jnp.int32)
    chunk_counts = within[:, -1, :]
    chunk_ends = jnp.cumsum(chunk_counts, axis=0)
    csum = within + (chunk_ends - chunk_counts)[:, None, :]
    rank = jnp.sum(jnp.where(oh, csum, 0), axis=-1).reshape(tk) - 1
    counts = chunk_ends[-1]
    padded = ((counts + MOE_BLOCK - 1) // MOE_BLOCK) * MOE_BLOCK
    pends = jnp.cumsum(padded)
    pstarts = pends - padded
    dest = pstarts[flat_e] + rank

    n_rows = tk + MOE_EXPERTS * MOE_BLOCK
    n_blocks = n_rows // MOE_BLOCK
    block_start = jnp.arange(n_blocks, dtype=jnp.int32) * MOE_BLOCK
    block_e = jnp.sum((pends[None, :] <= block_start[:, None]).astype(jnp.int32), axis=1)
    block_e = jnp.minimum(block_e, MOE_EXPERTS - 1).astype(jnp.int32)
    n_used = (pends[-1] // MOE_BLOCK).astype(jnp.int32).reshape(1)
    return e_w, dest.reshape(t, MOE_TOP_K), block_e, n_used, n_rows


def _layer(x2d, p2d, cos_t, sin_t, bsz, seq, w_in, b_gate, ssd_conv_w, ssd_conv_b, ssd_dt_bias, ssd_a_log,
           ssd_d, ssd_norm_w, ssd_w_out, conf_dw_w, conf_dw_b, conf_ln_g, conf_ln_b, conf_w_out, attn_sinks,
           attn_w_out, w_out, ln1_g, ln1_b, moe_w_group, moe_b_group, moe_w_expert, moe_b_expert, moe_w_gate,
           moe_w_up, moe_w_down, ln2_g, ln2_b, ple_w_gate, ple_w_proj):
    t = bsz * seq
    w_main = jnp.concatenate([w_in[:, :DT_COL_ORIG], w_in[:, DT_COL_ORIG + SSD_HEADS:]], axis=1)
    w_main = jnp.concatenate([w_main[:, :COL_Q], w_main[:, COL_Q:COL_V][:, _pair_interleave(COL_V - COL_Q)],
                              w_main[:, COL_V:]], axis=1).astype(BF16)
    w_dt = jnp.pad(w_in[:, DT_COL_ORIG:DT_COL_ORIG + SSD_HEADS], ((0, 0), (0, LANES - SSD_HEADS))).astype(BF16)
    h, dt_raw = _in_projection(x2d, w_main, w_dt)

    y_ssd = _ssd_mixer(h, dt_raw, bsz, seq, ssd_conv_w, ssd_conv_b, ssd_dt_bias, ssd_a_log, ssd_d, ssd_norm_w,
                       ssd_w_out)
    y_conf = _conformer(h, bsz, seq, conf_dw_w, conf_dw_b, conf_ln_g, conf_ln_b, conf_w_out)
    y_att = _attention(h, cos_t, sin_t, bsz, seq, attn_sinks, attn_w_out)

    n_router = MOE_GROUPS + MOE_EXPERTS
    w_router = jnp.pad(jnp.concatenate([moe_w_group, moe_w_expert], axis=1), ((0, 0), (0, LANES - n_router)))
    b_router = jnp.pad(jnp.concatenate([moe_b_group, moe_b_expert]), (0, LANES - n_router))[None, :]
    x1, logits = _merge(h, b_gate, y_ssd, y_conf, y_att, x2d, w_out, ln1_g, ln1_b, w_router, b_router)

    e_w, dest, block_e, n_used, n_rows = _route(logits)
    tm = min(COMB_TM, t)
    dest_blocks = dest.reshape(t // tm, tm, MOE_TOP_K).transpose(0, 2, 1).reshape(t // tm, MOE_TOP_K * tm)
    x_rows = _moe_dispatch(x1, dest_blocks, n_rows)
    y_rows = _moe_experts(x_rows, block_e, n_used, moe_w_gate.astype(BF16), moe_w_up.astype(BF16),
                          moe_w_down.astype(BF16))
    return _combine(y_rows, dest_blocks, e_w, x1, p2d, ln2_g, ln2_b, ple_w_gate, ple_w_proj)


def kernel(x, p, positions, w_in, b_gate, ssd_conv_w, ssd_conv_b, ssd_dt_bias, ssd_a_log, ssd_d, ssd_norm_w, ssd_w_out, conf_dw_w, conf_dw_b, conf_ln_g, conf_ln_b, conf_w_out, attn_sinks, attn_w_out, w_out, ln1_g, ln1_b, moe_w_group, moe_b_group, moe_w_expert, moe_b_expert, moe_w_gate, moe_w_up, moe_w_down, ln2_g, ln2_b, ple_w_gate, ple_w_proj):
    bsz, seq, d = x.shape
    t = bsz * seq
    cos_t, sin_t = _rope_tables(positions)
    x2d = x.reshape(t, d)
    per_layer = (w_in, b_gate, ssd_conv_w, ssd_conv_b, ssd_dt_bias, ssd_a_log, ssd_d, ssd_norm_w, ssd_w_out,
                 conf_dw_w, conf_dw_b, conf_ln_g, conf_ln_b, conf_w_out, attn_sinks, attn_w_out, w_out, ln1_g,
                 ln1_b, moe_w_group, moe_b_group, moe_w_expert, moe_b_expert, moe_w_gate, moe_w_up, moe_w_down,
                 ln2_g, ln2_b, ple_w_gate, ple_w_proj)
    for layer in range(w_in.shape[0]):
        x2d = _layer(x2d, p[layer].reshape(t, -1), cos_t, sin_t, bsz, seq, *(w[layer] for w in per_layer))
    return x2d.reshape(bsz, seq, d)
```

```python
import functools

import jax
import jax.numpy as jnp
from jax import lax
from jax.experimental import pallas as pl
from jax.experimental.pallas import tpu as pltpu

F32 = jnp.float32
BF16 = jnp.bfloat16

D_MODEL = 1024
N_LAYERS = 2
PLE_DIM = 256
SSD_HEADS = 16
SSD_HEAD_DIM = 64
SSD_INNER = SSD_HEADS * SSD_HEAD_DIM
SSD_GROUPS = 4
SSD_STATE = 128
SSD_CONV = 4
SSD_CHUNK = 128
SSD_XBC = SSD_INNER + 2 * SSD_GROUPS * SSD_STATE
CONF_CH = 1024
CONF_KERNEL = 31
ATT_Q_HEADS = 16
ATT_KV_HEADS = 4
ATT_HEAD_DIM = 64
ATT_BLOCK = 128
ROPE_THETA = 10000.0
MOE_GROUPS = 4
MOE_EXPERTS_PER_GROUP = 8
MOE_EXPERTS = MOE_GROUPS * MOE_EXPERTS_PER_GROUP
MOE_TOP_K = 2
MOE_FF = 512
MOE_BLOCK = 256
DEEPNORM_ALPHA = (2 * N_LAYERS) ** 0.25
LN_EPS = 1e-5

LANES = 128
SUBLANES = 8
VMEM_LIMIT_BYTES = 56 * 1024 * 1024

COL_GATES = 0
COL_Z = 3 * D_MODEL
COL_XBC = COL_Z + SSD_INNER
COL_U = COL_XBC + SSD_XBC
COL_Q = COL_U + 2 * CONF_CH
COL_K = COL_Q + ATT_Q_HEADS * ATT_HEAD_DIM
COL_V = COL_K + ATT_KV_HEADS * ATT_HEAD_DIM
H_WIDTH = COL_V + ATT_KV_HEADS * ATT_HEAD_DIM
DT_COL_ORIG = 3 * D_MODEL + SSD_INNER + SSD_XBC


def _params(*semantics):
    return pltpu.CompilerParams(dimension_semantics=semantics, vmem_limit_bytes=VMEM_LIMIT_BYTES)


def _sigmoid(x):
    return 0.5 * (jnp.tanh(0.5 * x) + 1.0)


def _silu(x):
    return x * jax.nn.sigmoid(x)


def _layer_norm(x, g, b):
    mu = jnp.mean(x, axis=-1, keepdims=True)
    xc = x - mu
    var = jnp.mean(xc * xc, axis=-1, keepdims=True)
    return xc * lax.rsqrt(var + LN_EPS) * g + b


ROW_TILES = D_MODEL // LANES


def _store_row_tiles(ref, x):
    rows = x.shape[0]
    for k in range(ROW_TILES):
        ref[pl.ds(k, rows, stride=ROW_TILES), :] = x[:, k * LANES:(k + 1) * LANES]


def _load_row_tiles(ref, first_row, rows):
    return jnp.concatenate(
        [ref[pl.ds(first_row * ROW_TILES + k, rows, stride=ROW_TILES), :] for k in range(ROW_TILES)], axis=1)


IN_TM = 512
IN_TN = 512


def _inproj_kernel(x_ref, w_ref, wdt_ref, h_ref, dt_ref):
    xb = x_ref[...].astype(BF16)
    dt_ref[...] = jnp.dot(xb, wdt_ref[...], preferred_element_type=F32)
    for j in range(H_WIDTH // IN_TN):
        cols = slice(j * IN_TN, (j + 1) * IN_TN)
        h_ref[:, cols] = jnp.dot(xb, w_ref[:, cols], preferred_element_type=F32).astype(BF16)


def _in_projection(x2d, w_main, w_dt):
    t = x2d.shape[0]
    tm = min(IN_TM, t)
    resident = pl.Buffered(1)
    return pl.pallas_call(
        _inproj_kernel,
        out_shape=(jax.ShapeDtypeStruct((t, H_WIDTH), BF16), jax.ShapeDtypeStruct((t, LANES), F32)),
        grid=(t // tm,),
        in_specs=[
            pl.BlockSpec((tm, D_MODEL), lambda i: (i, 0)),
            pl.BlockSpec((D_MODEL, H_WIDTH), lambda i: (0, 0), pipeline_mode=resident),
            pl.BlockSpec((D_MODEL, LANES), lambda i: (0, 0), pipeline_mode=resident),
        ],
        out_specs=(
            pl.BlockSpec((tm, H_WIDTH), lambda i: (i, 0)),
            pl.BlockSpec((tm, LANES), lambda i: (i, 0)),
        ),
        compiler_params=_params("parallel"),
        name="in_projection",
    )(x2d, w_main, w_dt)


def _rope_kernel(pos_ref, inv_ref, sign_ref, cos_ref, sin_ref):
    ang = pos_ref[...].astype(F32) * inv_ref[...]
    cos_ref[...] = jnp.cos(ang)
    sin_ref[...] = jnp.sin(ang) * sign_ref[...]


def _rope_tables(positions):
    t = positions.size
    tm = min(1024, t)
    half = ATT_HEAD_DIM // 2
    inv_freq = ROPE_THETA ** (-jnp.arange(half, dtype=F32) / half)
    inv = jnp.tile(inv_freq, LANES // half)[None, :]
    sign = jnp.concatenate([-jnp.ones((LANES // 2,), F32), jnp.ones((LANES // 2,), F32)])[None, :]
    return pl.pallas_call(
        _rope_kernel,
        out_shape=(jax.ShapeDtypeStruct((t, LANES), F32), jax.ShapeDtypeStruct((t, LANES), F32)),
        grid=(t // tm,),
        in_specs=[
            pl.BlockSpec((tm, 1), lambda i: (i, 0)),
            pl.BlockSpec((1, LANES), lambda i: (0, 0)),
            pl.BlockSpec((1, LANES), lambda i: (0, 0)),
        ],
        out_specs=(pl.BlockSpec((tm, LANES), lambda i: (i, 0)), pl.BlockSpec((tm, LANES), lambda i: (i, 0))),
        compiler_params=_params("parallel"),
        name="rope_tables",
    )(positions.reshape(t, 1), inv, sign)


SSD_PAIRS = SSD_HEADS // 2
SSD_STEP_CHUNKS = 4


def _ssd_kernel(xbc_ref, z_ref, dt_ref, shift_ref, cw_ref, cb_ref, dtb_ref, alog_ref, dsk_ref, nw_ref, wp_ref,
                y_ref, state_ref, ext_ref, ybuf_ref):
    L = SSD_CHUNK
    n_sub = xbc_ref.shape[0] // L
    c = pl.program_id(1)

    @pl.when(c == 0)
    def _():
        state_ref[...] = jnp.zeros(state_ref.shape, F32)
        ext_ref[0:L, :] = jnp.zeros((L, SSD_XBC), BF16)

    @pl.when(c > 0)
    def _():
        ext_ref[0:L, :] = ext_ref[n_sub * L:(n_sub + 1) * L, :]

    ext_ref[L:(n_sub + 1) * L, :] = xbc_ref[...]

    row = lax.broadcasted_iota(jnp.int32, (L, L), 0)
    col = lax.broadcasted_iota(jnp.int32, (L, L), 1)
    causal = row >= col
    tril = causal.astype(F32)
    lo = col < SSD_HEAD_DIM
    a = -jnp.exp(alog_ref[...])

    for sub in range(n_sub):
        rows = slice(sub * L, (sub + 1) * L)

        taps = jnp.dot(shift_ref[...], ext_ref[sub * L:(sub + 2) * L, :], preferred_element_type=F32)
        acc = cb_ref[...] + cw_ref[0:1, :] * taps[0:L]
        for k in range(1, SSD_CONV):
            acc = acc + cw_ref[k:k + 1, :] * taps[k * L:(k + 1) * L]
        act = _silu(acc)
        xs = act[:, :SSD_INNER]
        bm = act[:, SSD_INNER:SSD_INNER + SSD_GROUPS * SSD_STATE]
        cm = act[:, SSD_INNER + SSD_GROUPS * SSD_STATE:]

        x_dt = dt_ref[rows, :] + dtb_ref[...]
        dt = jnp.maximum(x_dt, 0.0) + jnp.log1p(jnp.exp(-jnp.abs(x_dt)))
        cs = jnp.dot(tril, dt * a, preferred_element_type=F32, precision=lax.Precision.HIGHEST)
        cs_t = cs.T

        for g in range(SSD_GROUPS):
            bm_g = bm[:, g * SSD_STATE:(g + 1) * SSD_STATE].astype(BF16)
            cm_g = cm[:, g * SSD_STATE:(g + 1) * SSD_STATE].astype(BF16)
            cb_g = lax.dot_general(cm_g, bm_g, (((1,), (1,)), ((), ())), preferred_element_type=F32)
            pairs_per_group = SSD_PAIRS // SSD_GROUPS
            for jj in range(pairs_per_group):
                j = g * pairs_per_group + jj
                h0, h1 = 2 * j, 2 * j + 1
                sl = slice(j * LANES, (j + 1) * LANES)
                col0 = jnp.broadcast_to(cs[:, h0:h0 + 1], (L, L))
                col1 = jnp.broadcast_to(cs[:, h1:h1 + 1], (L, L))
                dec0 = jnp.where(causal, jnp.exp(col0 - cs_t[h0:h0 + 1, :]), 0.0)
                dec1 = jnp.where(causal, jnp.exp(col1 - cs_t[h1:h1 + 1, :]), 0.0)
                m = jnp.concatenate([cb_g * dec0, cb_g * dec1], axis=1).astype(BF16)
                dt_p = jnp.where(lo, jnp.broadcast_to(dt[:, h0:h0 + 1], (L, L)),
                                 jnp.broadcast_to(dt[:, h1:h1 + 1], (L, L)))
                xs_p = xs[:, sl]
                xdt = xs_p * dt_p
                x2 = jnp.concatenate([jnp.where(lo, xdt, 0.0), jnp.where(lo, 0.0, xdt)], axis=0).astype(BF16)
                y_diag = jnp.dot(m, x2, preferred_element_type=F32)
                cs_p = jnp.where(lo, col0, col1)
                st = state_ref[j]
                y_off = jnp.exp(cs_p) * jnp.dot(cm_g, st.astype(BF16), preferred_element_type=F32)
                last = cs_p[L - 1:L, :]
                xdt_end = (xdt * jnp.exp(last - cs_p)).astype(BF16)
                new_st = lax.dot_general(bm_g, xdt_end, (((0,), (0,)), ((), ())), preferred_element_type=F32)
                state_ref[j] = st * jnp.exp(last) + new_st
                ybuf_ref[rows, sl] = y_diag + y_off + xs_p * dsk_ref[:, sl]

    y = ybuf_ref[...] * _silu(z_ref[...].astype(F32))
    gw = SSD_INNER // SSD_GROUPS
    parts = []
    for g in range(SSD_GROUPS):
        yg = y[:, g * gw:(g + 1) * gw]
        parts.append(yg * lax.rsqrt(jnp.mean(yg * yg, axis=-1, keepdims=True) + LN_EPS))
    yn = jnp.concatenate(parts, axis=1) * nw_ref[...]
    y_ref[...] = jnp.dot(yn.astype(BF16), wp_ref[...], preferred_element_type=F32).astype(BF16)


def _ssd_mixer(h, dt_raw, bsz, seq, conv_w, conv_b, dt_bias, a_log, d_skip, norm_w, w_proj):
    t = bsz * seq
    L = SSD_CHUNK
    ts = min(SSD_STEP_CHUNKS * L, seq)
    nc = seq // ts
    pad = LANES - SSD_HEADS
    dtb = jnp.pad(dt_bias, (0, pad))[None, :]
    alog = jnp.pad(a_log, (0, pad))[None, :]
    dsk = jnp.repeat(d_skip, SSD_HEAD_DIM)[None, :]
    r = jnp.arange(SSD_CONV * L, dtype=jnp.int32)
    shift = (jnp.arange(2 * L, dtype=jnp.int32)[None, :]
             == (L + r % L - (SSD_CONV - 1) + r // L)[:, None]).astype(BF16)
    const = lambda b, c: (0, 0)
    return pl.pallas_call(
        _ssd_kernel,
        out_shape=jax.ShapeDtypeStruct((t, D_MODEL), BF16),
        grid=(bsz, nc),
        in_specs=[
            pl.BlockSpec((ts, SSD_XBC), lambda b, c: (b * nc + c, COL_XBC // SSD_XBC)),
            pl.BlockSpec((ts, SSD_INNER), lambda b, c: (b * nc + c, COL_Z // SSD_INNER)),
            pl.BlockSpec((ts, LANES), lambda b, c: (b * nc + c, 0)),
            pl.BlockSpec((SSD_CONV * L, 2 * L), const),
            pl.BlockSpec((SSD_CONV, SSD_XBC), const),
            pl.BlockSpec((1, SSD_XBC), const),
            pl.BlockSpec((1, LANES), const),
            pl.BlockSpec((1, LANES), const),
            pl.BlockSpec((1, SSD_INNER), const),
            pl.BlockSpec((1, SSD_INNER), const),
            pl.BlockSpec((SSD_INNER, D_MODEL), const),
        ],
        out_specs=pl.BlockSpec((ts, D_MODEL), lambda b, c: (b * nc + c, 0)),
        scratch_shapes=[
            pltpu.VMEM((SSD_PAIRS, SSD_STATE, LANES), F32),
            pltpu.VMEM((ts + L, SSD_XBC), BF16),
            pltpu.VMEM((ts, SSD_INNER), F32),
        ],
        compiler_params=_params("parallel", "arbitrary"),
        name="ssd_mixer",
    )(h, h, dt_raw, shift, conv_w, conv_b[None, :], dtb, alog, dsk, norm_w[None, :], w_proj.astype(BF16))


CONF_TS = 256
CONF_HALO = 32
CONF_ROW_BLK = 64
CONF_COL_BLK = 256


def _conf_kernel(u_ref, dw_ref, db_ref, g_ref, b_ref, wp_ref, y_ref, ext_ref, conv_ref):
    ts = u_ref.shape[0]
    s = pl.program_id(1)

    @pl.when(s == 0)
    def _():
        ext_ref[:, 0:CONF_HALO, :] = jnp.zeros((SUBLANES, CONF_HALO, CONF_CH), F32)
        ext_ref[:, ts + CONF_HALO - SUBLANES:ts + CONF_HALO, :] = jnp.zeros((SUBLANES, SUBLANES, CONF_CH), F32)

    @pl.when(s > 0)
    def _():
        ext_ref[:, 0:CONF_HALO, :] = ext_ref[:, ts:ts + CONF_HALO, :]

    u = u_ref[...].astype(F32)
    glu = u[:, :CONF_CH] * _sigmoid(u[:, CONF_CH:])
    for b in range(SUBLANES):
        ext_ref[b, CONF_HALO - b:CONF_HALO - b + ts, :] = glu

    for r0 in range(0, ts, CONF_ROW_BLK):
        for c0 in range(0, CONF_CH, CONF_COL_BLK):
            cols = slice(c0, c0 + CONF_COL_BLK)
            acc = jnp.broadcast_to(db_ref[:, cols], (CONF_ROW_BLK, CONF_COL_BLK))
            for k in range(CONF_KERNEL):
                shift = CONF_HALO - (CONF_KERNEL - 1) + k
                off = shift - shift % SUBLANES + r0
                acc = acc + dw_ref[k:k + 1, cols] * ext_ref[shift % SUBLANES, off:off + CONF_ROW_BLK, cols]
            conv_ref[r0:r0 + CONF_ROW_BLK, cols] = acc

    hn = _silu(_layer_norm(conv_ref[...], g_ref[...], b_ref[...]))
    y_ref[...] = jnp.dot(hn.astype(BF16), wp_ref[...], preferred_element_type=F32).astype(BF16)


def _conformer(h, bsz, seq, dw_w, dw_b, ln_g, ln_b, w_proj):
    t = bsz * seq
    ts = min(CONF_TS, seq)
    nt = seq // ts
    const = lambda b, s: (0, 0)
    dw = jnp.pad(dw_w, ((0, CONF_HALO - CONF_KERNEL), (0, 0)))
    return pl.pallas_call(
        _conf_kernel,
        out_shape=jax.ShapeDtypeStruct((t, D_MODEL), BF16),
        grid=(bsz, nt),
        in_specs=[
            pl.BlockSpec((ts, 2 * CONF_CH), lambda b, s: (b * nt + s, COL_U // (2 * CONF_CH))),
            pl.BlockSpec((CONF_HALO, CONF_CH), const),
            pl.BlockSpec((1, CONF_CH), const),
            pl.BlockSpec((1, CONF_CH), const),
            pl.BlockSpec((1, CONF_CH), const),
            pl.BlockSpec((CONF_CH, D_MODEL), const),
        ],
        out_specs=pl.BlockSpec((ts, D_MODEL), lambda b, s: (b * nt + s, 0)),
        scratch_shapes=[
            pltpu.VMEM((SUBLANES, CONF_HALO + ts, CONF_CH), F32),
            pltpu.VMEM((ts, CONF_CH), F32),
        ],
        compiler_params=_params("parallel", "arbitrary"),
        name="conformer_conv",
    )(h, dw, dw_b[None, :], ln_g[None, :], ln_b[None, :], w_proj.astype(BF16))


def _pair_interleave(width):
    half = ATT_HEAD_DIM // 2
    j = jnp.arange(width, dtype=jnp.int32)
    block, lane = j // LANES, j % LANES
    chunk, within = lane // half, lane % half
    return block * LANES + (chunk % 2) * ATT_HEAD_DIM + (chunk // 2) * half + within


def _attn_kernel(q_ref, kc_ref, kp_ref, vc_ref, vp_ref, cosc_ref, sinc_ref, cosp_ref, sinp_ref, sink_ref,
                 wp_ref, y_ref, obuf_ref):
    blk = ATT_BLOCK
    n_sub = q_ref.shape[0] // blk
    i = pl.program_id(1)
    rep = ATT_Q_HEADS // ATT_KV_HEADS
    quarter = ATT_HEAD_DIM // 2
    scale = ATT_HEAD_DIM ** -0.5

    def lane_masks(rows):
        lane = lax.broadcasted_iota(jnp.int32, (rows, LANES), 1)
        return (lane % ATT_HEAD_DIM) < quarter, lane < ATT_HEAD_DIM

    def rope(x, cos, sin):
        return x * cos + pltpu.roll(x, ATT_HEAD_DIM, 1) * sin

    first_q, lo_q = lane_masks(blk)
    first_k, lo_k = lane_masks((n_sub + 1) * blk)

    cos_k = jnp.concatenate([cosp_ref[...], cosc_ref[...]], axis=0)
    sin_k = jnp.concatenate([sinp_ref[...], sinc_ref[...]], axis=0)
    k = jnp.concatenate([kp_ref[...], kc_ref[...]], axis=0).astype(F32)
    v = jnp.concatenate([vp_ref[...], vc_ref[...]], axis=0).astype(F32)

    qi = lax.broadcasted_iota(jnp.int32, (blk, 2 * blk), 0)
    kj = lax.broadcasted_iota(jnp.int32, (blk, 2 * blk), 1)
    mask_inner = (kj <= qi + blk) & (kj > qi)
    mask_first = (kj <= qi + blk) & (kj > jnp.where(i == 0, blk - 1, qi))

    for kh in range(ATT_KV_HEADS):
        kb = rope(k[:, (kh // 2) * LANES:(kh // 2 + 1) * LANES], cos_k, sin_k)
        vb = v[:, (kh // 2) * LANES:(kh // 2 + 1) * LANES]
        if kh % 2 == 0:
            k_dup = jnp.where(first_k, kb, pltpu.roll(kb, quarter, 1))
            v_dup = jnp.where(lo_k, vb, pltpu.roll(vb, ATT_HEAD_DIM, 1))
        else:
            k_dup = jnp.where(first_k, pltpu.roll(kb, LANES - quarter, 1), kb)
            v_dup = jnp.where(lo_k, pltpu.roll(vb, ATT_HEAD_DIM, 1), vb)
        k_dup = k_dup.astype(BF16)
        v_ext = jnp.concatenate([v_dup, jnp.ones_like(v_dup)], axis=1).astype(BF16)

        for j in range(n_sub):
            q_rows = slice(j * blk, (j + 1) * blk)
            key_rows = slice(j * blk, (j + 2) * blk)
            mask = mask_first if j == 0 else mask_inner
            cos_q, sin_q = cosc_ref[q_rows, :], sinc_ref[q_rows, :]

            stacked = []
            for qb in range(rep // 2):
                cols = slice((kh * (rep // 2) + qb) * LANES, (kh * (rep // 2) + qb + 1) * LANES)
                qr = rope(q_ref[q_rows, cols].astype(F32), cos_q, sin_q) * scale
                stacked.append(jnp.where(first_q, qr, 0.0).astype(BF16))
                stacked.append(jnp.where(first_q, 0.0, qr).astype(BF16))
            qs = jnp.concatenate(stacked, axis=0)
            s_all = lax.dot_general(qs, k_dup[key_rows], (((1,), (1,)), ((), ())), preferred_element_type=F32)

            es, sink_terms = [], []
            for r in range(rep):
                hq = kh * rep + r
                s = jnp.where(mask, s_all[r * blk:(r + 1) * blk], -jnp.inf)
                sink = sink_ref[hq:hq + 1, :]
                mx = jnp.maximum(jnp.broadcast_to(jnp.max(s, axis=-1, keepdims=True), (blk, LANES)), sink)
                sink_terms.append(jnp.exp(sink - mx))
                es.append(jnp.exp(s - jnp.concatenate([mx, mx], axis=1)).astype(BF16))
            o = jnp.dot(jnp.concatenate(es, axis=0), v_ext[key_rows], preferred_element_type=F32)
            outs = []
            for r in range(rep):
                rows = slice(r * blk, (r + 1) * blk)
                outs.append(o[rows, :LANES] * (1.0 / (o[rows, LANES:] + sink_terms[r])))
            for qb in range(rep // 2):
                cols = slice((kh * (rep // 2) + qb) * LANES, (kh * (rep // 2) + qb + 1) * LANES)
                obuf_ref[q_rows, cols] = jnp.where(lo_q, outs[2 * qb], outs[2 * qb + 1])

    y_ref[...] = jnp.dot(obuf_ref[...].astype(BF16), wp_ref[...], preferred_element_type=F32).astype(BF16)


ATT_TQ = 4 * ATT_BLOCK


def _attention(h, cos_t, sin_t, bsz, seq, sinks, w_proj):
    t = bsz * seq
    blk = ATT_BLOCK
    tq = min(ATT_TQ, seq)
    n_sub = tq // blk
    nt = seq // tq
    nb = seq // blk
    qw = ATT_Q_HEADS * ATT_HEAD_DIM
    kw = ATT_KV_HEADS * ATT_HEAD_DIM
    cur = lambda b, i: b * nt + i
    prev = lambda b, i: b * nb + jnp.maximum(i * n_sub - 1, 0)
    const = lambda b, i: (0, 0)
    sink = jnp.broadcast_to(sinks[:, None], (ATT_Q_HEADS, LANES))
    return pl.pallas_call(
        _attn_kernel,
        out_shape=jax.ShapeDtypeStruct((t, D_MODEL), BF16),
        grid=(bsz, nt),
        in_specs=[
            pl.BlockSpec((tq, qw), lambda b, i: (cur(b, i), COL_Q // qw)),
            pl.BlockSpec((tq, kw), lambda b, i: (cur(b, i), COL_K // kw)),
            pl.BlockSpec((blk, kw), lambda b, i: (prev(b, i), COL_K // kw)),
            pl.BlockSpec((tq, kw), lambda b, i: (cur(b, i), COL_V // kw)),
            pl.BlockSpec((blk, kw), lambda b, i: (prev(b, i), COL_V // kw)),
            pl.BlockSpec((tq, LANES), lambda b, i: (cur(b, i), 0)),
            pl.BlockSpec((tq, LANES), lambda b, i: (cur(b, i), 0)),
            pl.BlockSpec((blk, LANES), lambda b, i: (prev(b, i), 0)),
            pl.BlockSpec((blk, LANES), lambda b, i: (prev(b, i), 0)),
            pl.BlockSpec((ATT_Q_HEADS, LANES), const),
            pl.BlockSpec((qw, D_MODEL), const),
        ],
        out_specs=pl.BlockSpec((tq, D_MODEL), lambda b, i: (cur(b, i), 0)),
        scratch_shapes=[pltpu.VMEM((tq, qw), F32)],
        compiler_params=_params("parallel", "arbitrary"),
        name="swa_attention",
    )(h, h, h, h, h, cos_t, sin_t, cos_t, sin_t, sink, w_proj.astype(BF16))


MERGE_TM = 512


def _merge_kernel(g0_ref, g1_ref, g2_ref, bg_ref, y0_ref, y1_ref, y2_ref, x_ref, wo_ref, lg_ref, lb_ref,
                  wrh_ref, wrl_ref, br_ref, x1_ref, logit_ref):
    mixed = None
    for n, (g_ref, y_ref) in enumerate(((g0_ref, y0_ref), (g1_ref, y1_ref), (g2_ref, y2_ref))):
        term = _sigmoid(g_ref[...].astype(F32) + bg_ref[n:n + 1, :]) * y_ref[...].astype(F32)
        mixed = term if mixed is None else mixed + term
    r = DEEPNORM_ALPHA * x_ref[...] + jnp.dot(mixed.astype(BF16), wo_ref[...], preferred_element_type=F32)
    x1 = _layer_norm(r, lg_ref[...], lb_ref[...])
    _store_row_tiles(x1_ref, x1)
    x_hi = x1.astype(BF16)
    x_lo = (x1 - x_hi.astype(F32)).astype(BF16)
    logit_ref[...] = (jnp.dot(x_hi, wrh_ref[...], preferred_element_type=F32)
                      + (jnp.dot(x_hi, wrl_ref[...], preferred_element_type=F32)
                         + jnp.dot(x_lo, wrh_ref[...], preferred_element_type=F32))) + br_ref[...]


def _merge(h, b_gate, y_ssd, y_conf, y_att, x2d, w_out, ln_g, ln_b, w_router, b_router):
    t = x2d.shape[0]
    tm = min(MERGE_TM, t)
    const = lambda i: (0, 0)
    row = lambda i: (i, 0)
    w_router_hi = w_router.astype(BF16)
    w_router_lo = (w_router - w_router_hi.astype(F32)).astype(BF16)
    return pl.pallas_call(
        _merge_kernel,
        out_shape=(jax.ShapeDtypeStruct((t * ROW_TILES, LANES), F32), jax.ShapeDtypeStruct((t, LANES), F32)),
        grid=(t // tm,),
        in_specs=[
            pl.BlockSpec((tm, D_MODEL), lambda i: (i, 0)),
            pl.BlockSpec((tm, D_MODEL), lambda i: (i, 1)),
            pl.BlockSpec((tm, D_MODEL), lambda i: (i, 2)),
            pl.BlockSpec((3, D_MODEL), const),
            pl.BlockSpec((tm, D_MODEL), row),
            pl.BlockSpec((tm, D_MODEL), row),
            pl.BlockSpec((tm, D_MODEL), row),
            pl.BlockSpec((tm, D_MODEL), row),
            pl.BlockSpec((D_MODEL, D_MODEL), const),
            pl.BlockSpec((1, D_MODEL), const),
            pl.BlockSpec((1, D_MODEL), const),
            pl.BlockSpec((D_MODEL, LANES), const),
            pl.BlockSpec((D_MODEL, LANES), const),
            pl.BlockSpec((1, LANES), const),
        ],
        out_specs=(pl.BlockSpec((tm * ROW_TILES, LANES), row), pl.BlockSpec((tm, LANES), row)),
        compiler_params=_params("parallel"),
        name="merge_ln_router",
    )(h, h, h, b_gate, y_ssd, y_conf, y_att, x2d, w_out.astype(BF16), ln_g[None, :], ln_b[None, :],
      w_router_hi, w_router_lo, b_router)


GATHER_UNROLL = 8


def _on_slot(slot, fn, n_slots=2):
    for s in range(n_slots):
        @pl.when(slot == s)
        def _(s=s):
            fn(s)


def _gather_pipeline(i, n_steps, idx_hbm, src_hbm, idx_smem, buf, isem, gsem):
    n_rows = buf[0].shape[0] // ROW_TILES
    slot = i % 2

    def idx_copy(step, s):
        return pltpu.make_async_copy(idx_hbm.at[step], idx_smem[s], isem.at[s])

    def issue_rows(s):
        def body(it, carry):
            for u in range(GATHER_UNROLL):
                r = it * GATHER_UNROLL + u
                src_row = pl.multiple_of(idx_smem[s][r] * ROW_TILES, ROW_TILES)
                dst_row = pl.multiple_of(r * ROW_TILES, ROW_TILES)
                pltpu.make_async_copy(src_hbm.at[pl.ds(src_row, ROW_TILES)],
                                      buf[s].at[pl.ds(dst_row, ROW_TILES)], gsem.at[s]).start(priority=u % 2)
            return carry

        lax.fori_loop(0, n_rows // GATHER_UNROLL, body, 0)

    @pl.when(i == 0)
    def _():
        idx_copy(0, 0).start()
        idx_copy(0, 0).wait()
        issue_rows(0)

        @pl.when(n_steps > 1)
        def _():
            idx_copy(1, 1).start()

    @pl.when(i + 1 < n_steps)
    def _():
        def prefetch(s):
            idx_copy(i + 1, 1 - s).wait()
            issue_rows(1 - s)

            @pl.when(i + 2 < n_steps)
            def _():
                idx_copy(i + 2, s).start()

        _on_slot(slot, prefetch)


def _gather_wait(s, src_hbm, buf, gsem):
    pltpu.make_async_copy(src_hbm.at[pl.ds(0, buf[s].shape[0])], buf[s], gsem.at[s]).wait()


DISPATCH_SLOTS = 3


def _dispatch_kernel(pad_start_ref, pad_len_ref, idx_hbm, x1_hbm, rows_hbm, idx0, idx1, idx2, st0, st1, st2,
                     zero_ref, isem, xsem, dsem, zsem):
    i = pl.program_id(0)
    n_steps = pl.num_programs(0)
    idx = (idx0, idx1, idx2)
    stage = (st0, st1, st2)
    n_idx = idx0.shape[0]
    tm = n_idx // MOE_TOP_K

    def loads(step, s):
        first = pl.multiple_of(step * (tm * ROW_TILES), tm * ROW_TILES)
        return (pltpu.make_async_copy(idx_hbm.at[step], idx[s], isem.at[s]),
                pltpu.make_async_copy(x1_hbm.at[pl.ds(first, tm * ROW_TILES)], stage[s], xsem.at[s]))

    def start_loads(step, s):
        for c in loads(step, s):
            c.start()

    def wait_rows(s):
        pltpu.make_async_copy(x1_hbm.at[pl.ds(0, n_idx * ROW_TILES)], rows_hbm.at[pl.ds(0, n_idx * ROW_TILES)],
                              dsem.at[s]).wait()

    def zero_copy(row, n_rows):
        dst_row = pl.multiple_of(row * ROW_TILES, ROW_TILES)
        if n_rows is None:
            return pltpu.make_async_copy(zero_ref, rows_hbm.at[pl.ds(dst_row, ROW_TILES)], zsem.at[0])
        return pltpu.make_async_copy(x1_hbm.at[pl.ds(0, n_rows * ROW_TILES)],
                                     rows_hbm.at[pl.ds(dst_row, n_rows * ROW_TILES)], zsem.at[0])

    @pl.when(i == 0)
    def _():
        start_loads(0, 0)

        @pl.when(n_steps > 1)
        def _():
            start_loads(1, 1)

        zero_ref[...] = jnp.zeros(zero_ref.shape, F32)

        def zero_expert(e, carry):
            def zero_row(k, c):
                zero_copy(pad_start_ref[e] + k, None).start()
                return c

            return lax.fori_loop(0, pad_len_ref[e], zero_row, carry)

        lax.fori_loop(0, pad_len_ref.shape[0], zero_expert, 0)

    @pl.when(i == n_steps - 1)
    def _():
        def wait_expert(e, carry):
            @pl.when(pad_len_ref[e] > 0)
            def _():
                zero_copy(pad_start_ref[e], pad_len_ref[e]).wait()

            return carry

        lax.fori_loop(0, pad_len_ref.shape[0], wait_expert, 0)

    def step(s):
        freed = (s + DISPATCH_SLOTS - 1) % DISPATCH_SLOTS
        for c in loads(i, s):
            c.wait()

        def body(it, carry):
            for u in range(GATHER_UNROLL):
                r = it * GATHER_UNROLL + u
                src_row = pl.multiple_of((r % tm) * ROW_TILES, ROW_TILES)
                dst_row = pl.multiple_of(idx[s][r] * ROW_TILES, ROW_TILES)
                pltpu.make_async_copy(stage[s].at[pl.ds(src_row, ROW_TILES)],
                                      rows_hbm.at[pl.ds(dst_row, ROW_TILES)], dsem.at[s]).start(priority=u % 2)
            return carry

        lax.fori_loop(0, n_idx // GATHER_UNROLL, body, 0)

        @pl.when(i > 0)
        def _():
            wait_rows(freed)

        @pl.when(i + 2 < n_steps)
        def _():
            start_loads(i + 2, freed)

        @pl.when(i == n_steps - 1)
        def _():
            wait_rows(s)

    _on_slot(i % DISPATCH_SLOTS, step, DISPATCH_SLOTS)


def _moe_dispatch(x1, dest_blocks, pad_start, pad_len, n_rows):
    n_steps, n_idx = dest_blocks.shape
    any_spec = pl.BlockSpec(memory_space=pl.ANY)
    grid_spec = pltpu.PrefetchScalarGridSpec(
        num_scalar_prefetch=2,
        grid=(n_steps,),
        in_specs=[any_spec, any_spec],
        out_specs=any_spec,
        scratch_shapes=(
            [pltpu.SMEM((n_idx,), jnp.int32)] * DISPATCH_SLOTS
            + [pltpu.VMEM((n_idx // MOE_TOP_K * ROW_TILES, LANES), F32)] * DISPATCH_SLOTS
            + [pltpu.VMEM((ROW_TILES, LANES), F32)]
            + [pltpu.SemaphoreType.DMA((DISPATCH_SLOTS,))] * 3
            + [pltpu.SemaphoreType.DMA((1,))]
        ),
    )
    return pl.pallas_call(
        _dispatch_kernel,
        out_shape=jax.ShapeDtypeStruct((n_rows * ROW_TILES, LANES), F32),
        grid_spec=grid_spec,
        compiler_params=_params("arbitrary"),
        name="moe_dispatch",
    )(pad_start, pad_len, dest_blocks, x1)


def _moe_kernel(be_ref, nu_ref, x_ref, wg_ref, wu_ref, wd_ref, y_ref, wgb_ref, wub_ref, wdb_ref):
    i = pl.program_id(0)

    @pl.when(i < nu_ref[0])
    def _():
        @pl.when((i == 0) | (be_ref[i] != be_ref[jnp.maximum(i - 1, 0)]))
        def _():
            wgb_ref[...] = wg_ref[...].astype(BF16)
            wub_ref[...] = wu_ref[...].astype(BF16)
            wdb_ref[...] = wd_ref[...].astype(BF16)

        xb = _load_row_tiles(x_ref, 0, MOE_BLOCK).astype(BF16)
        gate = jnp.dot(xb, wgb_ref[...], preferred_element_type=F32)
        up = jnp.dot(xb, wub_ref[...], preferred_element_type=F32)
        hmid = (_silu(gate) * up).astype(BF16)
        _store_row_tiles(y_ref, jnp.dot(hmid, wdb_ref[...], preferred_element_type=F32))

    @pl.when(i >= nu_ref[0])
    def _():
        y_ref[...] = jnp.zeros(y_ref.shape, F32)


def _moe_experts(x_rows, block_e, n_used, w_gate, w_up, w_down):
    n_blocks = block_e.shape[0]
    grid_spec = pltpu.PrefetchScalarGridSpec(
        num_scalar_prefetch=2,
        grid=(n_blocks,),
        in_specs=[
            pl.BlockSpec((MOE_BLOCK * ROW_TILES, LANES), lambda i, be, nu: (jnp.minimum(i, nu[0] - 1), 0)),
            pl.BlockSpec((None, D_MODEL, MOE_FF), lambda i, be, nu: (be[i], 0, 0)),
            pl.BlockSpec((None, D_MODEL, MOE_FF), lambda i, be, nu: (be[i], 0, 0)),
            pl.BlockSpec((None, MOE_FF, D_MODEL), lambda i, be, nu: (be[i], 0, 0)),
        ],
        out_specs=pl.BlockSpec((MOE_BLOCK * ROW_TILES, LANES), lambda i, be, nu: (i, 0)),
        scratch_shapes=[
            pltpu.VMEM((D_MODEL, MOE_FF), BF16),
            pltpu.VMEM((D_MODEL, MOE_FF), BF16),
            pltpu.VMEM((MOE_FF, D_MODEL), BF16),
        ],
    )
    return pl.pallas_call(
        _moe_kernel,
        out_shape=jax.ShapeDtypeStruct((n_blocks * MOE_BLOCK * ROW_TILES, LANES), F32),
        grid_spec=grid_spec,
        compiler_params=_params("arbitrary"),
        name="moe_experts",
    )(block_e, n_used, x_rows, w_gate, w_up, w_down)


COMB_TM = 256


def _combine_kernel(idx_hbm, y_hbm, ew_ref, x1_ref, p_ref, lg_ref, lb_ref, wpg_ref, wpp_ref, out_ref,
                    idx0, idx1, ybuf0, ybuf1, ffn_ref, isem, gsem):
    i = pl.program_id(0)
    tm = out_ref.shape[0]
    ybuf = (ybuf0, ybuf1)
    _gather_pipeline(i, pl.num_programs(0), idx_hbm, y_hbm, (idx0, idx1), ybuf, isem, gsem)

    def weighted_sum(s):
        _gather_wait(s, y_hbm, ybuf, gsem)
        ew = ew_ref[...]
        ffn_ref[...] = (_load_row_tiles(ybuf[s], 0, tm) * ew[:, 0:1]
                        + _load_row_tiles(ybuf[s], tm, tm) * ew[:, 1:2])

    _on_slot(i % 2, weighted_sum)
    x1 = _load_row_tiles(x1_ref, 0, tm)
    x2 = _layer_norm(DEEPNORM_ALPHA * x1 + ffn_ref[...], lg_ref[...], lb_ref[...])
    gate = _sigmoid(jnp.dot(x2.astype(BF16), wpg_ref[...], preferred_element_type=F32))
    emb = jnp.dot(p_ref[...].astype(BF16), wpp_ref[...], preferred_element_type=F32)
    out_ref[...] = x2 + gate * emb


def _combine(y_rows, dest_blocks, e_w, x1, p2d, ln_g, ln_b, ple_w_gate, ple_w_proj):
    t = x1.shape[0] // ROW_TILES
    tm = dest_blocks.shape[1] // MOE_TOP_K
    const = lambda i: (0, 0)
    row = lambda i: (i, 0)
    return pl.pallas_call(
        _combine_kernel,
        out_shape=jax.ShapeDtypeStruct((t, D_MODEL), F32),
        grid=(t // tm,),
        in_specs=[
            pl.BlockSpec(memory_space=pl.ANY),
            pl.BlockSpec(memory_space=pl.ANY),
            pl.BlockSpec((tm, MOE_TOP_K), row),
            pl.BlockSpec((tm * ROW_TILES, LANES), row),
            pl.BlockSpec((tm, PLE_DIM), row),
            pl.BlockSpec((1, D_MODEL), const),
            pl.BlockSpec((1, D_MODEL), const),
            pl.BlockSpec((D_MODEL, D_MODEL), const),
            pl.BlockSpec((PLE_DIM, D_MODEL), const),
        ],
        out_specs=pl.BlockSpec((tm, D_MODEL), row),
        scratch_shapes=[
            pltpu.SMEM((MOE_TOP_K * tm,), jnp.int32),
            pltpu.SMEM((MOE_TOP_K * tm,), jnp.int32),
            pltpu.VMEM((MOE_TOP_K * tm * ROW_TILES, LANES), F32),
            pltpu.VMEM((MOE_TOP_K * tm * ROW_TILES, LANES), F32),
            pltpu.VMEM((tm, D_MODEL), F32),
            pltpu.SemaphoreType.DMA((2,)),
            pltpu.SemaphoreType.DMA((2,)),
        ],
        compiler_params=_params("arbitrary"),
        name="moe_combine",
    )(dest_blocks, y_rows, e_w, x1, p2d, ln_g[None, :], ln_b[None, :], ple_w_gate.astype(BF16),
      ple_w_proj.astype(BF16))


ROUTE_CHUNK = 256


def _route(logits):
    t = logits.shape[0]
    tk = t * MOE_TOP_K
    g_logits = logits[:, :MOE_GROUPS]
    e_logits = logits[:, MOE_GROUPS:MOE_GROUPS + MOE_EXPERTS].reshape(t, MOE_GROUPS, MOE_EXPERTS_PER_GROUP)
    g_prob = jax.nn.softmax(g_logits, axis=-1)
    g_idx = jnp.argmax(g_logits, axis=-1).astype(jnp.int32)
    g_sel = g_idx[:, None] == jnp.arange(MOE_GROUPS, dtype=jnp.int32)[None, :]
    g_w = jnp.sum(jnp.where(g_sel, g_prob, 0.0), axis=1, keepdims=True)
    e_in = jnp.sum(jnp.where(g_sel[:, :, None], e_logits, 0.0), axis=1)
    lane = jnp.arange(MOE_EXPERTS_PER_GROUP, dtype=jnp.int32)[None, :]
    i1 = jnp.argmax(e_in, axis=-1).astype(jnp.int32)
    rest = jnp.where(lane == i1[:, None], -jnp.inf, e_in)
    i2 = jnp.argmax(rest, axis=-1).astype(jnp.int32)
    top_i = jnp.stack([i1, i2], axis=-1)
    top_v = jnp.stack([jnp.max(e_in, axis=-1), jnp.max(rest, axis=-1)], axis=-1)
    e_w = jax.nn.softmax(top_v, axis=-1) * g_w
    e_id = g_idx[:, None] * MOE_EXPERTS_PER_GROUP + top_i
    flat_e = e_id.reshape(tk)

    chunk = min(ROUTE_CHUNK, tk)
    onehot = flat_e[:, None] == jnp.arange(MOE_EXPERTS, dtype=jnp.int32)[None, :]
    oh = onehot.reshape(tk // chunk, chunk, MOE_EXPERTS)
    tril = jnp.tril(jnp.ones((chunk, chunk), BF16))
    within = jnp.einsum("ij,cje->cie", tril, oh.astype(BF16), preferred_element_type=F32).astype(jnp.int32)
    chunk_counts = within[:, -1, :]
    chunk_ends = jnp.cumsum(chunk_counts, axis=0)
    csum = within + (chunk_ends - chunk_counts)[:, None, :]
    rank = jnp.sum(jnp.where(oh, csum, 0), axis=-1).reshape(tk) - 1
    counts = chunk_ends[-1]
    padded = ((counts + MOE_BLOCK - 1) // MOE_BLOCK) * MOE_BLOCK
    pends = jnp.cumsum(padded)
    pstarts = pends - padded
    dest = pstarts[flat_e] + rank

    n_rows = tk + MOE_EXPERTS * MOE_BLOCK
    n_blocks = n_rows // MOE_BLOCK
    block_start = jnp.arange(n_blocks, dtype=jnp.int32) * MOE_BLOCK
    block_e = jnp.sum((pends[None, :] <= block_start[:, None]).astype(jnp.int32), axis=1)
    block_e = jnp.minimum(block_e, MOE_EXPERTS - 1).astype(jnp.int32)
    n_used = (pends[-1] // MOE_BLOCK).astype(jnp.int32).reshape(1)
    pad = (jnp.concatenate([pstarts + counts, pends[-1:]]),
           jnp.concatenate([padded - counts, n_rows - pends[-1:]]))
    return e_w, dest.reshape(t, MOE_TOP_K), block_e, n_used, pad, n_rows


def _layer(x2d, p2d, cos_t, sin_t, bsz, seq, w_in, b_gate, ssd_conv_w, ssd_conv_b, ssd_dt_bias, ssd_a_log,
           ssd_d, ssd_norm_w, ssd_w_out, conf_dw_w, conf_dw_b, conf_ln_g, conf_ln_b, conf_w_out, attn_sinks,
           attn_w_out, w_out, ln1_g, ln1_b, moe_w_group, moe_b_group, moe_w_expert, moe_b_expert, moe_w_gate,
           moe_w_up, moe_w_down, ln2_g, ln2_b, ple_w_gate, ple_w_proj):
    t = bsz * seq
    w_main = jnp.concatenate([w_in[:, :DT_COL_ORIG], w_in[:, DT_COL_ORIG + SSD_HEADS:]], axis=1)
    w_main = jnp.concatenate([w_main[:, :COL_Q], w_main[:, COL_Q:COL_V][:, _pair_interleave(COL_V - COL_Q)],
                              w_main[:, COL_V:]], axis=1).astype(BF16)
    w_dt = jnp.pad(w_in[:, DT_COL_ORIG:DT_COL_ORIG + SSD_HEADS], ((0, 0), (0, LANES - SSD_HEADS))).astype(BF16)
    h, dt_raw = _in_projection(x2d, w_main, w_dt)

    y_ssd = _ssd_mixer(h, dt_raw, bsz, seq, ssd_conv_w, ssd_conv_b, ssd_dt_bias, ssd_a_log, ssd_d, ssd_norm_w,
                       ssd_w_out)
    y_conf = _conformer(h, bsz, seq, conf_dw_w, conf_dw_b, conf_ln_g, conf_ln_b, conf_w_out)
    y_att = _attention(h, cos_t, sin_t, bsz, seq, attn_sinks, attn_w_out)

    n_router = MOE_GROUPS + MOE_EXPERTS
    w_router = jnp.pad(jnp.concatenate([moe_w_group, moe_w_expert], axis=1), ((0, 0), (0, LANES - n_router)))
    b_router = jnp.pad(jnp.concatenate([moe_b_group, moe_b_expert]), (0, LANES - n_router))[None, :]
    x1, logits = _merge(h, b_gate, y_ssd, y_conf, y_att, x2d, w_out, ln1_g, ln1_b, w_router, b_router)

    e_w, dest, block_e, n_used, (pad_start, pad_len), n_rows = _route(logits)
    tm = min(COMB_TM, t)
    dest_blocks = dest.reshape(t // tm, tm, MOE_TOP_K).transpose(0, 2, 1).reshape(t // tm, MOE_TOP_K * tm)
    x_rows = _moe_dispatch(x1, dest_blocks, pad_start, pad_len, n_rows)
    y_rows = _moe_experts(x_rows, block_e, n_used, moe_w_gate, moe_w_up, moe_w_down)
    return _combine(y_rows, dest_blocks, e_w, x1, p2d, ln2_g, ln2_b, ple_w_gate, ple_w_proj)


def kernel(x, p, positions, w_in, b_gate, ssd_conv_w, ssd_conv_b, ssd_dt_bias, ssd_a_log, ssd_d, ssd_norm_w, ssd_w_out, conf_dw_w, conf_dw_b, conf_ln_g, conf_ln_b, conf_w_out, attn_sinks, attn_w_out, w_out, ln1_g, ln1_b, moe_w_group, moe_b_group, moe_w_expert, moe_b_expert, moe_w_gate, moe_w_up, moe_w_down, ln2_g, ln2_b, ple_w_gate, ple_w_proj):
    bsz, seq, d = x.shape
    t = bsz * seq
    cos_t, sin_t = _rope_tables(positions)
    x2d = x.reshape(t, d)
    per_layer = (w_in, b_gate, ssd_conv_w, ssd_conv_b, ssd_dt_bias, ssd_a_log, ssd_d, ssd_norm_w, ssd_w_out,
                 conf_dw_w, conf_dw_b, conf_ln_g, conf_ln_b, conf_w_out, attn_sinks, attn_w_out, w_out, ln1_g,
                 ln1_b, moe_w_group, moe_b_group, moe_w_expert, moe_b_expert, moe_w_gate, moe_w_up, moe_w_down,
                 ln2_g, ln2_b, ple_w_gate, ple_w_proj)
    for layer in range(w_in.shape[0]):
        x2d = _layer(x2d, p[layer].reshape(t, -1), cos_t, sin_t, bsz, seq, *(w[layer] for w in per_layer))
    return x2d.reshape(bsz, seq, d)
```

```python
import functools

import jax
import jax.numpy as jnp
from jax import lax
from jax.experimental import pallas as pl
from jax.experimental.pallas import tpu as pltpu

F32 = jnp.float32
BF16 = jnp.bfloat16

D_MODEL = 1024
N_LAYERS = 2
PLE_DIM = 256
SSD_HEADS = 16
SSD_HEAD_DIM = 64
SSD_INNER = SSD_HEADS * SSD_HEAD_DIM
SSD_GROUPS = 4
SSD_STATE = 128
SSD_CONV = 4
SSD_CHUNK = 128
SSD_XBC = SSD_INNER + 2 * SSD_GROUPS * SSD_STATE
CONF_CH = 1024
CONF_KERNEL = 31
ATT_Q_HEADS = 16
ATT_KV_HEADS = 4
ATT_HEAD_DIM = 64
ATT_BLOCK = 128
ROPE_THETA = 10000.0
MOE_GROUPS = 4
MOE_EXPERTS_PER_GROUP = 8
MOE_EXPERTS = MOE_GROUPS * MOE_EXPERTS_PER_GROUP
MOE_TOP_K = 2
MOE_FF = 512
MOE_BLOCK = 256
DEEPNORM_ALPHA = (2 * N_LAYERS) ** 0.25
LN_EPS = 1e-5

LANES = 128
SUBLANES = 8
VMEM_LIMIT_BYTES = 56 * 1024 * 1024

COL_GATES = 0
COL_Z = 3 * D_MODEL
COL_XBC = COL_Z + SSD_INNER
COL_U = COL_XBC + SSD_XBC
COL_Q = COL_U + 2 * CONF_CH
COL_K = COL_Q + ATT_Q_HEADS * ATT_HEAD_DIM
COL_V = COL_K + ATT_KV_HEADS * ATT_HEAD_DIM
H_WIDTH = COL_V + ATT_KV_HEADS * ATT_HEAD_DIM
DT_COL_ORIG = 3 * D_MODEL + SSD_INNER + SSD_XBC


def _params(*semantics):
    return pltpu.CompilerParams(dimension_semantics=semantics, vmem_limit_bytes=VMEM_LIMIT_BYTES)


def _sigmoid(x):
    return 0.5 * (jnp.tanh(0.5 * x) + 1.0)


def _silu(x):
    return x * jax.nn.sigmoid(x)


def _layer_norm(x, g, b):
    mu = jnp.mean(x, axis=-1, keepdims=True)
    xc = x - mu
    var = jnp.mean(xc * xc, axis=-1, keepdims=True)
    return xc * lax.rsqrt(var + LN_EPS) * g + b


ROW_TILES = D_MODEL // LANES


def _store_row_tiles(ref, x):
    rows = x.shape[0]
    for k in range(ROW_TILES):
        ref[pl.ds(k, rows, stride=ROW_TILES), :] = x[:, k * LANES:(k + 1) * LANES]


def _load_row_tiles(ref, first_row, rows):
    return jnp.concatenate(
        [ref[pl.ds(first_row * ROW_TILES + k, rows, stride=ROW_TILES), :] for k in range(ROW_TILES)], axis=1)


IN_TM = 512
IN_TN = 512


def _inproj_kernel(x_ref, w_ref, wdt_ref, h_ref, dt_ref):
    xb = x_ref[...].astype(BF16)
    dt_ref[...] = jnp.dot(xb, wdt_ref[...], preferred_element_type=F32)
    for j in range(H_WIDTH // IN_TN):
        cols = slice(j * IN_TN, (j + 1) * IN_TN)
        h_ref[:, cols] = jnp.dot(xb, w_ref[:, cols], preferred_element_type=F32).astype(BF16)


def _in_projection(x2d, w_main, w_dt):
    t = x2d.shape[0]
    tm = min(IN_TM, t)
    resident = pl.Buffered(1)
    return pl.pallas_call(
        _inproj_kernel,
        out_shape=(jax.ShapeDtypeStruct((t, H_WIDTH), BF16), jax.ShapeDtypeStruct((t, LANES), F32)),
        grid=(t // tm,),
        in_specs=[
            pl.BlockSpec((tm, D_MODEL), lambda i: (i, 0)),
            pl.BlockSpec((D_MODEL, H_WIDTH), lambda i: (0, 0), pipeline_mode=resident),
            pl.BlockSpec((D_MODEL, LANES), lambda i: (0, 0), pipeline_mode=resident),
        ],
        out_specs=(
            pl.BlockSpec((tm, H_WIDTH), lambda i: (i, 0)),
            pl.BlockSpec((tm, LANES), lambda i: (i, 0)),
        ),
        compiler_params=_params("parallel"),
        name="in_projection",
    )(x2d, w_main, w_dt)


def _rope_kernel(pos_ref, inv_ref, sign_ref, cos_ref, sin_ref):
    ang = pos_ref[...].astype(F32) * inv_ref[...]
    cos_ref[...] = jnp.cos(ang)
    sin_ref[...] = jnp.sin(ang) * sign_ref[...]


def _rope_tables(positions):
    t = positions.size
    tm = min(1024, t)
    half = ATT_HEAD_DIM // 2
    inv_freq = ROPE_THETA ** (-jnp.arange(half, dtype=F32) / half)
    inv = jnp.tile(inv_freq, LANES // half)[None, :]
    sign = jnp.concatenate([-jnp.ones((LANES // 2,), F32), jnp.ones((LANES // 2,), F32)])[None, :]
    return pl.pallas_call(
        _rope_kernel,
        out_shape=(jax.ShapeDtypeStruct((t, LANES), F32), jax.ShapeDtypeStruct((t, LANES), F32)),
        grid=(t // tm,),
        in_specs=[
            pl.BlockSpec((tm, 1), lambda i: (i, 0)),
            pl.BlockSpec((1, LANES), lambda i: (0, 0)),
            pl.BlockSpec((1, LANES), lambda i: (0, 0)),
        ],
        out_specs=(pl.BlockSpec((tm, LANES), lambda i: (i, 0)), pl.BlockSpec((tm, LANES), lambda i: (i, 0))),
        compiler_params=_params("parallel"),
        name="rope_tables",
    )(positions.reshape(t, 1), inv, sign)


SSD_PAIRS = SSD_HEADS // 2
SSD_STEP_CHUNKS = 4


def _ssd_kernel(xbc_ref, z_ref, dt_ref, shift_ref, cw_ref, cb_ref, dtb_ref, alog_ref, dsk_ref, nw_ref, wp_ref,
                y_ref, state_ref, ext_ref, ybuf_ref):
    L = SSD_CHUNK
    n_sub = xbc_ref.shape[0] // L
    c = pl.program_id(1)

    @pl.when(c == 0)
    def _():
        state_ref[...] = jnp.zeros(state_ref.shape, F32)
        ext_ref[0:L, :] = jnp.zeros((L, SSD_XBC), BF16)

    @pl.when(c > 0)
    def _():
        ext_ref[0:L, :] = ext_ref[n_sub * L:(n_sub + 1) * L, :]

    ext_ref[L:(n_sub + 1) * L, :] = xbc_ref[...]

    row = lax.broadcasted_iota(jnp.int32, (L, L), 0)
    col = lax.broadcasted_iota(jnp.int32, (L, L), 1)
    causal = row >= col
    tril = causal.astype(F32)
    lo = col < SSD_HEAD_DIM
    a = -jnp.exp(alog_ref[...])

    for sub in range(n_sub):
        rows = slice(sub * L, (sub + 1) * L)

        taps = jnp.dot(shift_ref[...], ext_ref[sub * L:(sub + 2) * L, :], preferred_element_type=F32)
        acc = cb_ref[...] + cw_ref[0:1, :] * taps[0:L]
        for k in range(1, SSD_CONV):
            acc = acc + cw_ref[k:k + 1, :] * taps[k * L:(k + 1) * L]
        act = _silu(acc)
        xs = act[:, :SSD_INNER]
        bm = act[:, SSD_INNER:SSD_INNER + SSD_GROUPS * SSD_STATE]
        cm = act[:, SSD_INNER + SSD_GROUPS * SSD_STATE:]

        x_dt = dt_ref[rows, :] + dtb_ref[...]
        dt = jnp.maximum(x_dt, 0.0) + jnp.log1p(jnp.exp(-jnp.abs(x_dt)))
        cs = jnp.dot(tril, dt * a, preferred_element_type=F32, precision=lax.Precision.HIGHEST)
        cs_t = cs.T

        for g in range(SSD_GROUPS):
            bm_g = bm[:, g * SSD_STATE:(g + 1) * SSD_STATE].astype(BF16)
            cm_g = cm[:, g * SSD_STATE:(g + 1) * SSD_STATE].astype(BF16)
            cb_g = lax.dot_general(cm_g, bm_g, (((1,), (1,)), ((), ())), preferred_element_type=F32)
            pairs_per_group = SSD_PAIRS // SSD_GROUPS
            for jj in range(pairs_per_group):
                j = g * pairs_per_group + jj
                h0, h1 = 2 * j, 2 * j + 1
                sl = slice(j * LANES, (j + 1) * LANES)
                col0 = jnp.broadcast_to(cs[:, h0:h0 + 1], (L, L))
                col1 = jnp.broadcast_to(cs[:, h1:h1 + 1], (L, L))
                dec0 = jnp.where(causal, jnp.exp(col0 - cs_t[h0:h0 + 1, :]), 0.0)
                dec1 = jnp.where(causal, jnp.exp(col1 - cs_t[h1:h1 + 1, :]), 0.0)
                m = jnp.concatenate([cb_g * dec0, cb_g * dec1], axis=1).astype(BF16)
                dt_p = jnp.where(lo, jnp.broadcast_to(dt[:, h0:h0 + 1], (L, L)),
                                 jnp.broadcast_to(dt[:, h1:h1 + 1], (L, L)))
                xs_p = xs[:, sl]
                xdt = xs_p * dt_p
                x2 = jnp.concatenate([jnp.where(lo, xdt, 0.0), jnp.where(lo, 0.0, xdt)], axis=0).astype(BF16)
                y_diag = jnp.dot(m, x2, preferred_element_type=F32)
                cs_p = jnp.where(lo, col0, col1)
                st = state_ref[j]
                y_off = jnp.exp(cs_p) * jnp.dot(cm_g, st.astype(BF16), preferred_element_type=F32)
                last = cs_p[L - 1:L, :]
                xdt_end = (xdt * jnp.exp(last - cs_p)).astype(BF16)
                new_st = lax.dot_general(bm_g, xdt_end, (((0,), (0,)), ((), ())), preferred_element_type=F32)
                state_ref[j] = st * jnp.exp(last) + new_st
                ybuf_ref[rows, sl] = y_diag + y_off + xs_p * dsk_ref[:, sl]

    y = ybuf_ref[...] * _silu(z_ref[...].astype(F32))
    gw = SSD_INNER // SSD_GROUPS
    parts = []
    for g in range(SSD_GROUPS):
        yg = y[:, g * gw:(g + 1) * gw]
        parts.append(yg * lax.rsqrt(jnp.mean(yg * yg, axis=-1, keepdims=True) + LN_EPS))
    yn = jnp.concatenate(parts, axis=1) * nw_ref[...]
    y_ref[...] = jnp.dot(yn.astype(BF16), wp_ref[...], preferred_element_type=F32).astype(BF16)


def _ssd_mixer(h, dt_raw, bsz, seq, conv_w, conv_b, dt_bias, a_log, d_skip, norm_w, w_proj):
    t = bsz * seq
    L = SSD_CHUNK
    ts = min(SSD_STEP_CHUNKS * L, seq)
    nc = seq // ts
    pad = LANES - SSD_HEADS
    dtb = jnp.pad(dt_bias, (0, pad))[None, :]
    alog = jnp.pad(a_log, (0, pad))[None, :]
    dsk = jnp.repeat(d_skip, SSD_HEAD_DIM)[None, :]
    r = jnp.arange(SSD_CONV * L, dtype=jnp.int32)
    shift = (jnp.arange(2 * L, dtype=jnp.int32)[None, :]
             == (L + r % L - (SSD_CONV - 1) + r // L)[:, None]).astype(BF16)
    const = lambda b, c: (0, 0)
    return pl.pallas_call(
        _ssd_kernel,
        out_shape=jax.ShapeDtypeStruct((t, D_MODEL), BF16),
        grid=(bsz, nc),
        in_specs=[
            pl.BlockSpec((ts, SSD_XBC), lambda b, c: (b * nc + c, COL_XBC // SSD_XBC)),
            pl.BlockSpec((ts, SSD_INNER), lambda b, c: (b * nc + c, COL_Z // SSD_INNER)),
            pl.BlockSpec((ts, LANES), lambda b, c: (b * nc + c, 0)),
            pl.BlockSpec((SSD_CONV * L, 2 * L), const),
            pl.BlockSpec((SSD_CONV, SSD_XBC), const),
            pl.BlockSpec((1, SSD_XBC), const),
            pl.BlockSpec((1, LANES), const),
            pl.BlockSpec((1, LANES), const),
            pl.BlockSpec((1, SSD_INNER), const),
            pl.BlockSpec((1, SSD_INNER), const),
            pl.BlockSpec((SSD_INNER, D_MODEL), const),
        ],
        out_specs=pl.BlockSpec((ts, D_MODEL), lambda b, c: (b * nc + c, 0)),
        scratch_shapes=[
            pltpu.VMEM((SSD_PAIRS, SSD_STATE, LANES), F32),
            pltpu.VMEM((ts + L, SSD_XBC), BF16),
            pltpu.VMEM((ts, SSD_INNER), F32),
        ],
        compiler_params=_params("parallel", "arbitrary"),
        name="ssd_mixer",
    )(h, h, dt_raw, shift, conv_w, conv_b[None, :], dtb, alog, dsk, norm_w[None, :], w_proj.astype(BF16))


CONF_TS = 256
CONF_HALO = 32
CONF_ROW_BLK = 64
CONF_COL_BLK = 256


def _conf_kernel(u_ref, shift_ref, dw_ref, db_ref, g_ref, b_ref, wp_ref, y_ref, ext_ref, conv_ref):
    ts = u_ref.shape[0]
    s = pl.program_id(1)
    span = ts + SUBLANES
    lead = CONF_HALO - SUBLANES

    @pl.when(s == 0)
    def _():
        ext_ref[:, 0:CONF_HALO, :] = jnp.zeros((SUBLANES, CONF_HALO, CONF_CH), F32)

    @pl.when(s > 0)
    def _():
        ext_ref[:, 0:CONF_HALO, :] = ext_ref[:, ts:ts + CONF_HALO, :]

    u = u_ref[...].astype(F32)
    glu = (u[:, :CONF_CH] * _sigmoid(u[:, CONF_CH:])).astype(BF16)
    ext_ref[0, CONF_HALO:CONF_HALO + ts, :] = glu.astype(F32)
    shifted = jnp.dot(shift_ref[...], glu, preferred_element_type=F32)
    for b in range(1, SUBLANES):
        blk = shifted[(b - 1) * span:b * span]
        ext_ref[b, lead:CONF_HALO, :] = ext_ref[b, lead:CONF_HALO, :] + blk[0:SUBLANES]
        ext_ref[b, CONF_HALO:CONF_HALO + ts, :] = blk[SUBLANES:span]

    for r0 in range(0, ts, CONF_ROW_BLK):
        for c0 in range(0, CONF_CH, CONF_COL_BLK):
            cols = slice(c0, c0 + CONF_COL_BLK)
            n_acc = CONF_ROW_BLK // SUBLANES
            acc = [jnp.broadcast_to(db_ref[:, cols], (SUBLANES, CONF_COL_BLK))] * n_acc
            for k in range(CONF_KERNEL):
                shift = CONF_HALO - (CONF_KERNEL - 1) + k
                off = shift - shift % SUBLANES + r0
                w = dw_ref[k, :, cols]
                acc = [a + w * ext_ref[shift % SUBLANES, off + SUBLANES * n:off + SUBLANES * (n + 1), cols]
                       for n, a in enumerate(acc)]
            conv_ref[r0:r0 + CONF_ROW_BLK, cols] = jnp.concatenate(acc, axis=0)

    hn = _silu(_layer_norm(conv_ref[...], g_ref[...], b_ref[...]))
    y_ref[...] = jnp.dot(hn.astype(BF16), wp_ref[...], preferred_element_type=F32).astype(BF16)


def _conformer(h, bsz, seq, dw_w, dw_b, ln_g, ln_b, w_proj):
    t = bsz * seq
    ts = min(CONF_TS, seq)
    nt = seq // ts
    const = lambda b, s: (0, 0)
    dw = jnp.broadcast_to(dw_w[:, None, :], (CONF_KERNEL, SUBLANES, CONF_CH))
    span = ts + SUBLANES
    r = jnp.arange((SUBLANES - 1) * span, dtype=jnp.int32)
    shift = (jnp.arange(ts, dtype=jnp.int32)[None, :] == (r % span - SUBLANES + 1 + r // span)[:, None]).astype(BF16)
    return pl.pallas_call(
        _conf_kernel,
        out_shape=jax.ShapeDtypeStruct((t, D_MODEL), BF16),
        grid=(bsz, nt),
        in_specs=[
            pl.BlockSpec((ts, 2 * CONF_CH), lambda b, s: (b * nt + s, COL_U // (2 * CONF_CH))),
            pl.BlockSpec(((SUBLANES - 1) * span, ts), const),
            pl.BlockSpec((CONF_KERNEL, SUBLANES, CONF_CH), lambda b, s: (0, 0, 0)),
            pl.BlockSpec((1, CONF_CH), const),
            pl.BlockSpec((1, CONF_CH), const),
            pl.BlockSpec((1, CONF_CH), const),
            pl.BlockSpec((CONF_CH, D_MODEL), const),
        ],
        out_specs=pl.BlockSpec((ts, D_MODEL), lambda b, s: (b * nt + s, 0)),
        scratch_shapes=[
            pltpu.VMEM((SUBLANES, CONF_HALO + ts, CONF_CH), F32),
            pltpu.VMEM((ts, CONF_CH), F32),
        ],
        compiler_params=_params("parallel", "arbitrary"),
        name="conformer_conv",
    )(h, shift, dw, dw_b[None, :], ln_g[None, :], ln_b[None, :], w_proj.astype(BF16))


def _pair_interleave(width):
    half = ATT_HEAD_DIM // 2
    j = jnp.arange(width, dtype=jnp.int32)
    block, lane = j // LANES, j % LANES
    chunk, within = lane // half, lane % half
    return block * LANES + (chunk % 2) * ATT_HEAD_DIM + (chunk // 2) * half + within


def _attn_kernel(q_ref, kc_ref, kp_ref, vc_ref, vp_ref, cosc_ref, sinc_ref, cosp_ref, sinp_ref, sink_ref,
                 wp_ref, y_ref, obuf_ref):
    blk = ATT_BLOCK
    n_sub = q_ref.shape[0] // blk
    i = pl.program_id(1)
    rep = ATT_Q_HEADS // ATT_KV_HEADS
    quarter = ATT_HEAD_DIM // 2
    scale = ATT_HEAD_DIM ** -0.5

    def lane_masks(rows):
        lane = lax.broadcasted_iota(jnp.int32, (rows, LANES), 1)
        return (lane % ATT_HEAD_DIM) < quarter, lane < ATT_HEAD_DIM

    def rope(x, cos, sin):
        return x * cos + pltpu.roll(x, ATT_HEAD_DIM, 1) * sin

    first_q, lo_q = lane_masks(blk)
    first_k, lo_k = lane_masks((n_sub + 1) * blk)

    cos_k = jnp.concatenate([cosp_ref[...], cosc_ref[...]], axis=0)
    sin_k = jnp.concatenate([sinp_ref[...], sinc_ref[...]], axis=0)
    k = jnp.concatenate([kp_ref[...], kc_ref[...]], axis=0).astype(F32)
    v = jnp.concatenate([vp_ref[...], vc_ref[...]], axis=0).astype(F32)

    qi = lax.broadcasted_iota(jnp.int32, (blk, 2 * blk), 0)
    kj = lax.broadcasted_iota(jnp.int32, (blk, 2 * blk), 1)
    mask_inner = (kj <= qi + blk) & (kj > qi)
    mask_first = (kj <= qi + blk) & (kj > jnp.where(i == 0, blk - 1, qi))

    for kh in range(ATT_KV_HEADS):
        kb = rope(k[:, (kh // 2) * LANES:(kh // 2 + 1) * LANES], cos_k, sin_k)
        vb = v[:, (kh // 2) * LANES:(kh // 2 + 1) * LANES]
        if kh % 2 == 0:
            k_dup = jnp.where(first_k, kb, pltpu.roll(kb, quarter, 1))
            v_dup = jnp.where(lo_k, vb, pltpu.roll(vb, ATT_HEAD_DIM, 1))
        else:
            k_dup = jnp.where(first_k, pltpu.roll(kb, LANES - quarter, 1), kb)
            v_dup = jnp.where(lo_k, pltpu.roll(vb, ATT_HEAD_DIM, 1), vb)
        k_dup = k_dup.astype(BF16)
        v_ext = jnp.concatenate([v_dup, jnp.ones_like(v_dup)], axis=1).astype(BF16)

        for j in range(n_sub):
            q_rows = slice(j * blk, (j + 1) * blk)
            key_rows = slice(j * blk, (j + 2) * blk)
            mask = mask_first if j == 0 else mask_inner
            cos_q, sin_q = cosc_ref[q_rows, :], sinc_ref[q_rows, :]

            stacked = []
            for qb in range(rep // 2):
                cols = slice((kh * (rep // 2) + qb) * LANES, (kh * (rep // 2) + qb + 1) * LANES)
                qr = rope(q_ref[q_rows, cols].astype(F32), cos_q, sin_q) * scale
                stacked.append(jnp.where(first_q, qr, 0.0).astype(BF16))
                stacked.append(jnp.where(first_q, 0.0, qr).astype(BF16))
            qs = jnp.concatenate(stacked, axis=0)
            s_all = lax.dot_general(qs, k_dup[key_rows], (((1,), (1,)), ((), ())), preferred_element_type=F32)

            es, sink_terms = [], []
            for r in range(rep):
                hq = kh * rep + r
                s = jnp.where(mask, s_all[r * blk:(r + 1) * blk], -jnp.inf)
                sink = sink_ref[hq:hq + 1, :]
                mx = jnp.maximum(jnp.broadcast_to(jnp.max(s, axis=-1, keepdims=True), (blk, LANES)), sink)
                sink_terms.append(jnp.exp(sink - mx))
                es.append(jnp.exp(s - jnp.concatenate([mx, mx], axis=1)).astype(BF16))
            o = jnp.dot(jnp.concatenate(es, axis=0), v_ext[key_rows], preferred_element_type=F32)
            outs = []
            for r in range(rep):
                rows = slice(r * blk, (r + 1) * blk)
                outs.append(o[rows, :LANES] * (1.0 / (o[rows, LANES:] + sink_terms[r])))
            for qb in range(rep // 2):
                cols = slice((kh * (rep // 2) + qb) * LANES, (kh * (rep // 2) + qb + 1) * LANES)
                obuf_ref[q_rows, cols] = jnp.where(lo_q, outs[2 * qb], outs[2 * qb + 1])

    y_ref[...] = jnp.dot(obuf_ref[...].astype(BF16), wp_ref[...], preferred_element_type=F32).astype(BF16)


ATT_TQ = 4 * ATT_BLOCK


def _attention(h, cos_t, sin_t, bsz, seq, sinks, w_proj):
    t = bsz * seq
    blk = ATT_BLOCK
    tq = min(ATT_TQ, seq)
    n_sub = tq // blk
    nt = seq // tq
    nb = seq // blk
    qw = ATT_Q_HEADS * ATT_HEAD_DIM
    kw = ATT_KV_HEADS * ATT_HEAD_DIM
    cur = lambda b, i: b * nt + i
    prev = lambda b, i: b * nb + jnp.maximum(i * n_sub - 1, 0)
    const = lambda b, i: (0, 0)
    sink = jnp.broadcast_to(sinks[:, None], (ATT_Q_HEADS, LANES))
    return pl.pallas_call(
        _attn_kernel,
        out_shape=jax.ShapeDtypeStruct((t, D_MODEL), BF16),
        grid=(bsz, nt),
        in_specs=[
            pl.BlockSpec((tq, qw), lambda b, i: (cur(b, i), COL_Q // qw)),
            pl.BlockSpec((tq, kw), lambda b, i: (cur(b, i), COL_K // kw)),
            pl.BlockSpec((blk, kw), lambda b, i: (prev(b, i), COL_K // kw)),
            pl.BlockSpec((tq, kw), lambda b, i: (cur(b, i), COL_V // kw)),
            pl.BlockSpec((blk, kw), lambda b, i: (prev(b, i), COL_V // kw)),
            pl.BlockSpec((tq, LANES), lambda b, i: (cur(b, i), 0)),
            pl.BlockSpec((tq, LANES), lambda b, i: (cur(b, i), 0)),
            pl.BlockSpec((blk, LANES), lambda b, i: (prev(b, i), 0)),
            pl.BlockSpec((blk, LANES), lambda b, i: (prev(b, i), 0)),
            pl.BlockSpec((ATT_Q_HEADS, LANES), const),
            pl.BlockSpec((qw, D_MODEL), const),
        ],
        out_specs=pl.BlockSpec((tq, D_MODEL), lambda b, i: (cur(b, i), 0)),
        scratch_shapes=[pltpu.VMEM((tq, qw), F32)],
        compiler_params=_params("parallel", "arbitrary"),
        name="swa_attention",
    )(h, h, h, h, h, cos_t, sin_t, cos_t, sin_t, sink, w_proj.astype(BF16))


MERGE_TM = 512


def _merge_kernel(g0_ref, g1_ref, g2_ref, bg_ref, y0_ref, y1_ref, y2_ref, x_ref, wo_ref, lg_ref, lb_ref,
                  wrh_ref, wrl_ref, br_ref, x1_ref, logit_ref):
    mixed = None
    for n, (g_ref, y_ref) in enumerate(((g0_ref, y0_ref), (g1_ref, y1_ref), (g2_ref, y2_ref))):
        term = _sigmoid(g_ref[...].astype(F32) + bg_ref[n:n + 1, :]) * y_ref[...].astype(F32)
        mixed = term if mixed is None else mixed + term
    r = DEEPNORM_ALPHA * x_ref[...] + jnp.dot(mixed.astype(BF16), wo_ref[...], preferred_element_type=F32)
    x1 = _layer_norm(r, lg_ref[...], lb_ref[...])
    _store_row_tiles(x1_ref, x1)
    x_hi = x1.astype(BF16)
    x_lo = (x1 - x_hi.astype(F32)).astype(BF16)
    logit_ref[...] = (jnp.dot(x_hi, wrh_ref[...], preferred_element_type=F32)
                      + (jnp.dot(x_hi, wrl_ref[...], preferred_element_type=F32)
                         + jnp.dot(x_lo, wrh_ref[...], preferred_element_type=F32))) + br_ref[...]


def _merge(h, b_gate, y_ssd, y_conf, y_att, x2d, w_out, ln_g, ln_b, w_router, b_router):
    t = x2d.shape[0]
    tm = min(MERGE_TM, t)
    const = lambda i: (0, 0)
    row = lambda i: (i, 0)
    w_router_hi = w_router.astype(BF16)
    w_router_lo = (w_router - w_router_hi.astype(F32)).astype(BF16)
    return pl.pallas_call(
        _merge_kernel,
        out_shape=(jax.ShapeDtypeStruct((t * ROW_TILES, LANES), F32), jax.ShapeDtypeStruct((t, LANES), F32)),
        grid=(t // tm,),
        in_specs=[
            pl.BlockSpec((tm, D_MODEL), lambda i: (i, 0)),
            pl.BlockSpec((tm, D_MODEL), lambda i: (i, 1)),
            pl.BlockSpec((tm, D_MODEL), lambda i: (i, 2)),
            pl.BlockSpec((3, D_MODEL), const),
            pl.BlockSpec((tm, D_MODEL), row),
            pl.BlockSpec((tm, D_MODEL), row),
            pl.BlockSpec((tm, D_MODEL), row),
            pl.BlockSpec((tm, D_MODEL), row),
            pl.BlockSpec((D_MODEL, D_MODEL), const),
            pl.BlockSpec((1, D_MODEL), const),
            pl.BlockSpec((1, D_MODEL), const),
            pl.BlockSpec((D_MODEL, LANES), const),
            pl.BlockSpec((D_MODEL, LANES), const),
            pl.BlockSpec((1, LANES), const),
        ],
        out_specs=(pl.BlockSpec((tm * ROW_TILES, LANES), row), pl.BlockSpec((tm, LANES), row)),
        compiler_params=_params("parallel"),
        name="merge_ln_router",
    )(h, h, h, b_gate, y_ssd, y_conf, y_att, x2d, w_out.astype(BF16), ln_g[None, :], ln_b[None, :],
      w_router_hi, w_router_lo, b_router)


GATHER_UNROLL = 8


def _on_slot(slot, fn, n_slots=2):
    for s in range(n_slots):
        @pl.when(slot == s)
        def _(s=s):
            fn(s)


def _gather_pipeline(i, n_steps, idx_hbm, src_hbm, idx_smem, buf, isem, gsem):
    n_rows = buf[0].shape[0] // ROW_TILES
    slot = i % 2

    def idx_copy(step, s):
        return pltpu.make_async_copy(idx_hbm.at[step], idx_smem[s], isem.at[s])

    def issue_rows(s):
        def body(it, carry):
            for u in range(GATHER_UNROLL):
                r = it * GATHER_UNROLL + u
                src_row = pl.multiple_of(idx_smem[s][r] * ROW_TILES, ROW_TILES)
                dst_row = pl.multiple_of(r * ROW_TILES, ROW_TILES)
                pltpu.make_async_copy(src_hbm.at[pl.ds(src_row, ROW_TILES)],
                                      buf[s].at[pl.ds(dst_row, ROW_TILES)], gsem.at[s]).start(priority=u % 2)
            return carry

        lax.fori_loop(0, n_rows // GATHER_UNROLL, body, 0)

    @pl.when(i == 0)
    def _():
        idx_copy(0, 0).start()
        idx_copy(0, 0).wait()
        issue_rows(0)

        @pl.when(n_steps > 1)
        def _():
            idx_copy(1, 1).start()

    @pl.when(i + 1 < n_steps)
    def _():
        def prefetch(s):
            idx_copy(i + 1, 1 - s).wait()
            issue_rows(1 - s)

            @pl.when(i + 2 < n_steps)
            def _():
                idx_copy(i + 2, s).start()

        _on_slot(slot, prefetch)


def _gather_wait(s, src_hbm, buf, gsem):
    pltpu.make_async_copy(src_hbm.at[pl.ds(0, buf[s].shape[0])], buf[s], gsem.at[s]).wait()


DISPATCH_SLOTS = 3


def _dispatch_kernel(pad_start_ref, pad_len_ref, idx_hbm, x1_hbm, rows_hbm, idx0, idx1, idx2, st0, st1, st2,
                     zero_ref, isem, xsem, dsem, zsem):
    i = pl.program_id(0)
    n_steps = pl.num_programs(0)
    idx = (idx0, idx1, idx2)
    stage = (st0, st1, st2)
    n_idx = idx0.shape[0]
    tm = n_idx // MOE_TOP_K

    def loads(step, s):
        first = pl.multiple_of(step * (tm * ROW_TILES), tm * ROW_TILES)
        return (pltpu.make_async_copy(idx_hbm.at[step], idx[s], isem.at[s]),
                pltpu.make_async_copy(x1_hbm.at[pl.ds(first, tm * ROW_TILES)], stage[s], xsem.at[s]))

    def start_loads(step, s):
        for c in loads(step, s):
            c.start()

    def wait_rows(s):
        pltpu.make_async_copy(x1_hbm.at[pl.ds(0, n_idx * ROW_TILES)], rows_hbm.at[pl.ds(0, n_idx * ROW_TILES)],
                              dsem.at[s]).wait()

    def zero_copy(row, n_rows):
        dst_row = pl.multiple_of(row * ROW_TILES, ROW_TILES)
        if n_rows is None:
            return pltpu.make_async_copy(zero_ref, rows_hbm.at[pl.ds(dst_row, ROW_TILES)], zsem.at[0])
        return pltpu.make_async_copy(x1_hbm.at[pl.ds(0, n_rows * ROW_TILES)],
                                     rows_hbm.at[pl.ds(dst_row, n_rows * ROW_TILES)], zsem.at[0])

    @pl.when(i == 0)
    def _():
        start_loads(0, 0)

        @pl.when(n_steps > 1)
        def _():
            start_loads(1, 1)

        zero_ref[...] = jnp.zeros(zero_ref.shape, F32)

        def zero_expert(e, carry):
            def zero_row(k, c):
                zero_copy(pad_start_ref[e] + k, None).start()
                return c

            return lax.fori_loop(0, pad_len_ref[e], zero_row, carry)

        lax.fori_loop(0, pad_len_ref.shape[0], zero_expert, 0)

    @pl.when(i == n_steps - 1)
    def _():
        def wait_expert(e, carry):
            @pl.when(pad_len_ref[e] > 0)
            def _():
                zero_copy(pad_start_ref[e], pad_len_ref[e]).wait()

            return carry

        lax.fori_loop(0, pad_len_ref.shape[0], wait_expert, 0)

    def step(s):
        freed = (s + DISPATCH_SLOTS - 1) % DISPATCH_SLOTS
        for c in loads(i, s):
            c.wait()

        def body(it, carry):
            for u in range(GATHER_UNROLL):
                r = it * GATHER_UNROLL + u
                src_row = pl.multiple_of((r % tm) * ROW_TILES, ROW_TILES)
                dst_row = pl.multiple_of(idx[s][r] * ROW_TILES, ROW_TILES)
                pltpu.make_async_copy(stage[s].at[pl.ds(src_row, ROW_TILES)],
                                      rows_hbm.at[pl.ds(dst_row, ROW_TILES)], dsem.at[s]).start(priority=u % 2)
            return carry

        lax.fori_loop(0, n_idx // GATHER_UNROLL, body, 0)

        @pl.when(i > 0)
        def _():
            wait_rows(freed)

        @pl.when(i + 2 < n_steps)
        def _():
            start_loads(i + 2, freed)

        @pl.when(i == n_steps - 1)
        def _():
            wait_rows(s)

    _on_slot(i % DISPATCH_SLOTS, step, DISPATCH_SLOTS)


def _moe_dispatch(x1, dest_blocks, pad_start, pad_len, n_rows):
    n_steps, n_idx = dest_blocks.shape
    any_spec = pl.BlockSpec(memory_space=pl.ANY)
    grid_spec = pltpu.PrefetchScalarGridSpec(
        num_scalar_prefetch=2,
        grid=(n_steps,),
        in_specs=[any_spec, any_spec],
        out_specs=any_spec,
        scratch_shapes=(
            [pltpu.SMEM((n_idx,), jnp.int32)] * DISPATCH_SLOTS
            + [pltpu.VMEM((n_idx // MOE_TOP_K * ROW_TILES, LANES), F32)] * DISPATCH_SLOTS
            + [pltpu.VMEM((ROW_TILES, LANES), F32)]
            + [pltpu.SemaphoreType.DMA((DISPATCH_SLOTS,))] * 3
            + [pltpu.SemaphoreType.DMA((1,))]
        ),
    )
    return pl.pallas_call(
        _dispatch_kernel,
        out_shape=jax.ShapeDtypeStruct((n_rows * ROW_TILES, LANES), F32),
        grid_spec=grid_spec,
        compiler_params=_params("arbitrary"),
        name="moe_dispatch",
    )(pad_start, pad_len, dest_blocks, x1)


def _moe_kernel(be_ref, nu_ref, x_ref, wg_ref, wu_ref, wd_ref, y_ref, wgb_ref, wub_ref, wdb_ref):
    i = pl.program_id(0)

    @pl.when(i < nu_ref[0])
    def _():
        @pl.when((i == 0) | (be_ref[i] != be_ref[jnp.maximum(i - 1, 0)]))
        def _():
            wgb_ref[...] = wg_ref[...].astype(BF16)
            wub_ref[...] = wu_ref[...].astype(BF16)
            wdb_ref[...] = wd_ref[...].astype(BF16)

        xb = _load_row_tiles(x_ref, 0, MOE_BLOCK).astype(BF16)
        gate = jnp.dot(xb, wgb_ref[...], preferred_element_type=F32)
        up = jnp.dot(xb, wub_ref[...], preferred_element_type=F32)
        hmid = (_silu(gate) * up).astype(BF16)
        _store_row_tiles(y_ref, jnp.dot(hmid, wdb_ref[...], preferred_element_type=F32))

    @pl.when(i >= nu_ref[0])
    def _():
        y_ref[...] = jnp.zeros(y_ref.shape, F32)


def _moe_experts(x_rows, block_e, n_used, w_gate, w_up, w_down):
    n_blocks = block_e.shape[0]
    grid_spec = pltpu.PrefetchScalarGridSpec(
        num_scalar_prefetch=2,
        grid=(n_blocks,),
        in_specs=[
            pl.BlockSpec((MOE_BLOCK * ROW_TILES, LANES), lambda i, be, nu: (jnp.minimum(i, nu[0] - 1), 0)),
            pl.BlockSpec((None, D_MODEL, MOE_FF), lambda i, be, nu: (be[i], 0, 0)),
            pl.BlockSpec((None, D_MODEL, MOE_FF), lambda i, be, nu: (be[i], 0, 0)),
            pl.BlockSpec((None, MOE_FF, D_MODEL), lambda i, be, nu: (be[i], 0, 0)),
        ],
        out_specs=pl.BlockSpec((MOE_BLOCK * ROW_TILES, LANES), lambda i, be, nu: (i, 0)),
        scratch_shapes=[
            pltpu.VMEM((D_MODEL, MOE_FF), BF16),
            pltpu.VMEM((D_MODEL, MOE_FF), BF16),
            pltpu.VMEM((MOE_FF, D_MODEL), BF16),
        ],
    )
    return pl.pallas_call(
        _moe_kernel,
        out_shape=jax.ShapeDtypeStruct((n_blocks * MOE_BLOCK * ROW_TILES, LANES), F32),
        grid_spec=grid_spec,
        compiler_params=_params("arbitrary"),
        name="moe_experts",
    )(block_e, n_used, x_rows, w_gate, w_up, w_down)


COMB_TM = 256


def _combine_kernel(idx_hbm, y_hbm, ew_ref, x1_ref, p_ref, lg_ref, lb_ref, wpg_ref, wpp_ref, out_ref,
                    idx0, idx1, ybuf0, ybuf1, ffn_ref, isem, gsem):
    i = pl.program_id(0)
    tm = out_ref.shape[0]
    ybuf = (ybuf0, ybuf1)
    _gather_pipeline(i, pl.num_programs(0), idx_hbm, y_hbm, (idx0, idx1), ybuf, isem, gsem)

    def weighted_sum(s):
        _gather_wait(s, y_hbm, ybuf, gsem)
        ew = ew_ref[...]
        ffn_ref[...] = (_load_row_tiles(ybuf[s], 0, tm) * ew[:, 0:1]
                        + _load_row_tiles(ybuf[s], tm, tm) * ew[:, 1:2])

    _on_slot(i % 2, weighted_sum)
    x1 = _load_row_tiles(x1_ref, 0, tm)
    x2 = _layer_norm(DEEPNORM_ALPHA * x1 + ffn_ref[...], lg_ref[...], lb_ref[...])
    gate = _sigmoid(jnp.dot(x2.astype(BF16), wpg_ref[...], preferred_element_type=F32))
    emb = jnp.dot(p_ref[...].astype(BF16), wpp_ref[...], preferred_element_type=F32)
    out_ref[...] = x2 + gate * emb


def _combine(y_rows, dest_blocks, e_w, x1, p2d, ln_g, ln_b, ple_w_gate, ple_w_proj):
    t = x1.shape[0] // ROW_TILES
    tm = dest_blocks.shape[1] // MOE_TOP_K
    const = lambda i: (0, 0)
    row = lambda i: (i, 0)
    return pl.pallas_call(
        _combine_kernel,
        out_shape=jax.ShapeDtypeStruct((t, D_MODEL), F32),
        grid=(t // tm,),
        in_specs=[
            pl.BlockSpec(memory_space=pl.ANY),
            pl.BlockSpec(memory_space=pl.ANY),
            pl.BlockSpec((tm, MOE_TOP_K), row),
            pl.BlockSpec((tm * ROW_TILES, LANES), row),
            pl.BlockSpec((tm, PLE_DIM), row),
            pl.BlockSpec((1, D_MODEL), const),
            pl.BlockSpec((1, D_MODEL), const),
            pl.BlockSpec((D_MODEL, D_MODEL), const),
            pl.BlockSpec((PLE_DIM, D_MODEL), const),
        ],
        out_specs=pl.BlockSpec((tm, D_MODEL), row),
        scratch_shapes=[
            pltpu.SMEM((MOE_TOP_K * tm,), jnp.int32),
            pltpu.SMEM((MOE_TOP_K * tm,), jnp.int32),
            pltpu.VMEM((MOE_TOP_K * tm * ROW_TILES, LANES), F32),
            pltpu.VMEM((MOE_TOP_K * tm * ROW_TILES, LANES), F32),
            pltpu.VMEM((tm, D_MODEL), F32),
            pltpu.SemaphoreType.DMA((2,)),
            pltpu.SemaphoreType.DMA((2,)),
        ],
        compiler_params=_params("arbitrary"),
        name="moe_combine",
    )(dest_blocks, y_rows, e_w, x1, p2d, ln_g[None, :], ln_b[None, :], ple_w_gate.astype(BF16),
      ple_w_proj.astype(BF16))


ROUTE_CHUNK = 256


def _route(logits):
    t = logits.shape[0]
    tk = t * MOE_TOP_K
    g_logits = logits[:, :MOE_GROUPS]
    e_logits = logits[:, MOE_GROUPS:MOE_GROUPS + MOE_EXPERTS].reshape(t, MOE_GROUPS, MOE_EXPERTS_PER_GROUP)
    g_prob = jax.nn.softmax(g_logits, axis=-1)
    g_idx = jnp.argmax(g_logits, axis=-1).astype(jnp.int32)
    g_sel = g_idx[:, None] == jnp.arange(MOE_GROUPS, dtype=jnp.int32)[None, :]
    g_w = jnp.sum(jnp.where(g_sel, g_prob, 0.0), axis=1, keepdims=True)
    e_in = jnp.sum(jnp.where(g_sel[:, :, None], e_logits, 0.0), axis=1)
    lane = jnp.arange(MOE_EXPERTS_PER_GROUP, dtype=jnp.int32)[None, :]
    i1 = jnp.argmax(e_in, axis=-1).astype(jnp.int32)
    rest = jnp.where(lane == i1[:, None], -jnp.inf, e_in)
    i2 = jnp.argmax(rest, axis=-1).astype(jnp.int32)
    top_i = jnp.stack([i1, i2], axis=-1)
    top_v = jnp.stack([jnp.max(e_in, axis=-1), jnp.max(rest, axis=-1)], axis=-1)
    e_w = jax.nn.softmax(top_v, axis=-1) * g_w
    e_id = g_idx[:, None] * MOE_EXPERTS_PER_GROUP + top_i
    flat_e = e_id.reshape(tk)

    chunk = min(ROUTE_CHUNK, tk)
    onehot = flat_e[:, None] == jnp.arange(MOE_EXPERTS, dtype=jnp.int32)[None, :]
    oh = onehot.reshape(tk // chunk, chunk, MOE_EXPERTS)
    tril = jnp.tril(jnp.ones((chunk, chunk), BF16))
    within = jnp.einsum("ij,cje->cie", tril, oh.astype(BF16), preferred_element_type=F32).astype(jnp.int32)
    chunk_counts = within[:, -1, :]
    chunk_ends = jnp.cumsum(chunk_counts, axis=0)
    csum = within + (chunk_ends - chunk_counts)[:, None, :]
    rank = jnp.sum(jnp.where(oh, csum, 0), axis=-1).reshape(tk) - 1
    counts = chunk_ends[-1]
    padded = ((counts + MOE_BLOCK - 1) // MOE_BLOCK) * MOE_BLOCK
    pends = jnp.cumsum(padded)
    pstarts = pends - padded
    dest = pstarts[flat_e] + rank

    n_rows = tk + MOE_EXPERTS * MOE_BLOCK
    n_blocks = n_rows // MOE_BLOCK
    block_start = jnp.arange(n_blocks, dtype=jnp.int32) * MOE_BLOCK
    block_e = jnp.sum((pends[None, :] <= block_start[:, None]).astype(jnp.int32), axis=1)
    block_e = jnp.minimum(block_e, MOE_EXPERTS - 1).astype(jnp.int32)
    n_used = (pends[-1] // MOE_BLOCK).astype(jnp.int32).reshape(1)
    pad = (jnp.concatenate([pstarts + counts, pends[-1:]]),
           jnp.concatenate([padded - counts, n_rows - pends[-1:]]))
    return e_w, dest.reshape(t, MOE_TOP_K), block_e, n_used, pad, n_rows


def _layer(x2d, p2d, cos_t, sin_t, bsz, seq, w_in, b_gate, ssd_conv_w, ssd_conv_b, ssd_dt_bias, ssd_a_log,
           ssd_d, ssd_norm_w, ssd_w_out, conf_dw_w, conf_dw_b, conf_ln_g, conf_ln_b, conf_w_out, attn_sinks,
           attn_w_out, w_out, ln1_g, ln1_b, moe_w_group, moe_b_group, moe_w_expert, moe_b_expert, moe_w_gate,
           moe_w_up, moe_w_down, ln2_g, ln2_b, ple_w_gate, ple_w_proj):
    t = bsz * seq
    w_main = jnp.concatenate([w_in[:, :DT_COL_ORIG], w_in[:, DT_COL_ORIG + SSD_HEADS:]], axis=1)
    w_main = jnp.concatenate([w_main[:, :COL_Q], w_main[:, COL_Q:COL_V][:, _pair_interleave(COL_V - COL_Q)],
                              w_main[:, COL_V:]], axis=1).astype(BF16)
    w_dt = jnp.pad(w_in[:, DT_COL_ORIG:DT_COL_ORIG + SSD_HEADS], ((0, 0), (0, LANES - SSD_HEADS))).astype(BF16)
    h, dt_raw = _in_projection(x2d, w_main, w_dt)

    y_ssd = _ssd_mixer(h, dt_raw, bsz, seq, ssd_conv_w, ssd_conv_b, ssd_dt_bias, ssd_a_log, ssd_d, ssd_norm_w,
                       ssd_w_out)
    y_conf = _conformer(h, bsz, seq, conf_dw_w, conf_dw_b, conf_ln_g, conf_ln_b, conf_w_out)
    y_att = _attention(h, cos_t, sin_t, bsz, seq, attn_sinks, attn_w_out)

    n_router = MOE_GROUPS + MOE_EXPERTS
    w_router = jnp.pad(jnp.concatenate([moe_w_group, moe_w_expert], axis=1), ((0, 0), (0, LANES - n_router)))
    b_router = jnp.pad(jnp.concatenate([moe_b_group, moe_b_expert]), (0, LANES - n_router))[None, :]
    x1, logits = _merge(h, b_gate, y_ssd, y_conf, y_att, x2d, w_out, ln1_g, ln1_b, w_router, b_router)

    e_w, dest, block_e, n_used, (pad_start, pad_len), n_rows = _route(logits)
    tm = min(COMB_TM, t)
    dest_blocks = dest.reshape(t // tm, tm, MOE_TOP_K).transpose(0, 2, 1).reshape(t // tm, MOE_TOP_K * tm)
    x_rows = _moe_dispatch(x1, dest_blocks, pad_start, pad_len, n_rows)
    y_rows = _moe_experts(x_rows, block_e, n_used, moe_w_gate, moe_w_up, moe_w_down)
    return _combine(y_rows, dest_blocks, e_w, x1, p2d, ln2_g, ln2_b, ple_w_gate, ple_w_proj)


def kernel(x, p, positions, w_in, b_gate, ssd_conv_w, ssd_conv_b, ssd_dt_bias, ssd_a_log, ssd_d, ssd_norm_w, ssd_w_out, conf_dw_w, conf_dw_b, conf_ln_g, conf_ln_b, conf_w_out, attn_sinks, attn_w_out, w_out, ln1_g, ln1_b, moe_w_group, moe_b_group, moe_w_expert, moe_b_expert, moe_w_gate, moe_w_up, moe_w_down, ln2_g, ln2_b, ple_w_gate, ple_w_proj):
    bsz, seq, d = x.shape
    t = bsz * seq
    cos_t, sin_t = _rope_tables(positions)
    x2d = x.reshape(t, d)
    per_layer = (w_in, b_gate, ssd_conv_w, ssd_conv_b, ssd_dt_bias, ssd_a_log, ssd_d, ssd_norm_w, ssd_w_out,
                 conf_dw_w, conf_dw_b, conf_ln_g, conf_ln_b, conf_w_out, attn_sinks, attn_w_out, w_out, ln1_g,
                 ln1_b, moe_w_group, moe_b_group, moe_w_expert, moe_b_expert, moe_w_gate, moe_w_up, moe_w_down,
                 ln2_g, ln2_b, ple_w_gate, ple_w_proj)
    for layer in range(w_in.shape[0]):
        x2d = _layer(x2d, p[layer].reshape(t, -1), cos_t, sin_t, bsz, seq, *(w[layer] for w in per_layer))
    return x2d.reshape(bsz, seq, d)
```

```python
import functools

import jax
import jax.numpy as jnp
from jax import lax
from jax.experimental import pallas as pl
from jax.experimental.pallas import tpu as pltpu

F32 = jnp.float32
BF16 = jnp.bfloat16

D_MODEL = 1024
N_LAYERS = 2
PLE_DIM = 256
SSD_HEADS = 16
SSD_HEAD_DIM = 64
SSD_INNER = SSD_HEADS * SSD_HEAD_DIM
SSD_GROUPS = 4
SSD_STATE = 128
SSD_CONV = 4
SSD_CHUNK = 128
SSD_XBC = SSD_INNER + 2 * SSD_GROUPS * SSD_STATE
CONF_CH = 1024
CONF_KERNEL = 31
ATT_Q_HEADS = 16
ATT_KV_HEADS = 4
ATT_HEAD_DIM = 64
ATT_BLOCK = 128
ROPE_THETA = 10000.0
MOE_GROUPS = 4
MOE_EXPERTS_PER_GROUP = 8
MOE_EXPERTS = MOE_GROUPS * MOE_EXPERTS_PER_GROUP
MOE_TOP_K = 2
MOE_FF = 512
MOE_BLOCK = 256
DEEPNORM_ALPHA = (2 * N_LAYERS) ** 0.25
LN_EPS = 1e-5

LANES = 128
SUBLANES = 8
VMEM_LIMIT_BYTES = 56 * 1024 * 1024

COL_GATES = 0
COL_Z = 3 * D_MODEL
COL_XBC = COL_Z + SSD_INNER
COL_U = COL_XBC + SSD_XBC
COL_Q = COL_U + 2 * CONF_CH
COL_K = COL_Q + ATT_Q_HEADS * ATT_HEAD_DIM
COL_V = COL_K + ATT_KV_HEADS * ATT_HEAD_DIM
H_WIDTH = COL_V + ATT_KV_HEADS * ATT_HEAD_DIM
DT_COL_ORIG = 3 * D_MODEL + SSD_INNER + SSD_XBC


def _params(*semantics):
    return pltpu.CompilerParams(dimension_semantics=semantics, vmem_limit_bytes=VMEM_LIMIT_BYTES)


def _sigmoid(x):
    return 0.5 * (jnp.tanh(0.5 * x) + 1.0)


def _silu(x):
    return x * jax.nn.sigmoid(x)


def _layer_norm(x, g, b):
    mu = jnp.mean(x, axis=-1, keepdims=True)
    xc = x - mu
    var = jnp.mean(xc * xc, axis=-1, keepdims=True)
    return xc * lax.rsqrt(var + LN_EPS) * g + b


ROW_TILES = D_MODEL // LANES


def _store_row_tiles(ref, x):
    rows = x.shape[0]
    for k in range(ROW_TILES):
        ref[pl.ds(k, rows, stride=ROW_TILES), :] = x[:, k * LANES:(k + 1) * LANES]


def _load_row_tiles(ref, first_row, rows):
    return jnp.concatenate(
        [ref[pl.ds(first_row * ROW_TILES + k, rows, stride=ROW_TILES), :] for k in range(ROW_TILES)], axis=1)


IN_TM = 512
IN_TN = 512


def _inproj_kernel(x_ref, w_ref, wdt_ref, h_ref, dt_ref):
    xb = x_ref[...].astype(BF16)
    dt_ref[...] = jnp.dot(xb, wdt_ref[...], preferred_element_type=F32)
    for j in range(H_WIDTH // IN_TN):
        cols = slice(j * IN_TN, (j + 1) * IN_TN)
        h_ref[:, cols] = jnp.dot(xb, w_ref[:, cols], preferred_element_type=F32).astype(BF16)


def _in_projection(x2d, w_main, w_dt):
    t = x2d.shape[0]
    tm = min(IN_TM, t)
    resident = pl.Buffered(1)
    return pl.pallas_call(
        _inproj_kernel,
        out_shape=(jax.ShapeDtypeStruct((t, H_WIDTH), BF16), jax.ShapeDtypeStruct((t, LANES), F32)),
        grid=(t // tm,),
        in_specs=[
            pl.BlockSpec((tm, D_MODEL), lambda i: (i, 0)),
            pl.BlockSpec((D_MODEL, H_WIDTH), lambda i: (0, 0), pipeline_mode=resident),
            pl.BlockSpec((D_MODEL, LANES), lambda i: (0, 0), pipeline_mode=resident),
        ],
        out_specs=(
            pl.BlockSpec((tm, H_WIDTH), lambda i: (i, 0)),
            pl.BlockSpec((tm, LANES), lambda i: (i, 0)),
        ),
        compiler_params=_params("parallel"),
        name="in_projection",
    )(x2d, w_main, w_dt)


def _rope_kernel(pos_ref, inv_ref, sign_ref, cos_ref, sin_ref):
    ang = pos_ref[...].astype(F32) * inv_ref[...]
    cos_ref[...] = jnp.cos(ang)
    sin_ref[...] = jnp.sin(ang) * sign_ref[...]


def _rope_tables(positions):
    t = positions.size
    tm = min(1024, t)
    half = ATT_HEAD_DIM // 2
    inv_freq = ROPE_THETA ** (-jnp.arange(half, dtype=F32) / half)
    inv = jnp.tile(inv_freq, LANES // half)[None, :]
    sign = jnp.concatenate([-jnp.ones((LANES // 2,), F32), jnp.ones((LANES // 2,), F32)])[None, :]
    return pl.pallas_call(
        _rope_kernel,
        out_shape=(jax.ShapeDtypeStruct((t, LANES), F32), jax.ShapeDtypeStruct((t, LANES), F32)),
        grid=(t // tm,),
        in_specs=[
            pl.BlockSpec((tm, 1), lambda i: (i, 0)),
            pl.BlockSpec((1, LANES), lambda i: (0, 0)),
            pl.BlockSpec((1, LANES), lambda i: (0, 0)),
        ],
        out_specs=(pl.BlockSpec((tm, LANES), lambda i: (i, 0)), pl.BlockSpec((tm, LANES), lambda i: (i, 0))),
        compiler_params=_params("parallel"),
        name="rope_tables",
    )(positions.reshape(t, 1), inv, sign)


SSD_PAIRS = SSD_HEADS // 2
SSD_STEP_CHUNKS = 4


def _ssd_kernel(xbc_ref, z_ref, dt_ref, shift_ref, cw_ref, cb_ref, dtb_ref, alog_ref, dsk_ref, nw_ref, wp_ref,
                y_ref, state_ref, ext_ref, ybuf_ref):
    L = SSD_CHUNK
    n_sub = xbc_ref.shape[0] // L
    c = pl.program_id(1)

    @pl.when(c == 0)
    def _():
        state_ref[...] = jnp.zeros(state_ref.shape, F32)
        ext_ref[0:L, :] = jnp.zeros((L, SSD_XBC), BF16)

    @pl.when(c > 0)
    def _():
        ext_ref[0:L, :] = ext_ref[n_sub * L:(n_sub + 1) * L, :]

    ext_ref[L:(n_sub + 1) * L, :] = xbc_ref[...]

    row = lax.broadcasted_iota(jnp.int32, (L, L), 0)
    col = lax.broadcasted_iota(jnp.int32, (L, L), 1)
    causal = row >= col
    tril = causal.astype(F32)
    lo = col < SSD_HEAD_DIM
    a = -jnp.exp(alog_ref[...])

    steps = []
    for sub in range(n_sub):
        x_dt = dt_ref[sub * L:(sub + 1) * L, :] + dtb_ref[...]
        dt = jnp.maximum(x_dt, 0.0) + jnp.log1p(jnp.exp(-jnp.abs(x_dt)))
        cs = jnp.dot(tril, dt * a, preferred_element_type=F32, precision=lax.Precision.HIGHEST)
        steps.append((dt, cs, cs.T))

    for sub in range(n_sub):
        rows = slice(sub * L, (sub + 1) * L)
        dt, cs, cs_t = steps[sub]

        taps = jnp.dot(shift_ref[...], ext_ref[sub * L:(sub + 2) * L, :], preferred_element_type=F32)
        acc = cb_ref[...] + cw_ref[0:1, :] * taps[0:L]
        for k in range(1, SSD_CONV):
            acc = acc + cw_ref[k:k + 1, :] * taps[k * L:(k + 1) * L]
        act = _silu(acc)
        xs = act[:, :SSD_INNER]
        bm = act[:, SSD_INNER:SSD_INNER + SSD_GROUPS * SSD_STATE]
        cm = act[:, SSD_INNER + SSD_GROUPS * SSD_STATE:]

        for g in range(SSD_GROUPS):
            bm_g = bm[:, g * SSD_STATE:(g + 1) * SSD_STATE].astype(BF16)
            cm_g = cm[:, g * SSD_STATE:(g + 1) * SSD_STATE].astype(BF16)
            cb_g = lax.dot_general(cm_g, bm_g, (((1,), (1,)), ((), ())), preferred_element_type=F32)
            pairs_per_group = SSD_PAIRS // SSD_GROUPS
            for jj in range(pairs_per_group):
                j = g * pairs_per_group + jj
                h0, h1 = 2 * j, 2 * j + 1
                sl = slice(j * LANES, (j + 1) * LANES)
                col0 = jnp.broadcast_to(cs[:, h0:h0 + 1], (L, L))
                col1 = jnp.broadcast_to(cs[:, h1:h1 + 1], (L, L))
                dec0 = jnp.where(causal, jnp.exp(col0 - cs_t[h0:h0 + 1, :]), 0.0)
                dec1 = jnp.where(causal, jnp.exp(col1 - cs_t[h1:h1 + 1, :]), 0.0)
                m = jnp.concatenate([cb_g * dec0, cb_g * dec1], axis=1).astype(BF16)
                dt_p = jnp.where(lo, jnp.broadcast_to(dt[:, h0:h0 + 1], (L, L)),
                                 jnp.broadcast_to(dt[:, h1:h1 + 1], (L, L)))
                xs_p = xs[:, sl]
                xdt = xs_p * dt_p
                x2 = jnp.concatenate([jnp.where(lo, xdt, 0.0), jnp.where(lo, 0.0, xdt)], axis=0).astype(BF16)
                y_diag = jnp.dot(m, x2, preferred_element_type=F32)
                cs_p = jnp.where(lo, col0, col1)
                st = state_ref[j]
                y_off = jnp.exp(cs_p) * jnp.dot(cm_g, st.astype(BF16), preferred_element_type=F32)
                last = cs_p[L - 1:L, :]
                xdt_end = (xdt * jnp.exp(last - cs_p)).astype(BF16)
                new_st = lax.dot_general(bm_g, xdt_end, (((0,), (0,)), ((), ())), preferred_element_type=F32)
                state_ref[j] = st * jnp.exp(last) + new_st
                ybuf_ref[rows, sl] = y_diag + y_off + xs_p * dsk_ref[:, sl]

    y = ybuf_ref[...] * _silu(z_ref[...].astype(F32))
    gw = SSD_INNER // SSD_GROUPS
    parts = []
    for g in range(SSD_GROUPS):
        yg = y[:, g * gw:(g + 1) * gw]
        parts.append(yg * lax.rsqrt(jnp.mean(yg * yg, axis=-1, keepdims=True) + LN_EPS))
    yn = jnp.concatenate(parts, axis=1) * nw_ref[...]
    y_ref[...] = jnp.dot(yn.astype(BF16), wp_ref[...], preferred_element_type=F32).astype(BF16)


def _ssd_mixer(h, dt_raw, bsz, seq, conv_w, conv_b, dt_bias, a_log, d_skip, norm_w, w_proj):
    t = bsz * seq
    L = SSD_CHUNK
    ts = min(SSD_STEP_CHUNKS * L, seq)
    nc = seq // ts
    pad = LANES - SSD_HEADS
    dtb = jnp.pad(dt_bias, (0, pad))[None, :]
    alog = jnp.pad(a_log, (0, pad))[None, :]
    dsk = jnp.repeat(d_skip, SSD_HEAD_DIM)[None, :]
    r = jnp.arange(SSD_CONV * L, dtype=jnp.int32)
    shift = (jnp.arange(2 * L, dtype=jnp.int32)[None, :]
             == (L + r % L - (SSD_CONV - 1) + r // L)[:, None]).astype(BF16)
    const = lambda b, c: (0, 0)
    return pl.pallas_call(
        _ssd_kernel,
        out_shape=jax.ShapeDtypeStruct((t, D_MODEL), BF16),
        grid=(bsz, nc),
        in_specs=[
            pl.BlockSpec((ts, SSD_XBC), lambda b, c: (b * nc + c, COL_XBC // SSD_XBC)),
            pl.BlockSpec((ts, SSD_INNER), lambda b, c: (b * nc + c, COL_Z // SSD_INNER)),
            pl.BlockSpec((ts, LANES), lambda b, c: (b * nc + c, 0)),
            pl.BlockSpec((SSD_CONV * L, 2 * L), const),
            pl.BlockSpec((SSD_CONV, SSD_XBC), const),
            pl.BlockSpec((1, SSD_XBC), const),
            pl.BlockSpec((1, LANES), const),
            pl.BlockSpec((1, LANES), const),
            pl.BlockSpec((1, SSD_INNER), const),
            pl.BlockSpec((1, SSD_INNER), const),
            pl.BlockSpec((SSD_INNER, D_MODEL), const),
        ],
        out_specs=pl.BlockSpec((ts, D_MODEL), lambda b, c: (b * nc + c, 0)),
        scratch_shapes=[
            pltpu.VMEM((SSD_PAIRS, SSD_STATE, LANES), F32),
            pltpu.VMEM((ts + L, SSD_XBC), BF16),
            pltpu.VMEM((ts, SSD_INNER), F32),
        ],
        compiler_params=_params("parallel", "arbitrary"),
        name="ssd_mixer",
    )(h, h, dt_raw, shift, conv_w, conv_b[None, :], dtb, alog, dsk, norm_w[None, :], w_proj.astype(BF16))


CONF_TS = 256
CONF_HALO = 32
CONF_ROW_BLK = 64
CONF_COL_BLK = 256


def _conf_kernel(u_ref, shift_ref, dw_ref, db_ref, g_ref, b_ref, wp_ref, y_ref, ext_ref, conv_ref):
    ts = u_ref.shape[0]
    s = pl.program_id(1)
    span = ts + SUBLANES
    lead = CONF_HALO - SUBLANES

    @pl.when(s == 0)
    def _():
        ext_ref[:, 0:CONF_HALO, :] = jnp.zeros((SUBLANES, CONF_HALO, CONF_CH), F32)

    @pl.when(s > 0)
    def _():
        ext_ref[:, 0:CONF_HALO, :] = ext_ref[:, ts:ts + CONF_HALO, :]

    u = u_ref[...].astype(F32)
    glu = (u[:, :CONF_CH] * _sigmoid(u[:, CONF_CH:])).astype(BF16)
    ext_ref[0, CONF_HALO:CONF_HALO + ts, :] = glu.astype(F32)
    shifted = jnp.dot(shift_ref[...], glu, preferred_element_type=F32)
    for b in range(1, SUBLANES):
        blk = shifted[(b - 1) * span:b * span]
        ext_ref[b, lead:CONF_HALO, :] = ext_ref[b, lead:CONF_HALO, :] + blk[0:SUBLANES]
        ext_ref[b, CONF_HALO:CONF_HALO + ts, :] = blk[SUBLANES:span]

    for r0 in range(0, ts, CONF_ROW_BLK):
        for c0 in range(0, CONF_CH, CONF_COL_BLK):
            cols = slice(c0, c0 + CONF_COL_BLK)
            n_acc = CONF_ROW_BLK // SUBLANES
            acc = [jnp.broadcast_to(db_ref[:, cols], (SUBLANES, CONF_COL_BLK))] * n_acc
            for k in range(CONF_KERNEL):
                shift = CONF_HALO - (CONF_KERNEL - 1) + k
                off = shift - shift % SUBLANES + r0
                w = dw_ref[k, :, cols]
                acc = [a + w * ext_ref[shift % SUBLANES, off + SUBLANES * n:off + SUBLANES * (n + 1), cols]
                       for n, a in enumerate(acc)]
            conv_ref[r0:r0 + CONF_ROW_BLK, cols] = jnp.concatenate(acc, axis=0)

    hn = _silu(_layer_norm(conv_ref[...], g_ref[...], b_ref[...]))
    y_ref[...] = jnp.dot(hn.astype(BF16), wp_ref[...], preferred_element_type=F32).astype(BF16)


def _conformer(h, bsz, seq, dw_w, dw_b, ln_g, ln_b, w_proj):
    t = bsz * seq
    ts = min(CONF_TS, seq)
    nt = seq // ts
    const = lambda b, s: (0, 0)
    dw = jnp.broadcast_to(dw_w[:, None, :], (CONF_KERNEL, SUBLANES, CONF_CH))
    span = ts + SUBLANES
    r = jnp.arange((SUBLANES - 1) * span, dtype=jnp.int32)
    shift = (jnp.arange(ts, dtype=jnp.int32)[None, :] == (r % span - SUBLANES + 1 + r // span)[:, None]).astype(BF16)
    return pl.pallas_call(
        _conf_kernel,
        out_shape=jax.ShapeDtypeStruct((t, D_MODEL), BF16),
        grid=(bsz, nt),
        in_specs=[
            pl.BlockSpec((ts, 2 * CONF_CH), lambda b, s: (b * nt + s, COL_U // (2 * CONF_CH))),
            pl.BlockSpec(((SUBLANES - 1) * span, ts), const),
            pl.BlockSpec((CONF_KERNEL, SUBLANES, CONF_CH), lambda b, s: (0, 0, 0)),
            pl.BlockSpec((1, CONF_CH), const),
            pl.BlockSpec((1, CONF_CH), const),
            pl.BlockSpec((1, CONF_CH), const),
            pl.BlockSpec((CONF_CH, D_MODEL), const),
        ],
        out_specs=pl.BlockSpec((ts, D_MODEL), lambda b, s: (b * nt + s, 0)),
        scratch_shapes=[
            pltpu.VMEM((SUBLANES, CONF_HALO + ts, CONF_CH), F32),
            pltpu.VMEM((ts, CONF_CH), F32),
        ],
        compiler_params=_params("parallel", "arbitrary"),
        name="conformer_conv",
    )(h, shift, dw, dw_b[None, :], ln_g[None, :], ln_b[None, :], w_proj.astype(BF16))


def _pair_interleave(width):
    half = ATT_HEAD_DIM // 2
    j = jnp.arange(width, dtype=jnp.int32)
    block, lane = j // LANES, j % LANES
    chunk, within = lane // half, lane % half
    return block * LANES + (chunk % 2) * ATT_HEAD_DIM + (chunk // 2) * half + within


def _attn_lane_selectors():
    lane = jnp.arange(LANES, dtype=jnp.int32)
    swap = (lane + LANES // 2) % LANES
    first = (lane % ATT_HEAD_DIM) < ATT_HEAD_DIM // 2
    key_src = (jnp.where(first, lane, lane - ATT_HEAD_DIM // 2), jnp.where(first, lane + ATT_HEAD_DIM // 2, lane))
    val_src = (lane % ATT_HEAD_DIM, ATT_HEAD_DIM + lane % ATT_HEAD_DIM)
    sources = [swap, key_src[0], swap[key_src[0]], key_src[1], swap[key_src[1]], val_src[0], val_src[1]]
    return jnp.concatenate([(lane[:, None] == src[None, :]).astype(BF16) for src in sources], axis=1)


def _attn_kernel(q_ref, kc_ref, kp_ref, vc_ref, vp_ref, cosc_ref, sinc_ref, cosp_ref, sinp_ref, sink_ref,
                 sel_ref, wp_ref, y_ref, obuf_ref):
    blk = ATT_BLOCK
    n_sub = q_ref.shape[0] // blk
    i = pl.program_id(1)
    rep = ATT_Q_HEADS // ATT_KV_HEADS
    quarter = ATT_HEAD_DIM // 2
    scale = ATT_HEAD_DIM ** -0.5

    lane = lax.broadcasted_iota(jnp.int32, (blk, LANES), 1)
    first_q, lo_q = (lane % ATT_HEAD_DIM) < quarter, lane < ATT_HEAD_DIM

    cos_k = jnp.concatenate([cosp_ref[...], cosc_ref[...]], axis=0)
    sin_k = jnp.concatenate([sinp_ref[...], sinc_ref[...]], axis=0)
    k = jnp.concatenate([kp_ref[...], kc_ref[...]], axis=0)
    v = jnp.concatenate([vp_ref[...], vc_ref[...]], axis=0)

    qi = lax.broadcasted_iota(jnp.int32, (blk, 2 * blk), 0)
    kj = lax.broadcasted_iota(jnp.int32, (blk, 2 * blk), 1)
    mask_inner = (kj <= qi + blk) & (kj > qi)
    mask_first = (kj <= qi + blk) & (kj > jnp.where(i == 0, blk - 1, qi))

    keys = []
    for kh in range(ATT_KV_HEADS):
        block = slice((kh // 2) * LANES, (kh // 2 + 1) * LANES)
        sel = sel_ref[:, (1 + 2 * (kh % 2)) * LANES:(3 + 2 * (kh % 2)) * LANES]
        kd = jnp.dot(k[:, block], sel, preferred_element_type=F32)
        k_dup = kd[:, :LANES] * cos_k + kd[:, LANES:] * sin_k
        v_dup = jnp.dot(v[:, block], sel_ref[:, (5 + kh % 2) * LANES:(6 + kh % 2) * LANES],
                        preferred_element_type=F32)
        keys.append((k_dup.astype(BF16), jnp.concatenate([v_dup, jnp.ones_like(v_dup)], axis=1).astype(BF16)))

    def scores(kh, j):
        q_rows = slice(j * blk, (j + 1) * blk)
        cos_q, sin_q = cosc_ref[q_rows, :], sinc_ref[q_rows, :]
        stacked = []
        for qb in range(rep // 2):
            cols = slice((kh * (rep // 2) + qb) * LANES, (kh * (rep // 2) + qb + 1) * LANES)
            q_raw = q_ref[q_rows, cols]
            q_swap = jnp.dot(q_raw, sel_ref[:, 0:LANES], preferred_element_type=F32)
            qr = (q_raw.astype(F32) * cos_q + q_swap * sin_q) * scale
            stacked.append(jnp.where(first_q, qr, 0.0).astype(BF16))
            stacked.append(jnp.where(first_q, 0.0, qr).astype(BF16))
        qs = jnp.concatenate(stacked, axis=0)
        return lax.dot_general(qs, keys[kh][0][j * blk:(j + 2) * blk], (((1,), (1,)), ((), ())),
                               preferred_element_type=F32)

    def attend(kh, j, s_all):
        q_rows = slice(j * blk, (j + 1) * blk)
        mask = mask_first if j == 0 else mask_inner
        es, sink_terms = [], []
        for r in range(rep):
            hq = kh * rep + r
            s = jnp.where(mask, s_all[r * blk:(r + 1) * blk], -jnp.inf)
            sink = sink_ref[hq:hq + 1, :]
            mx = jnp.maximum(jnp.broadcast_to(jnp.max(s, axis=-1, keepdims=True), (blk, LANES)), sink)
            sink_terms.append(jnp.exp(sink - mx))
            es.append(jnp.exp(s - jnp.concatenate([mx, mx], axis=1)).astype(BF16))
        o = jnp.dot(jnp.concatenate(es, axis=0), keys[kh][1][j * blk:(j + 2) * blk], preferred_element_type=F32)
        outs = []
        for r in range(rep):
            rows = slice(r * blk, (r + 1) * blk)
            outs.append(o[rows, :LANES] * (1.0 / (o[rows, LANES:] + sink_terms[r])))
        for qb in range(rep // 2):
            cols = slice((kh * (rep // 2) + qb) * LANES, (kh * (rep // 2) + qb + 1) * LANES)
            obuf_ref[q_rows, cols] = jnp.where(lo_q, outs[2 * qb], outs[2 * qb + 1])

    pairs = [(kh, j) for kh in range(ATT_KV_HEADS) for j in range(n_sub)]
    s_next = scores(*pairs[0])
    for n, pair in enumerate(pairs):
        s_cur = s_next
        if n + 1 < len(pairs):
            s_next = scores(*pairs[n + 1])
        attend(*pair, s_cur)

    y_ref[...] = jnp.dot(obuf_ref[...].astype(BF16), wp_ref[...], preferred_element_type=F32).astype(BF16)


ATT_TQ = 4 * ATT_BLOCK


def _attention(h, cos_t, sin_t, bsz, seq, sinks, w_proj):
    t = bsz * seq
    blk = ATT_BLOCK
    tq = min(ATT_TQ, seq)
    n_sub = tq // blk
    nt = seq // tq
    nb = seq // blk
    qw = ATT_Q_HEADS * ATT_HEAD_DIM
    kw = ATT_KV_HEADS * ATT_HEAD_DIM
    cur = lambda b, i: b * nt + i
    prev = lambda b, i: b * nb + jnp.maximum(i * n_sub - 1, 0)
    const = lambda b, i: (0, 0)
    sink = jnp.broadcast_to(sinks[:, None], (ATT_Q_HEADS, LANES))
    return pl.pallas_call(
        _attn_kernel,
        out_shape=jax.ShapeDtypeStruct((t, D_MODEL), BF16),
        grid=(bsz, nt),
        in_specs=[
            pl.BlockSpec((tq, qw), lambda b, i: (cur(b, i), COL_Q // qw)),
            pl.BlockSpec((tq, kw), lambda b, i: (cur(b, i), COL_K // kw)),
            pl.BlockSpec((blk, kw), lambda b, i: (prev(b, i), COL_K // kw)),
            pl.BlockSpec((tq, kw), lambda b, i: (cur(b, i), COL_V // kw)),
            pl.BlockSpec((blk, kw), lambda b, i: (prev(b, i), COL_V // kw)),
            pl.BlockSpec((tq, LANES), lambda b, i: (cur(b, i), 0)),
            pl.BlockSpec((tq, LANES), lambda b, i: (cur(b, i), 0)),
            pl.BlockSpec((blk, LANES), lambda b, i: (prev(b, i), 0)),
            pl.BlockSpec((blk, LANES), lambda b, i: (prev(b, i), 0)),
            pl.BlockSpec((ATT_Q_HEADS, LANES), const),
            pl.BlockSpec((LANES, 7 * LANES), const),
            pl.BlockSpec((qw, D_MODEL), const),
        ],
        out_specs=pl.BlockSpec((tq, D_MODEL), lambda b, i: (cur(b, i), 0)),
        scratch_shapes=[pltpu.VMEM((tq, qw), F32)],
        compiler_params=_params("parallel", "arbitrary"),
        name="swa_attention",
    )(h, h, h, h, h, cos_t, sin_t, cos_t, sin_t, sink, _attn_lane_selectors(), w_proj.astype(BF16))


MERGE_TM = 512


def _merge_kernel(g0_ref, g1_ref, g2_ref, bg_ref, y0_ref, y1_ref, y2_ref, x_ref, wo_ref, lg_ref, lb_ref,
                  wrh_ref, wrl_ref, br_ref, x1_ref, logit_ref):
    mixed = None
    for n, (g_ref, y_ref) in enumerate(((g0_ref, y0_ref), (g1_ref, y1_ref), (g2_ref, y2_ref))):
        term = _sigmoid(g_ref[...].astype(F32) + bg_ref[n:n + 1, :]) * y_ref[...].astype(F32)
        mixed = term if mixed is None else mixed + term
    r = DEEPNORM_ALPHA * x_ref[...] + jnp.dot(mixed.astype(BF16), wo_ref[...], preferred_element_type=F32)
    x1 = _layer_norm(r, lg_ref[...], lb_ref[...])
    _store_row_tiles(x1_ref, x1)
    x_hi = x1.astype(BF16)
    x_lo = (x1 - x_hi.astype(F32)).astype(BF16)
    logit_ref[...] = (jnp.dot(x_hi, wrh_ref[...], preferred_element_type=F32)
                      + (jnp.dot(x_hi, wrl_ref[...], preferred_element_type=F32)
                         + jnp.dot(x_lo, wrh_ref[...], preferred_element_type=F32))) + br_ref[...]


def _merge(h, b_gate, y_ssd, y_conf, y_att, x2d, w_out, ln_g, ln_b, w_router, b_router):
    t = x2d.shape[0]
    tm = min(MERGE_TM, t)
    const = lambda i: (0, 0)
    row = lambda i: (i, 0)
    w_router_hi = w_router.astype(BF16)
    w_router_lo = (w_router - w_router_hi.astype(F32)).astype(BF16)
    return pl.pallas_call(
        _merge_kernel,
        out_shape=(jax.ShapeDtypeStruct((t * ROW_TILES, LANES), F32), jax.ShapeDtypeStruct((t, LANES), F32)),
        grid=(t // tm,),
        in_specs=[
            pl.BlockSpec((tm, D_MODEL), lambda i: (i, 0)),
            pl.BlockSpec((tm, D_MODEL), lambda i: (i, 1)),
            pl.BlockSpec((tm, D_MODEL), lambda i: (i, 2)),
            pl.BlockSpec((3, D_MODEL), const),
            pl.BlockSpec((tm, D_MODEL), row),
            pl.BlockSpec((tm, D_MODEL), row),
            pl.BlockSpec((tm, D_MODEL), row),
            pl.BlockSpec((tm, D_MODEL), row),
            pl.BlockSpec((D_MODEL, D_MODEL), const),
            pl.BlockSpec((1, D_MODEL), const),
            pl.BlockSpec((1, D_MODEL), const),
            pl.BlockSpec((D_MODEL, LANES), const),
            pl.BlockSpec((D_MODEL, LANES), const),
            pl.BlockSpec((1, LANES), const),
        ],
        out_specs=(pl.BlockSpec((tm * ROW_TILES, LANES), row), pl.BlockSpec((tm, LANES), row)),
        compiler_params=_params("parallel"),
        name="merge_ln_router",
    )(h, h, h, b_gate, y_ssd, y_conf, y_att, x2d, w_out.astype(BF16), ln_g[None, :], ln_b[None, :],
      w_router_hi, w_router_lo, b_router)


GATHER_UNROLL = 8


def _on_slot(slot, fn, n_slots=2):
    for s in range(n_slots):
        @pl.when(slot == s)
        def _(s=s):
            fn(s)


def _gather_pipeline(i, n_steps, idx_hbm, src_hbm, idx_smem, buf, isem, gsem):
    n_rows = buf[0].shape[0] // ROW_TILES
    slot = i % 2

    def idx_copy(step, s):
        return pltpu.make_async_copy(idx_hbm.at[step], idx_smem[s], isem.at[s])

    def issue_rows(s):
        def body(it, carry):
            for u in range(GATHER_UNROLL):
                r = it * GATHER_UNROLL + u
                src_row = pl.multiple_of(idx_smem[s][r] * ROW_TILES, ROW_TILES)
                dst_row = pl.multiple_of(r * ROW_TILES, ROW_TILES)
                pltpu.make_async_copy(src_hbm.at[pl.ds(src_row, ROW_TILES)],
                                      buf[s].at[pl.ds(dst_row, ROW_TILES)], gsem.at[s]).start(priority=u % 2)
            return carry

        lax.fori_loop(0, n_rows // GATHER_UNROLL, body, 0)

    @pl.when(i == 0)
    def _():
        idx_copy(0, 0).start()
        idx_copy(0, 0).wait()
        issue_rows(0)

        @pl.when(n_steps > 1)
        def _():
            idx_copy(1, 1).start()

    @pl.when(i + 1 < n_steps)
    def _():
        def prefetch(s):
            idx_copy(i + 1, 1 - s).wait()
            issue_rows(1 - s)

            @pl.when(i + 2 < n_steps)
            def _():
                idx_copy(i + 2, s).start()

        _on_slot(slot, prefetch)


def _gather_wait(s, src_hbm, buf, gsem):
    pltpu.make_async_copy(src_hbm.at[pl.ds(0, buf[s].shape[0])], buf[s], gsem.at[s]).wait()


DISPATCH_SLOTS = 3


def _dispatch_kernel(pad_start_ref, pad_len_ref, idx_hbm, x1_hbm, rows_hbm, idx0, idx1, idx2, st0, st1, st2,
                     zero_ref, isem, xsem, dsem, zsem):
    i = pl.program_id(0)
    n_steps = pl.num_programs(0)
    idx = (idx0, idx1, idx2)
    stage = (st0, st1, st2)
    n_idx = idx0.shape[0]
    tm = n_idx // MOE_TOP_K

    def loads(step, s):
        first = pl.multiple_of(step * (tm * ROW_TILES), tm * ROW_TILES)
        return (pltpu.make_async_copy(idx_hbm.at[step], idx[s], isem.at[s]),
                pltpu.make_async_copy(x1_hbm.at[pl.ds(first, tm * ROW_TILES)], stage[s], xsem.at[s]))

    def start_loads(step, s):
        for c in loads(step, s):
            c.start()

    def wait_rows(s):
        pltpu.make_async_copy(x1_hbm.at[pl.ds(0, n_idx * ROW_TILES)], rows_hbm.at[pl.ds(0, n_idx * ROW_TILES)],
                              dsem.at[s]).wait()

    def zero_copy(row, n_rows):
        dst_row = pl.multiple_of(row * ROW_TILES, ROW_TILES)
        if n_rows is None:
            return pltpu.make_async_copy(zero_ref, rows_hbm.at[pl.ds(dst_row, ROW_TILES)], zsem.at[0])
        return pltpu.make_async_copy(x1_hbm.at[pl.ds(0, n_rows * ROW_TILES)],
                                     rows_hbm.at[pl.ds(dst_row, n_rows * ROW_TILES)], zsem.at[0])

    @pl.when(i == 0)
    def _():
        start_loads(0, 0)

        @pl.when(n_steps > 1)
        def _():
            start_loads(1, 1)

        zero_ref[...] = jnp.zeros(zero_ref.shape, F32)

        def zero_expert(e, carry):
            def zero_row(k, c):
                zero_copy(pad_start_ref[e] + k, None).start()
                return c

            return lax.fori_loop(0, pad_len_ref[e], zero_row, carry)

        lax.fori_loop(0, pad_len_ref.shape[0], zero_expert, 0)

    @pl.when(i == n_steps - 1)
    def _():
        def wait_expert(e, carry):
            @pl.when(pad_len_ref[e] > 0)
            def _():
                zero_copy(pad_start_ref[e], pad_len_ref[e]).wait()

            return carry

        lax.fori_loop(0, pad_len_ref.shape[0], wait_expert, 0)

    def step(s):
        freed = (s + DISPATCH_SLOTS - 1) % DISPATCH_SLOTS
        for c in loads(i, s):
            c.wait()

        def body(it, carry):
            for u in range(GATHER_UNROLL):
                r = it * GATHER_UNROLL + u
                src_row = pl.multiple_of((r % tm) * ROW_TILES, ROW_TILES)
                dst_row = pl.multiple_of(idx[s][r] * ROW_TILES, ROW_TILES)
                pltpu.make_async_copy(stage[s].at[pl.ds(src_row, ROW_TILES)],
                                      rows_hbm.at[pl.ds(dst_row, ROW_TILES)], dsem.at[s]).start(priority=u % 2)
            return carry

        lax.fori_loop(0, n_idx // GATHER_UNROLL, body, 0)

        @pl.when(i > 0)
        def _():
            wait_rows(freed)

        @pl.when(i + 2 < n_steps)
        def _():
            start_loads(i + 2, freed)

        @pl.when(i == n_steps - 1)
        def _():
            wait_rows(s)

    _on_slot(i % DISPATCH_SLOTS, step, DISPATCH_SLOTS)


def _moe_dispatch(x1, dest_blocks, pad_start, pad_len, n_rows):
    n_steps, n_idx = dest_blocks.shape
    any_spec = pl.BlockSpec(memory_space=pl.ANY)
    grid_spec = pltpu.PrefetchScalarGridSpec(
        num_scalar_prefetch=2,
        grid=(n_steps,),
        in_specs=[any_spec, any_spec],
        out_specs=any_spec,
        scratch_shapes=(
            [pltpu.SMEM((n_idx,), jnp.int32)] * DISPATCH_SLOTS
            + [pltpu.VMEM((n_idx // MOE_TOP_K * ROW_TILES, LANES), F32)] * DISPATCH_SLOTS
            + [pltpu.VMEM((ROW_TILES, LANES), F32)]
            + [pltpu.SemaphoreType.DMA((DISPATCH_SLOTS,))] * 3
            + [pltpu.SemaphoreType.DMA((1,))]
        ),
    )
    return pl.pallas_call(
        _dispatch_kernel,
        out_shape=jax.ShapeDtypeStruct((n_rows * ROW_TILES, LANES), F32),
        grid_spec=grid_spec,
        compiler_params=_params("arbitrary"),
        name="moe_dispatch",
    )(pad_start, pad_len, dest_blocks, x1)


def _moe_kernel(be_ref, nu_ref, x_ref, wg_ref, wu_ref, wd_ref, y_ref, wgb_ref, wub_ref, wdb_ref):
    i = pl.program_id(0)

    @pl.when(i < nu_ref[0])
    def _():
        @pl.when((i == 0) | (be_ref[i] != be_ref[jnp.maximum(i - 1, 0)]))
        def _():
            wgb_ref[...] = wg_ref[...].astype(BF16)
            wub_ref[...] = wu_ref[...].astype(BF16)
            wdb_ref[...] = wd_ref[...].astype(BF16)

        xb = _load_row_tiles(x_ref, 0, MOE_BLOCK).astype(BF16)
        gate = jnp.dot(xb, wgb_ref[...], preferred_element_type=F32)
        up = jnp.dot(xb, wub_ref[...], preferred_element_type=F32)
        hmid = (_silu(gate) * up).astype(BF16)
        _store_row_tiles(y_ref, jnp.dot(hmid, wdb_ref[...], preferred_element_type=F32))

    @pl.when(i >= nu_ref[0])
    def _():
        y_ref[...] = jnp.zeros(y_ref.shape, F32)


def _moe_experts(x_rows, block_e, n_used, w_gate, w_up, w_down):
    n_blocks = block_e.shape[0]
    grid_spec = pltpu.PrefetchScalarGridSpec(
        num_scalar_prefetch=2,
        grid=(n_blocks,),
        in_specs=[
            pl.BlockSpec((MOE_BLOCK * ROW_TILES, LANES), lambda i, be, nu: (jnp.minimum(i, nu[0] - 1), 0)),
            pl.BlockSpec((None, D_MODEL, MOE_FF), lambda i, be, nu: (be[i], 0, 0)),
            pl.BlockSpec((None, D_MODEL, MOE_FF), lambda i, be, nu: (be[i], 0, 0)),
            pl.BlockSpec((None, MOE_FF, D_MODEL), lambda i, be, nu: (be[i], 0, 0)),
        ],
        out_specs=pl.BlockSpec((MOE_BLOCK * ROW_TILES, LANES), lambda i, be, nu: (i, 0)),
        scratch_shapes=[
            pltpu.VMEM((D_MODEL, MOE_FF), BF16),
            pltpu.VMEM((D_MODEL, MOE_FF), BF16),
            pltpu.VMEM((MOE_FF, D_MODEL), BF16),
        ],
    )
    return pl.pallas_call(
        _moe_kernel,
        out_shape=jax.ShapeDtypeStruct((n_blocks * MOE_BLOCK * ROW_TILES, LANES), F32),
        grid_spec=grid_spec,
        compiler_params=_params("arbitrary"),
        name="moe_experts",
    )(block_e, n_used, x_rows, w_gate, w_up, w_down)


COMB_TM = 256


def _combine_kernel(idx_hbm, y_hbm, ew_ref, x1_ref, p_ref, lg_ref, lb_ref, wpg_ref, wpp_ref, out_ref,
                    idx0, idx1, ybuf0, ybuf1, ffn_ref, isem, gsem):
    i = pl.program_id(0)
    tm = out_ref.shape[0]
    ybuf = (ybuf0, ybuf1)
    _gather_pipeline(i, pl.num_programs(0), idx_hbm, y_hbm, (idx0, idx1), ybuf, isem, gsem)

    def weighted_sum(s):
        _gather_wait(s, y_hbm, ybuf, gsem)
        ew = ew_ref[...]
        ffn_ref[...] = (_load_row_tiles(ybuf[s], 0, tm) * ew[:, 0:1]
                        + _load_row_tiles(ybuf[s], tm, tm) * ew[:, 1:2])

    _on_slot(i % 2, weighted_sum)
    x1 = _load_row_tiles(x1_ref, 0, tm)
    x2 = _layer_norm(DEEPNORM_ALPHA * x1 + ffn_ref[...], lg_ref[...], lb_ref[...])
    gate = _sigmoid(jnp.dot(x2.astype(BF16), wpg_ref[...], preferred_element_type=F32))
    emb = jnp.dot(p_ref[...].astype(BF16), wpp_ref[...], preferred_element_type=F32)
    out_ref[...] = x2 + gate * emb


def _combine(y_rows, dest_blocks, e_w, x1, p2d, ln_g, ln_b, ple_w_gate, ple_w_proj):
    t = x1.shape[0] // ROW_TILES
    tm = dest_blocks.shape[1] // MOE_TOP_K
    const = lambda i: (0, 0)
    row = lambda i: (i, 0)
    return pl.pallas_call(
        _combine_kernel,
        out_shape=jax.ShapeDtypeStruct((t, D_MODEL), F32),
        grid=(t // tm,),
        in_specs=[
            pl.BlockSpec(memory_space=pl.ANY),
            pl.BlockSpec(memory_space=pl.ANY),
            pl.BlockSpec((tm, MOE_TOP_K), row),
            pl.BlockSpec((tm * ROW_TILES, LANES), row),
            pl.BlockSpec((tm, PLE_DIM), row),
            pl.BlockSpec((1, D_MODEL), const),
            pl.BlockSpec((1, D_MODEL), const),
            pl.BlockSpec((D_MODEL, D_MODEL), const),
            pl.BlockSpec((PLE_DIM, D_MODEL), const),
        ],
        out_specs=pl.BlockSpec((tm, D_MODEL), row),
        scratch_shapes=[
            pltpu.SMEM((MOE_TOP_K * tm,), jnp.int32),
            pltpu.SMEM((MOE_TOP_K * tm,), jnp.int32),
            pltpu.VMEM((MOE_TOP_K * tm * ROW_TILES, LANES), F32),
            pltpu.VMEM((MOE_TOP_K * tm * ROW_TILES, LANES), F32),
            pltpu.VMEM((tm, D_MODEL), F32),
            pltpu.SemaphoreType.DMA((2,)),
            pltpu.SemaphoreType.DMA((2,)),
        ],
        compiler_params=_params("arbitrary"),
        name="moe_combine",
    )(dest_blocks, y_rows, e_w, x1, p2d, ln_g[None, :], ln_b[None, :], ple_w_gate.astype(BF16),
      ple_w_proj.astype(BF16))


ROUTE_CHUNK = 256


def _route(logits):
    t = logits.shape[0]
    g_logits = logits[:, :MOE_GROUPS]
    e_logits = logits[:, MOE_GROUPS:MOE_GROUPS + MOE_EXPERTS]
    g_idx = jnp.argmax(g_logits, axis=-1).astype(jnp.int32)
    g_w = 1.0 / jnp.sum(jnp.exp(g_logits - jnp.max(g_logits, axis=-1, keepdims=True)), axis=-1)

    lane = jnp.arange(MOE_EXPERTS, dtype=jnp.int32)[None, :]
    masked = jnp.where(lane // MOE_EXPERTS_PER_GROUP == g_idx[:, None], e_logits, -jnp.inf)
    e1 = jnp.argmax(masked, axis=-1).astype(jnp.int32)
    sel1 = lane == e1[:, None]
    rest = jnp.where(sel1, -jnp.inf, masked)
    e2 = jnp.argmax(rest, axis=-1).astype(jnp.int32)
    sel2 = lane == e2[:, None]
    r = jnp.exp(jnp.max(rest, axis=-1) - jnp.max(masked, axis=-1))
    e_w = jnp.stack([1.0 / (1.0 + r), r / (1.0 + r)], axis=-1) * g_w[:, None]

    chunk = min(ROUTE_CHUNK, t)
    hot = (sel1 | sel2).reshape(t // chunk, chunk, MOE_EXPERTS)
    tril = jnp.tril(jnp.ones((chunk, chunk), BF16))
    within = jnp.einsum("ij,cje->cie", tril, hot.astype(BF16), preferred_element_type=F32).astype(jnp.int32)
    chunk_counts = within[:, -1, :]
    chunk_ends = jnp.cumsum(chunk_counts, axis=0)
    counts = chunk_ends[-1]
    padded = ((counts + MOE_BLOCK - 1) // MOE_BLOCK) * MOE_BLOCK
    pends = jnp.cumsum(padded)
    pstarts = pends - padded
    row = (within + (chunk_ends - chunk_counts + pstarts[None, :] - 1)[:, None, :]).reshape(t, MOE_EXPERTS)
    dest = (jnp.sum(jnp.where(sel1, row, 0), axis=-1), jnp.sum(jnp.where(sel2, row, 0), axis=-1))

    n_rows = t * MOE_TOP_K + MOE_EXPERTS * MOE_BLOCK
    n_blocks = n_rows // MOE_BLOCK
    block_start = jnp.arange(n_blocks, dtype=jnp.int32) * MOE_BLOCK
    block_e = jnp.sum((pends[None, :] <= block_start[:, None]).astype(jnp.int32), axis=1)
    block_e = jnp.minimum(block_e, MOE_EXPERTS - 1).astype(jnp.int32)
    n_used = (pends[-1] // MOE_BLOCK).astype(jnp.int32).reshape(1)
    pad = (jnp.concatenate([pstarts + counts, pends[-1:]]),
           jnp.concatenate([padded - counts, n_rows - pends[-1:]]))
    return e_w, dest, block_e, n_used, pad, n_rows


def _layer(x2d, p2d, cos_t, sin_t, bsz, seq, w_in, b_gate, ssd_conv_w, ssd_conv_b, ssd_dt_bias, ssd_a_log,
           ssd_d, ssd_norm_w, ssd_w_out, conf_dw_w, conf_dw_b, conf_ln_g, conf_ln_b, conf_w_out, attn_sinks,
           attn_w_out, w_out, ln1_g, ln1_b, moe_w_group, moe_b_group, moe_w_expert, moe_b_expert, moe_w_gate,
           moe_w_up, moe_w_down, ln2_g, ln2_b, ple_w_gate, ple_w_proj):
    t = bsz * seq
    w_main = jnp.concatenate([w_in[:, :DT_COL_ORIG], w_in[:, DT_COL_ORIG + SSD_HEADS:]], axis=1)
    w_main = jnp.concatenate([w_main[:, :COL_Q], w_main[:, COL_Q:COL_V][:, _pair_interleave(COL_V - COL_Q)],
                              w_main[:, COL_V:]], axis=1).astype(BF16)
    w_dt = jnp.pad(w_in[:, DT_COL_ORIG:DT_COL_ORIG + SSD_HEADS], ((0, 0), (0, LANES - SSD_HEADS))).astype(BF16)
    h, dt_raw = _in_projection(x2d, w_main, w_dt)

    y_ssd = _ssd_mixer(h, dt_raw, bsz, seq, ssd_conv_w, ssd_conv_b, ssd_dt_bias, ssd_a_log, ssd_d, ssd_norm_w,
                       ssd_w_out)
    y_conf = _conformer(h, bsz, seq, conf_dw_w, conf_dw_b, conf_ln_g, conf_ln_b, conf_w_out)
    y_att = _attention(h, cos_t, sin_t, bsz, seq, attn_sinks, attn_w_out)

    n_router = MOE_GROUPS + MOE_EXPERTS
    w_router = jnp.pad(jnp.concatenate([moe_w_group, moe_w_expert], axis=1), ((0, 0), (0, LANES - n_router)))
    b_router = jnp.pad(jnp.concatenate([moe_b_group, moe_b_expert]), (0, LANES - n_router))[None, :]
    x1, logits = _merge(h, b_gate, y_ssd, y_conf, y_att, x2d, w_out, ln1_g, ln1_b, w_router, b_router)

    e_w, dest, block_e, n_used, (pad_start, pad_len), n_rows = _route(logits)
    tm = min(COMB_TM, t)
    dest_blocks = jnp.concatenate([d.reshape(t // tm, tm) for d in dest], axis=1)
    x_rows = _moe_dispatch(x1, dest_blocks, pad_start, pad_len, n_rows)
    y_rows = _moe_experts(x_rows, block_e, n_used, moe_w_gate, moe_w_up, moe_w_down)
    return _combine(y_rows, dest_blocks, e_w, x1, p2d, ln2_g, ln2_b, ple_w_gate, ple_w_proj)


def kernel(x, p, positions, w_in, b_gate, ssd_conv_w, ssd_conv_b, ssd_dt_bias, ssd_a_log, ssd_d, ssd_norm_w, ssd_w_out, conf_dw_w, conf_dw_b, conf_ln_g, conf_ln_b, conf_w_out, attn_sinks, attn_w_out, w_out, ln1_g, ln1_b, moe_w_group, moe_b_group, moe_w_expert, moe_b_expert, moe_w_gate, moe_w_up, moe_w_down, ln2_g, ln2_b, ple_w_gate, ple_w_proj):
    bsz, seq, d = x.shape
    t = bsz * seq
    cos_t, sin_t = _rope_tables(positions)
    x2d = x.reshape(t, d)
    per_layer = (w_in, b_gate, ssd_conv_w, ssd_conv_b, ssd_dt_bias, ssd_a_log, ssd_d, ssd_norm_w, ssd_w_out,
                 conf_dw_w, conf_dw_b, conf_ln_g, conf_ln_b, conf_w_out, attn_sinks, attn_w_out, w_out, ln1_g,
                 ln1_b, moe_w_group, moe_b_group, moe_w_expert, moe_b_expert, moe_w_gate, moe_w_up, moe_w_down,
                 ln2_g, ln2_b, ple_w_gate, ple_w_proj)
    for layer in range(w_in.shape[0]):
        x2d = _layer(x2d, p[layer].reshape(t, -1), cos_t, sin_t, bsz, seq, *(w[layer] for w in per_layer))
    return x2d.reshape(bsz, seq, d)
```

```python
import functools

import jax
import jax.numpy as jnp
from jax import lax
from jax.experimental import pallas as pl
from jax.experimental.pallas import tpu as pltpu

F32 = jnp.float32
BF16 = jnp.bfloat16

D_MODEL = 1024
N_LAYERS = 2
PLE_DIM = 256
SSD_HEADS = 16
SSD_HEAD_DIM = 64
SSD_INNER = SSD_HEADS * SSD_HEAD_DIM
SSD_GROUPS = 4
SSD_STATE = 128
SSD_CONV = 4
SSD_CHUNK = 128
SSD_XBC = SSD_INNER + 2 * SSD_GROUPS * SSD_STATE
CONF_CH = 1024
CONF_KERNEL = 31
ATT_Q_HEADS = 16
ATT_KV_HEADS = 4
ATT_HEAD_DIM = 64
ATT_BLOCK = 128
ROPE_THETA = 10000.0
MOE_GROUPS = 4
MOE_EXPERTS_PER_GROUP = 8
MOE_EXPERTS = MOE_GROUPS * MOE_EXPERTS_PER_GROUP
MOE_TOP_K = 2
MOE_FF = 512
MOE_BLOCK = 256
DEEPNORM_ALPHA = (2 * N_LAYERS) ** 0.25
LN_EPS = 1e-5

LANES = 128
SUBLANES = 8
VMEM_LIMIT_BYTES = 56 * 1024 * 1024

COL_GATES = 0
COL_Z = 3 * D_MODEL
COL_XBC = COL_Z + SSD_INNER
COL_U = COL_XBC + SSD_XBC
COL_Q = COL_U + 2 * CONF_CH
COL_K = COL_Q + ATT_Q_HEADS * ATT_HEAD_DIM
COL_V = COL_K + ATT_KV_HEADS * ATT_HEAD_DIM
H_WIDTH = COL_V + ATT_KV_HEADS * ATT_HEAD_DIM
DT_COL_ORIG = 3 * D_MODEL + SSD_INNER + SSD_XBC


def _params(*semantics):
    return pltpu.CompilerParams(dimension_semantics=semantics, vmem_limit_bytes=VMEM_LIMIT_BYTES)


def _sigmoid(x):
    return 0.5 * (jnp.tanh(0.5 * x) + 1.0)


def _silu(x):
    half = 0.5 * x
    return half * (jnp.tanh(half) + 1.0)


def _layer_norm(x, g, b):
    mu = jnp.mean(x, axis=-1, keepdims=True)
    xc = x - mu
    var = jnp.mean(xc * xc, axis=-1, keepdims=True)
    return xc * lax.rsqrt(var + LN_EPS) * g + b


ROW_TILES = D_MODEL // LANES


def _store_row_tiles(ref, x):
    rows = x.shape[0]
    for k in range(ROW_TILES):
        ref[pl.ds(k, rows, stride=ROW_TILES), :] = x[:, k * LANES:(k + 1) * LANES]


def _load_row_tiles(ref, first_row, rows):
    return jnp.concatenate(
        [ref[pl.ds(first_row * ROW_TILES + k, rows, stride=ROW_TILES), :] for k in range(ROW_TILES)], axis=1)


IN_TM = 512
IN_TN = 512


def _inproj_kernel(x_ref, w_ref, wdt_ref, h_ref, dt_ref):
    xb = x_ref[...].astype(BF16)
    dt_ref[...] = jnp.dot(xb, wdt_ref[...], preferred_element_type=F32)
    for j in range(H_WIDTH // IN_TN):
        cols = slice(j * IN_TN, (j + 1) * IN_TN)
        acc = jnp.dot(xb, w_ref[:, cols], preferred_element_type=F32)
        if (j + 1) * IN_TN <= COL_Z:
            acc = 0.5 * acc
        h_ref[:, cols] = acc.astype(BF16)


def _in_projection(x2d, w_main, w_dt):
    t = x2d.shape[0]
    tm = min(IN_TM, t)
    resident = pl.Buffered(1)
    return pl.pallas_call(
        _inproj_kernel,
        out_shape=(jax.ShapeDtypeStruct((t, H_WIDTH), BF16), jax.ShapeDtypeStruct((t, LANES), F32)),
        grid=(t // tm,),
        in_specs=[
            pl.BlockSpec((tm, D_MODEL), lambda i: (i, 0)),
            pl.BlockSpec((D_MODEL, H_WIDTH), lambda i: (0, 0), pipeline_mode=resident),
            pl.BlockSpec((D_MODEL, LANES), lambda i: (0, 0), pipeline_mode=resident),
        ],
        out_specs=(
            pl.BlockSpec((tm, H_WIDTH), lambda i: (i, 0)),
            pl.BlockSpec((tm, LANES), lambda i: (i, 0)),
        ),
        compiler_params=_params("parallel"),
        name="in_projection",
    )(x2d, w_main, w_dt)


def _rope_kernel(pos_ref, inv_ref, sign_ref, cos_ref, sin_ref):
    ang = pos_ref[...].astype(F32) * inv_ref[...]
    cos_ref[...] = jnp.cos(ang)
    sin_ref[...] = jnp.sin(ang) * sign_ref[...]


def _rope_tables(positions):
    t = positions.size
    tm = min(1024, t)
    half = ATT_HEAD_DIM // 2
    inv_freq = ROPE_THETA ** (-jnp.arange(half, dtype=F32) / half)
    inv = jnp.tile(inv_freq, LANES // half)[None, :]
    sign = jnp.concatenate([-jnp.ones((LANES // 2,), F32), jnp.ones((LANES // 2,), F32)])[None, :]
    return pl.pallas_call(
        _rope_kernel,
        out_shape=(jax.ShapeDtypeStruct((t, LANES), F32), jax.ShapeDtypeStruct((t, LANES), F32)),
        grid=(t // tm,),
        in_specs=[
            pl.BlockSpec((tm, 1), lambda i: (i, 0)),
            pl.BlockSpec((1, LANES), lambda i: (0, 0)),
            pl.BlockSpec((1, LANES), lambda i: (0, 0)),
        ],
        out_specs=(pl.BlockSpec((tm, LANES), lambda i: (i, 0)), pl.BlockSpec((tm, LANES), lambda i: (i, 0))),
        compiler_params=_params("parallel"),
        name="rope_tables",
    )(positions.reshape(t, 1), inv, sign)


SSD_PAIRS = SSD_HEADS // 2
SSD_STEP_CHUNKS = 4


def _ssd_kernel(xbc_ref, z_ref, dt_ref, shift_ref, cw_ref, cb_ref, dtb_ref, alog_ref, dsk_ref, nw_ref, wp_ref,
                y_ref, state_ref, ext_ref, ybuf_ref):
    L = SSD_CHUNK
    n_sub = xbc_ref.shape[0] // L
    c = pl.program_id(1)

    @pl.when(c == 0)
    def _():
        state_ref[...] = jnp.zeros(state_ref.shape, F32)
        ext_ref[0:L, :] = jnp.zeros((L, SSD_XBC), BF16)

    @pl.when(c > 0)
    def _():
        ext_ref[0:L, :] = ext_ref[n_sub * L:(n_sub + 1) * L, :]

    ext_ref[L:(n_sub + 1) * L, :] = xbc_ref[...]

    row = lax.broadcasted_iota(jnp.int32, (L, L), 0)
    col = lax.broadcasted_iota(jnp.int32, (L, L), 1)
    causal = row >= col
    tril = causal.astype(F32)
    lo = col < SSD_HEAD_DIM
    a = -jnp.exp(alog_ref[...])

    steps = []
    for sub in range(n_sub):
        x_dt = dt_ref[sub * L:(sub + 1) * L, :] + dtb_ref[...]
        dt = jnp.maximum(x_dt, 0.0) + jnp.log1p(jnp.exp(-jnp.abs(x_dt)))
        cs = jnp.dot(tril, dt * a, preferred_element_type=F32, precision=lax.Precision.HIGHEST)
        steps.append((dt, cs, cs.T))

    for sub in range(n_sub):
        rows = slice(sub * L, (sub + 1) * L)
        dt, cs, cs_t = steps[sub]

        taps = jnp.dot(shift_ref[...], ext_ref[sub * L:(sub + 2) * L, :], preferred_element_type=F32)
        acc = cb_ref[...] + cw_ref[0:1, :] * taps[0:L]
        for k in range(1, SSD_CONV):
            acc = acc + cw_ref[k:k + 1, :] * taps[k * L:(k + 1) * L]
        act = _silu(acc)
        xs = act[:, :SSD_INNER]
        bm = act[:, SSD_INNER:SSD_INNER + SSD_GROUPS * SSD_STATE]
        cm = act[:, SSD_INNER + SSD_GROUPS * SSD_STATE:]

        for g in range(SSD_GROUPS):
            bm_g = bm[:, g * SSD_STATE:(g + 1) * SSD_STATE].astype(BF16)
            cm_g = cm[:, g * SSD_STATE:(g + 1) * SSD_STATE].astype(BF16)
            cb_g = lax.dot_general(cm_g, bm_g, (((1,), (1,)), ((), ())), preferred_element_type=F32)
            pairs_per_group = SSD_PAIRS // SSD_GROUPS
            for jj in range(pairs_per_group):
                j = g * pairs_per_group + jj
                h0, h1 = 2 * j, 2 * j + 1
                sl = slice(j * LANES, (j + 1) * LANES)
                col0 = jnp.broadcast_to(cs[:, h0:h0 + 1], (L, L))
                col1 = jnp.broadcast_to(cs[:, h1:h1 + 1], (L, L))
                dec0 = jnp.where(causal, jnp.exp(col0 - cs_t[h0:h0 + 1, :]), 0.0)
                dec1 = jnp.where(causal, jnp.exp(col1 - cs_t[h1:h1 + 1, :]), 0.0)
                m = jnp.concatenate([cb_g * dec0, cb_g * dec1], axis=1).astype(BF16)
                dt_p = jnp.where(lo, jnp.broadcast_to(dt[:, h0:h0 + 1], (L, L)),
                                 jnp.broadcast_to(dt[:, h1:h1 + 1], (L, L)))
                xs_p = xs[:, sl]
                xdt = xs_p * dt_p
                x2 = jnp.concatenate([jnp.where(lo, xdt, 0.0), jnp.where(lo, 0.0, xdt)], axis=0).astype(BF16)
                y_diag = jnp.dot(m, x2, preferred_element_type=F32)
                cs_p = jnp.where(lo, col0, col1)
                st = state_ref[j]
                y_off = jnp.exp(cs_p) * jnp.dot(cm_g, st.astype(BF16), preferred_element_type=F32)
                last = cs_p[L - 1:L, :]
                xdt_end = (xdt * jnp.exp(last - cs_p)).astype(BF16)
                new_st = lax.dot_general(bm_g, xdt_end, (((0,), (0,)), ((), ())), preferred_element_type=F32)
                state_ref[j] = st * jnp.exp(last) + new_st
                ybuf_ref[rows, sl] = y_diag + y_off + xs_p * dsk_ref[:, sl]

    y = ybuf_ref[...] * _silu(z_ref[...].astype(F32))
    gw = SSD_INNER // SSD_GROUPS
    parts = []
    for g in range(SSD_GROUPS):
        yg = y[:, g * gw:(g + 1) * gw]
        parts.append(yg * lax.rsqrt(jnp.mean(yg * yg, axis=-1, keepdims=True) + LN_EPS))
    yn = jnp.concatenate(parts, axis=1) * nw_ref[...]
    y_ref[...] = jnp.dot(yn.astype(BF16), wp_ref[...], preferred_element_type=F32).astype(BF16)


def _ssd_mixer(h, dt_raw, bsz, seq, conv_w, conv_b, dt_bias, a_log, d_skip, norm_w, w_proj):
    t = bsz * seq
    L = SSD_CHUNK
    ts = min(SSD_STEP_CHUNKS * L, seq)
    nc = seq // ts
    pad = LANES - SSD_HEADS
    dtb = jnp.pad(dt_bias, (0, pad))[None, :]
    alog = jnp.pad(a_log, (0, pad))[None, :]
    dsk = jnp.repeat(d_skip, SSD_HEAD_DIM)[None, :]
    r = jnp.arange(SSD_CONV * L, dtype=jnp.int32)
    shift = (jnp.arange(2 * L, dtype=jnp.int32)[None, :]
             == (L + r % L - (SSD_CONV - 1) + r // L)[:, None]).astype(BF16)
    const = lambda b, c: (0, 0)
    return pl.pallas_call(
        _ssd_kernel,
        out_shape=jax.ShapeDtypeStruct((t, D_MODEL), BF16),
        grid=(bsz, nc),
        in_specs=[
            pl.BlockSpec((ts, SSD_XBC), lambda b, c: (b * nc + c, COL_XBC // SSD_XBC)),
            pl.BlockSpec((ts, SSD_INNER), lambda b, c: (b * nc + c, COL_Z // SSD_INNER)),
            pl.BlockSpec((ts, LANES), lambda b, c: (b * nc + c, 0)),
            pl.BlockSpec((SSD_CONV * L, 2 * L), const),
            pl.BlockSpec((SSD_CONV, SSD_XBC), const),
            pl.BlockSpec((1, SSD_XBC), const),
            pl.BlockSpec((1, LANES), const),
            pl.BlockSpec((1, LANES), const),
            pl.BlockSpec((1, SSD_INNER), const),
            pl.BlockSpec((1, SSD_INNER), const),
            pl.BlockSpec((SSD_INNER, D_MODEL), const),
        ],
        out_specs=pl.BlockSpec((ts, D_MODEL), lambda b, c: (b * nc + c, 0)),
        scratch_shapes=[
            pltpu.VMEM((SSD_PAIRS, SSD_STATE, LANES), F32),
            pltpu.VMEM((ts + L, SSD_XBC), BF16),
            pltpu.VMEM((ts, SSD_INNER), F32),
        ],
        compiler_params=_params("parallel", "arbitrary"),
        name="ssd_mixer",
    )(h, h, dt_raw, shift, conv_w, conv_b[None, :], dtb, alog, dsk, norm_w[None, :], w_proj.astype(BF16))


CONF_TS = 256
CONF_HALO = 32
CONF_ROW_BLK = 64
CONF_COL_BLK = 256


def _conf_kernel(u_ref, shift_ref, dw_ref, db_ref, g_ref, b_ref, wp_ref, y_ref, ext_ref, conv_ref):
    ts = u_ref.shape[0]
    s = pl.program_id(1)
    span = ts + SUBLANES
    lead = CONF_HALO - SUBLANES

    @pl.when(s == 0)
    def _():
        ext_ref[:, 0:CONF_HALO, :] = jnp.zeros((SUBLANES, CONF_HALO, CONF_CH), F32)

    @pl.when(s > 0)
    def _():
        ext_ref[:, 0:CONF_HALO, :] = ext_ref[:, ts:ts + CONF_HALO, :]

    u = u_ref[...].astype(F32)
    glu = (u[:, :CONF_CH] * _sigmoid(u[:, CONF_CH:])).astype(BF16)
    ext_ref[0, CONF_HALO:CONF_HALO + ts, :] = glu.astype(F32)
    carried = ext_ref[1:SUBLANES, lead:CONF_HALO, :]
    shifted = jnp.dot(shift_ref[...], glu, preferred_element_type=F32)
    ext_ref[1:SUBLANES, lead:CONF_HALO + ts, :] = shifted.reshape(SUBLANES - 1, span, CONF_CH)
    ext_ref[1:SUBLANES, lead:CONF_HALO, :] = ext_ref[1:SUBLANES, lead:CONF_HALO, :] + carried

    for r0 in range(0, ts, CONF_ROW_BLK):
        for c0 in range(0, CONF_CH, CONF_COL_BLK):
            cols = slice(c0, c0 + CONF_COL_BLK)
            n_acc = CONF_ROW_BLK // SUBLANES
            acc = [jnp.broadcast_to(db_ref[:, cols], (SUBLANES, CONF_COL_BLK))] * n_acc
            for k in range(CONF_KERNEL):
                shift = CONF_HALO - (CONF_KERNEL - 1) + k
                off = shift - shift % SUBLANES + r0
                w = dw_ref[k, :, cols]
                acc = [a + w * ext_ref[shift % SUBLANES, off + SUBLANES * n:off + SUBLANES * (n + 1), cols]
                       for n, a in enumerate(acc)]
            conv_ref[r0:r0 + CONF_ROW_BLK, cols] = jnp.concatenate(acc, axis=0)

    hn = _silu(_layer_norm(conv_ref[...], g_ref[...], b_ref[...]))
    y_ref[...] = jnp.dot(hn.astype(BF16), wp_ref[...], preferred_element_type=F32).astype(BF16)


def _conformer(h, bsz, seq, dw_w, dw_b, ln_g, ln_b, w_proj):
    t = bsz * seq
    ts = min(CONF_TS, seq)
    nt = seq // ts
    const = lambda b, s: (0, 0)
    dw = jnp.broadcast_to(dw_w[:, None, :], (CONF_KERNEL, SUBLANES, CONF_CH))
    span = ts + SUBLANES
    r = jnp.arange((SUBLANES - 1) * span, dtype=jnp.int32)
    shift = (jnp.arange(ts, dtype=jnp.int32)[None, :] == (r % span - SUBLANES + 1 + r // span)[:, None]).astype(BF16)
    return pl.pallas_call(
        _conf_kernel,
        out_shape=jax.ShapeDtypeStruct((t, D_MODEL), BF16),
        grid=(bsz, nt),
        in_specs=[
            pl.BlockSpec((ts, 2 * CONF_CH), lambda b, s: (b * nt + s, COL_U // (2 * CONF_CH))),
            pl.BlockSpec(((SUBLANES - 1) * span, ts), const),
            pl.BlockSpec((CONF_KERNEL, SUBLANES, CONF_CH), lambda b, s: (0, 0, 0)),
            pl.BlockSpec((1, CONF_CH), const),
            pl.BlockSpec((1, CONF_CH), const),
            pl.BlockSpec((1, CONF_CH), const),
            pl.BlockSpec((CONF_CH, D_MODEL), const),
        ],
        out_specs=pl.BlockSpec((ts, D_MODEL), lambda b, s: (b * nt + s, 0)),
        scratch_shapes=[
            pltpu.VMEM((SUBLANES, CONF_HALO + ts, CONF_CH), F32),
            pltpu.VMEM((ts, CONF_CH), F32),
        ],
        compiler_params=_params("parallel", "arbitrary"),
        name="conformer_conv",
    )(h, shift, dw, dw_b[None, :], ln_g[None, :], ln_b[None, :], w_proj.astype(BF16))


def _pair_interleave(width):
    half = ATT_HEAD_DIM // 2
    j = jnp.arange(width, dtype=jnp.int32)
    block, lane = j // LANES, j % LANES
    chunk, within = lane // half, lane % half
    return block * LANES + (chunk % 2) * ATT_HEAD_DIM + (chunk // 2) * half + within


def _attn_lane_selectors():
    lane = jnp.arange(LANES, dtype=jnp.int32)
    swap = (lane + LANES // 2) % LANES
    first = (lane % ATT_HEAD_DIM) < ATT_HEAD_DIM // 2
    key_src = (jnp.where(first, lane, lane - ATT_HEAD_DIM // 2), jnp.where(first, lane + ATT_HEAD_DIM // 2, lane))
    val_src = (lane % ATT_HEAD_DIM, ATT_HEAD_DIM + lane % ATT_HEAD_DIM)
    sources = [swap, key_src[0], swap[key_src[0]], key_src[1], swap[key_src[1]], val_src[0], val_src[1]]
    return jnp.concatenate([(lane[:, None] == src[None, :]).astype(BF16) for src in sources], axis=1)


def _attn_kernel(q_ref, kc_ref, kp_ref, vc_ref, vp_ref, cosc_ref, sinc_ref, cosp_ref, sinp_ref, sink_ref,
                 sel_ref, wp_ref, y_ref, obuf_ref):
    blk = ATT_BLOCK
    n_sub = q_ref.shape[0] // blk
    i = pl.program_id(1)
    rep = ATT_Q_HEADS // ATT_KV_HEADS
    quarter = ATT_HEAD_DIM // 2
    scale = ATT_HEAD_DIM ** -0.5

    lane = lax.broadcasted_iota(jnp.int32, (blk, LANES), 1)
    first_q, lo_q = (lane % ATT_HEAD_DIM) < quarter, lane < ATT_HEAD_DIM

    cos_k = jnp.concatenate([cosp_ref[...], cosc_ref[...]], axis=0)
    sin_k = jnp.concatenate([sinp_ref[...], sinc_ref[...]], axis=0)
    k = jnp.concatenate([kp_ref[...], kc_ref[...]], axis=0)
    v = jnp.concatenate([vp_ref[...], vc_ref[...]], axis=0)

    qi = lax.broadcasted_iota(jnp.int32, (blk, 2 * blk), 0)
    kj = lax.broadcasted_iota(jnp.int32, (blk, 2 * blk), 1)
    mask_inner = (kj <= qi + blk) & (kj > qi)
    mask_first = (kj <= qi + blk) & (kj > jnp.where(i == 0, blk - 1, qi))

    keys = []
    for kh in range(ATT_KV_HEADS):
        block = slice((kh // 2) * LANES, (kh // 2 + 1) * LANES)
        sel = sel_ref[:, (1 + 2 * (kh % 2)) * LANES:(3 + 2 * (kh % 2)) * LANES]
        kd = jnp.dot(k[:, block], sel, preferred_element_type=F32)
        k_dup = kd[:, :LANES] * cos_k + kd[:, LANES:] * sin_k
        v_dup = jnp.dot(v[:, block], sel_ref[:, (5 + kh % 2) * LANES:(6 + kh % 2) * LANES],
                        preferred_element_type=F32)
        keys.append((k_dup.astype(BF16), jnp.concatenate([v_dup, jnp.ones_like(v_dup)], axis=1).astype(BF16)))

    def scores(kh, j):
        q_rows = slice(j * blk, (j + 1) * blk)
        cos_q, sin_q = cosc_ref[q_rows, :], sinc_ref[q_rows, :]
        stacked = []
        for qb in range(rep // 2):
            cols = slice((kh * (rep // 2) + qb) * LANES, (kh * (rep // 2) + qb + 1) * LANES)
            q_raw = q_ref[q_rows, cols]
            q_swap = jnp.dot(q_raw, sel_ref[:, 0:LANES], preferred_element_type=F32)
            qr = (q_raw.astype(F32) * cos_q + q_swap * sin_q) * scale
            stacked.append(jnp.where(first_q, qr, 0.0).astype(BF16))
            stacked.append(jnp.where(first_q, 0.0, qr).astype(BF16))
        qs = jnp.concatenate(stacked, axis=0)
        return lax.dot_general(qs, keys[kh][0][j * blk:(j + 2) * blk], (((1,), (1,)), ((), ())),
                               preferred_element_type=F32)

    def attend(kh, j, s_all):
        q_rows = slice(j * blk, (j + 1) * blk)
        mask = mask_first if j == 0 else mask_inner
        es, sink_terms = [], []
        for r in range(rep):
            hq = kh * rep + r
            s = jnp.where(mask, s_all[r * blk:(r + 1) * blk], -jnp.inf)
            sink = sink_ref[hq:hq + 1, :]
            mx = jnp.maximum(jnp.broadcast_to(jnp.max(s, axis=-1, keepdims=True), (blk, LANES)), sink)
            sink_terms.append(jnp.exp(sink - mx))
            es.append(jnp.exp(s - jnp.concatenate([mx, mx], axis=1)).astype(BF16))
        o = jnp.dot(jnp.concatenate(es, axis=0), keys[kh][1][j * blk:(j + 2) * blk], preferred_element_type=F32)
        outs = []
        for r in range(rep):
            rows = slice(r * blk, (r + 1) * blk)
            outs.append(o[rows, :LANES] * (1.0 / (o[rows, LANES:] + sink_terms[r])))
        for qb in range(rep // 2):
            cols = slice((kh * (rep // 2) + qb) * LANES, (kh * (rep // 2) + qb + 1) * LANES)
            obuf_ref[q_rows, cols] = jnp.where(lo_q, outs[2 * qb], outs[2 * qb + 1])

    pairs = [(kh, j) for kh in range(ATT_KV_HEADS) for j in range(n_sub)]
    s_next = scores(*pairs[0])
    for n, pair in enumerate(pairs):
        s_cur = s_next
        if n + 1 < len(pairs):
            s_next = scores(*pairs[n + 1])
        attend(*pair, s_cur)

    y_ref[...] = jnp.dot(obuf_ref[...].astype(BF16), wp_ref[...], preferred_element_type=F32).astype(BF16)


ATT_TQ = 4 * ATT_BLOCK


def _attention(h, cos_t, sin_t, bsz, seq, sinks, w_proj):
    t = bsz * seq
    blk = ATT_BLOCK
    tq = min(ATT_TQ, seq)
    n_sub = tq // blk
    nt = seq // tq
    nb = seq // blk
    qw = ATT_Q_HEADS * ATT_HEAD_DIM
    kw = ATT_KV_HEADS * ATT_HEAD_DIM
    cur = lambda b, i: b * nt + i
    prev = lambda b, i: b * nb + jnp.maximum(i * n_sub - 1, 0)
    const = lambda b, i: (0, 0)
    sink = jnp.broadcast_to(sinks[:, None], (ATT_Q_HEADS, LANES))
    return pl.pallas_call(
        _attn_kernel,
        out_shape=jax.ShapeDtypeStruct((t, D_MODEL), BF16),
        grid=(bsz, nt),
        in_specs=[
            pl.BlockSpec((tq, qw), lambda b, i: (cur(b, i), COL_Q // qw)),
            pl.BlockSpec((tq, kw), lambda b, i: (cur(b, i), COL_K // kw)),
            pl.BlockSpec((blk, kw), lambda b, i: (prev(b, i), COL_K // kw)),
            pl.BlockSpec((tq, kw), lambda b, i: (cur(b, i), COL_V // kw)),
            pl.BlockSpec((blk, kw), lambda b, i: (prev(b, i), COL_V // kw)),
            pl.BlockSpec((tq, LANES), lambda b, i: (cur(b, i), 0)),
            pl.BlockSpec((tq, LANES), lambda b, i: (cur(b, i), 0)),
            pl.BlockSpec((blk, LANES), lambda b, i: (prev(b, i), 0)),
            pl.BlockSpec((blk, LANES), lambda b, i: (prev(b, i), 0)),
            pl.BlockSpec((ATT_Q_HEADS, LANES), const),
            pl.BlockSpec((LANES, 7 * LANES), const),
            pl.BlockSpec((qw, D_MODEL), const),
        ],
        out_specs=pl.BlockSpec((tq, D_MODEL), lambda b, i: (cur(b, i), 0)),
        scratch_shapes=[pltpu.VMEM((tq, qw), F32)],
        compiler_params=_params("parallel", "arbitrary"),
        name="swa_attention",
    )(h, h, h, h, h, cos_t, sin_t, cos_t, sin_t, sink, _attn_lane_selectors(), w_proj.astype(BF16))


MERGE_TM = 512


def _merge_kernel(g0_ref, g1_ref, g2_ref, bg_ref, y0_ref, y1_ref, y2_ref, x_ref, wo_ref, lg_ref, lb_ref,
                  wrh_ref, wrl_ref, br_ref, x1_ref, logit_ref):
    mixed = None
    for n, (g_ref, y_ref) in enumerate(((g0_ref, y0_ref), (g1_ref, y1_ref), (g2_ref, y2_ref))):
        term = (jnp.tanh(g_ref[...].astype(F32) + bg_ref[n:n + 1, :]) + 1.0) * y_ref[...].astype(F32)
        mixed = term if mixed is None else mixed + term
    mixed = 0.5 * mixed
    r = DEEPNORM_ALPHA * x_ref[...] + jnp.dot(mixed.astype(BF16), wo_ref[...], preferred_element_type=F32)
    x1 = _layer_norm(r, lg_ref[...], lb_ref[...])
    _store_row_tiles(x1_ref, x1)
    x_hi = x1.astype(BF16)
    x_lo = (x1 - x_hi.astype(F32)).astype(BF16)
    logit_ref[...] = (jnp.dot(x_hi, wrh_ref[...], preferred_element_type=F32)
                      + (jnp.dot(x_hi, wrl_ref[...], preferred_element_type=F32)
                         + jnp.dot(x_lo, wrh_ref[...], preferred_element_type=F32))) + br_ref[...]


def _merge(h, b_gate, y_ssd, y_conf, y_att, x2d, w_out, ln_g, ln_b, w_router, b_router):
    t = x2d.shape[0]
    tm = min(MERGE_TM, t)
    const = lambda i: (0, 0)
    row = lambda i: (i, 0)
    w_router_hi = w_router.astype(BF16)
    w_router_lo = (w_router - w_router_hi.astype(F32)).astype(BF16)
    return pl.pallas_call(
        _merge_kernel,
        out_shape=(jax.ShapeDtypeStruct((t * ROW_TILES, LANES), F32), jax.ShapeDtypeStruct((t, LANES), F32)),
        grid=(t // tm,),
        in_specs=[
            pl.BlockSpec((tm, D_MODEL), lambda i: (i, 0)),
            pl.BlockSpec((tm, D_MODEL), lambda i: (i, 1)),
            pl.BlockSpec((tm, D_MODEL), lambda i: (i, 2)),
            pl.BlockSpec((3, D_MODEL), const),
            pl.BlockSpec((tm, D_MODEL), row),
            pl.BlockSpec((tm, D_MODEL), row),
            pl.BlockSpec((tm, D_MODEL), row),
            pl.BlockSpec((tm, D_MODEL), row),
            pl.BlockSpec((D_MODEL, D_MODEL), const),
            pl.BlockSpec((1, D_MODEL), const),
            pl.BlockSpec((1, D_MODEL), const),
            pl.BlockSpec((D_MODEL, LANES), const),
            pl.BlockSpec((D_MODEL, LANES), const),
            pl.BlockSpec((1, LANES), const),
        ],
        out_specs=(pl.BlockSpec((tm * ROW_TILES, LANES), row), pl.BlockSpec((tm, LANES), row)),
        compiler_params=_params("parallel"),
        name="merge_ln_router",
    )(h, h, h, 0.5 * b_gate, y_ssd, y_conf, y_att, x2d, w_out.astype(BF16), ln_g[None, :], ln_b[None, :],
      w_router_hi, w_router_lo, b_router)


GATHER_UNROLL = 8


def _on_slot(slot, fn, n_slots=2):
    for s in range(n_slots):
        @pl.when(slot == s)
        def _(s=s):
            fn(s)


def _gather_pipeline(i, n_steps, idx_hbm, src_hbm, idx_smem, buf, isem, gsem):
    n_rows = buf[0].shape[0] // ROW_TILES
    slot = i % 2

    def idx_copy(step, s):
        return pltpu.make_async_copy(idx_hbm.at[step], idx_smem[s], isem.at[s])

    def issue_rows(s):
        def body(it, carry):
            for u in range(GATHER_UNROLL):
                r = it * GATHER_UNROLL + u
                src_row = pl.multiple_of(idx_smem[s][r] * ROW_TILES, ROW_TILES)
                dst_row = pl.multiple_of(r * ROW_TILES, ROW_TILES)
                pltpu.make_async_copy(src_hbm.at[pl.ds(src_row, ROW_TILES)],
                                      buf[s].at[pl.ds(dst_row, ROW_TILES)], gsem.at[s]).start(priority=u % 2)
            return carry

        lax.fori_loop(0, n_rows // GATHER_UNROLL, body, 0)

    @pl.when(i == 0)
    def _():
        idx_copy(0, 0).start()
        idx_copy(0, 0).wait()
        issue_rows(0)

        @pl.when(n_steps > 1)
        def _():
            idx_copy(1, 1).start()

    @pl.when(i + 1 < n_steps)
    def _():
        def prefetch(s):
            idx_copy(i + 1, 1 - s).wait()
            issue_rows(1 - s)

            @pl.when(i + 2 < n_steps)
            def _():
                idx_copy(i + 2, s).start()

        _on_slot(slot, prefetch)


def _gather_wait(s, src_hbm, buf, gsem):
    pltpu.make_async_copy(src_hbm.at[pl.ds(0, buf[s].shape[0])], buf[s], gsem.at[s]).wait()


DISPATCH_SLOTS = 3


def _dispatch_kernel(pad_start_ref, pad_len_ref, idx_hbm, x1_hbm, rows_hbm, idx0, idx1, idx2, st0, st1, st2,
                     zero_ref, isem, xsem, dsem, zsem):
    i = pl.program_id(0)
    n_steps = pl.num_programs(0)
    idx = (idx0, idx1, idx2)
    stage = (st0, st1, st2)
    n_idx = idx0.shape[0]
    tm = n_idx // MOE_TOP_K

    def loads(step, s):
        first = pl.multiple_of(step * (tm * ROW_TILES), tm * ROW_TILES)
        return (pltpu.make_async_copy(idx_hbm.at[step], idx[s], isem.at[s]),
                pltpu.make_async_copy(x1_hbm.at[pl.ds(first, tm * ROW_TILES)], stage[s], xsem.at[s]))

    def start_loads(step, s):
        for c in loads(step, s):
            c.start()

    def wait_rows(s):
        pltpu.make_async_copy(x1_hbm.at[pl.ds(0, n_idx * ROW_TILES)], rows_hbm.at[pl.ds(0, n_idx * ROW_TILES)],
                              dsem.at[s]).wait()

    def zero_copy(row, n_rows):
        dst_row = pl.multiple_of(row * ROW_TILES, ROW_TILES)
        if n_rows is None:
            return pltpu.make_async_copy(zero_ref, rows_hbm.at[pl.ds(dst_row, ROW_TILES)], zsem.at[0])
        return pltpu.make_async_copy(x1_hbm.at[pl.ds(0, n_rows * ROW_TILES)],
                                     rows_hbm.at[pl.ds(dst_row, n_rows * ROW_TILES)], zsem.at[0])

    @pl.when(i == 0)
    def _():
        start_loads(0, 0)

        @pl.when(n_steps > 1)
        def _():
            start_loads(1, 1)

        zero_ref[...] = jnp.zeros(zero_ref.shape, F32)

        def zero_expert(e, carry):
            def zero_row(k, c):
                zero_copy(pad_start_ref[e] + k, None).start()
                return c

            return lax.fori_loop(0, pad_len_ref[e], zero_row, carry)

        lax.fori_loop(0, pad_len_ref.shape[0], zero_expert, 0)

    @pl.when(i == n_steps - 1)
    def _():
        def wait_expert(e, carry):
            @pl.when(pad_len_ref[e] > 0)
            def _():
                zero_copy(pad_start_ref[e], pad_len_ref[e]).wait()

            return carry

        lax.fori_loop(0, pad_len_ref.shape[0], wait_expert, 0)

    def step(s):
        freed = (s + DISPATCH_SLOTS - 1) % DISPATCH_SLOTS
        for c in loads(i, s):
            c.wait()

        def body(it, carry):
            for u in range(GATHER_UNROLL):
                r = it * GATHER_UNROLL + u
                src_row = pl.multiple_of((r % tm) * ROW_TILES, ROW_TILES)
                dst_row = pl.multiple_of(idx[s][r] * ROW_TILES, ROW_TILES)
                pltpu.make_async_copy(stage[s].at[pl.ds(src_row, ROW_TILES)],
                                      rows_hbm.at[pl.ds(dst_row, ROW_TILES)], dsem.at[s]).start(priority=u % 2)
            return carry

        lax.fori_loop(0, n_idx // GATHER_UNROLL, body, 0)

        @pl.when(i > 0)
        def _():
            wait_rows(freed)

        @pl.when(i + 2 < n_steps)
        def _():
            start_loads(i + 2, freed)

        @pl.when(i == n_steps - 1)
        def _():
            wait_rows(s)

    _on_slot(i % DISPATCH_SLOTS, step, DISPATCH_SLOTS)


def _moe_dispatch(x1, dest_blocks, pad_start, pad_len, n_rows):
    n_steps, n_idx = dest_blocks.shape
    any_spec = pl.BlockSpec(memory_space=pl.ANY)
    grid_spec = pltpu.PrefetchScalarGridSpec(
        num_scalar_prefetch=2,
        grid=(n_steps,),
        in_specs=[any_spec, any_spec],
        out_specs=any_spec,
        scratch_shapes=(
            [pltpu.SMEM((n_idx,), jnp.int32)] * DISPATCH_SLOTS
            + [pltpu.VMEM((n_idx // MOE_TOP_K * ROW_TILES, LANES), F32)] * DISPATCH_SLOTS
            + [pltpu.VMEM((ROW_TILES, LANES), F32)]
            + [pltpu.SemaphoreType.DMA((DISPATCH_SLOTS,))] * 3
            + [pltpu.SemaphoreType.DMA((1,))]
        ),
    )
    return pl.pallas_call(
        _dispatch_kernel,
        out_shape=jax.ShapeDtypeStruct((n_rows * ROW_TILES, LANES), F32),
        grid_spec=grid_spec,
        compiler_params=_params("arbitrary"),
        name="moe_dispatch",
    )(pad_start, pad_len, dest_blocks, x1)


def _moe_kernel(be_ref, nu_ref, x_ref, wg_ref, wu_ref, wd_ref, y_ref, wgb_ref, wub_ref, wdb_ref):
    i = pl.program_id(0)

    @pl.when(i < nu_ref[0])
    def _():
        @pl.when((i == 0) | (be_ref[i] != be_ref[jnp.maximum(i - 1, 0)]))
        def _():
            wgb_ref[...] = wg_ref[...].astype(BF16)
            wub_ref[...] = wu_ref[...].astype(BF16)
            wdb_ref[...] = wd_ref[...].astype(BF16)

        xb = _load_row_tiles(x_ref, 0, MOE_BLOCK).astype(BF16)
        gate = jnp.dot(xb, wgb_ref[...], preferred_element_type=F32)
        up = jnp.dot(xb, wub_ref[...], preferred_element_type=F32)
        hmid = (_silu(gate) * up).astype(BF16)
        _store_row_tiles(y_ref, jnp.dot(hmid, wdb_ref[...], preferred_element_type=F32))

    @pl.when(i >= nu_ref[0])
    def _():
        y_ref[...] = jnp.zeros(y_ref.shape, F32)


def _moe_experts(x_rows, block_e, n_used, w_gate, w_up, w_down):
    n_blocks = block_e.shape[0]
    grid_spec = pltpu.PrefetchScalarGridSpec(
        num_scalar_prefetch=2,
        grid=(n_blocks,),
        in_specs=[
            pl.BlockSpec((MOE_BLOCK * ROW_TILES, LANES), lambda i, be, nu: (jnp.minimum(i, nu[0] - 1), 0)),
            pl.BlockSpec((None, D_MODEL, MOE_FF), lambda i, be, nu: (be[i], 0, 0)),
            pl.BlockSpec((None, D_MODEL, MOE_FF), lambda i, be, nu: (be[i], 0, 0)),
            pl.BlockSpec((None, MOE_FF, D_MODEL), lambda i, be, nu: (be[i], 0, 0)),
        ],
        out_specs=pl.BlockSpec((MOE_BLOCK * ROW_TILES, LANES), lambda i, be, nu: (i, 0)),
        scratch_shapes=[
            pltpu.VMEM((D_MODEL, MOE_FF), BF16),
            pltpu.VMEM((D_MODEL, MOE_FF), BF16),
            pltpu.VMEM((MOE_FF, D_MODEL), BF16),
        ],
    )
    return pl.pallas_call(
        _moe_kernel,
        out_shape=jax.ShapeDtypeStruct((n_blocks * MOE_BLOCK * ROW_TILES, LANES), F32),
        grid_spec=grid_spec,
        compiler_params=_params("arbitrary"),
        name="moe_experts",
    )(block_e, n_used, x_rows, w_gate, w_up, w_down)


COMB_TM = 256


def _combine_kernel(idx_hbm, y_hbm, ew_ref, x1_ref, p_ref, lg_ref, lb_ref, wpg_ref, wpp_ref, out_ref,
                    idx0, idx1, ybuf0, ybuf1, ffn_ref, isem, gsem):
    i = pl.program_id(0)
    tm = out_ref.shape[0]
    ybuf = (ybuf0, ybuf1)
    _gather_pipeline(i, pl.num_programs(0), idx_hbm, y_hbm, (idx0, idx1), ybuf, isem, gsem)

    def weighted_sum(s):
        _gather_wait(s, y_hbm, ybuf, gsem)
        ew = ew_ref[...]
        ffn_ref[...] = (_load_row_tiles(ybuf[s], 0, tm) * ew[:, 0:1]
                        + _load_row_tiles(ybuf[s], tm, tm) * ew[:, 1:2])

    _on_slot(i % 2, weighted_sum)
    x1 = _load_row_tiles(x1_ref, 0, tm)
    x2 = _layer_norm(DEEPNORM_ALPHA * x1 + ffn_ref[...], lg_ref[...], lb_ref[...])
    gate = _sigmoid(jnp.dot(x2.astype(BF16), wpg_ref[...], preferred_element_type=F32))
    emb = jnp.dot(p_ref[...].astype(BF16), wpp_ref[...], preferred_element_type=F32)
    out_ref[...] = x2 + gate * emb


def _combine(y_rows, dest_blocks, e_w, x1, p2d, ln_g, ln_b, ple_w_gate, ple_w_proj):
    t = x1.shape[0] // ROW_TILES
    tm = dest_blocks.shape[1] // MOE_TOP_K
    const = lambda i: (0, 0)
    row = lambda i: (i, 0)
    return pl.pallas_call(
        _combine_kernel,
        out_shape=jax.ShapeDtypeStruct((t, D_MODEL), F32),
        grid=(t // tm,),
        in_specs=[
            pl.BlockSpec(memory_space=pl.ANY),
            pl.BlockSpec(memory_space=pl.ANY),
            pl.BlockSpec((tm, MOE_TOP_K), row),
            pl.BlockSpec((tm * ROW_TILES, LANES), row),
            pl.BlockSpec((tm, PLE_DIM), row),
            pl.BlockSpec((1, D_MODEL), const),
            pl.BlockSpec((1, D_MODEL), const),
            pl.BlockSpec((D_MODEL, D_MODEL), const),
            pl.BlockSpec((PLE_DIM, D_MODEL), const),
        ],
        out_specs=pl.BlockSpec((tm, D_MODEL), row),
        scratch_shapes=[
            pltpu.SMEM((MOE_TOP_K * tm,), jnp.int32),
            pltpu.SMEM((MOE_TOP_K * tm,), jnp.int32),
            pltpu.VMEM((MOE_TOP_K * tm * ROW_TILES, LANES), F32),
            pltpu.VMEM((MOE_TOP_K * tm * ROW_TILES, LANES), F32),
            pltpu.VMEM((tm, D_MODEL), F32),
            pltpu.SemaphoreType.DMA((2,)),
            pltpu.SemaphoreType.DMA((2,)),
        ],
        compiler_params=_params("arbitrary"),
        name="moe_combine",
    )(dest_blocks, y_rows, e_w, x1, p2d, ln_g[None, :], ln_b[None, :], ple_w_gate.astype(BF16),
      ple_w_proj.astype(BF16))


ROUTE_CHUNK = 256


def _route(logits):
    t = logits.shape[0]
    g_logits = logits[:, :MOE_GROUPS]
    e_logits = logits[:, MOE_GROUPS:MOE_GROUPS + MOE_EXPERTS]
    g_idx = jnp.argmax(g_logits, axis=-1).astype(jnp.int32)
    g_w = 1.0 / jnp.sum(jnp.exp(g_logits - jnp.max(g_logits, axis=-1, keepdims=True)), axis=-1)

    lane = jnp.arange(MOE_EXPERTS, dtype=jnp.int32)[None, :]
    masked = jnp.where(lane // MOE_EXPERTS_PER_GROUP == g_idx[:, None], e_logits, -jnp.inf)
    e1 = jnp.argmax(masked, axis=-1).astype(jnp.int32)
    sel1 = lane == e1[:, None]
    rest = jnp.where(sel1, -jnp.inf, masked)
    e2 = jnp.argmax(rest, axis=-1).astype(jnp.int32)
    sel2 = lane == e2[:, None]
    r = jnp.exp(jnp.max(rest, axis=-1) - jnp.max(masked, axis=-1))
    e_w = jnp.stack([1.0 / (1.0 + r), r / (1.0 + r)], axis=-1) * g_w[:, None]

    chunk = min(ROUTE_CHUNK, t)
    hot = (sel1 | sel2).reshape(t // chunk, chunk, MOE_EXPERTS)
    tril = jnp.tril(jnp.ones((chunk, chunk), BF16))
    within = jnp.einsum("ij,cje->cie", tril, hot.astype(BF16), preferred_element_type=F32).astype(jnp.int32)
    chunk_counts = within[:, -1, :]
    chunk_ends = jnp.cumsum(chunk_counts, axis=0)
    counts = chunk_ends[-1]
    padded = ((counts + MOE_BLOCK - 1) // MOE_BLOCK) * MOE_BLOCK
    pends = jnp.cumsum(padded)
    pstarts = pends - padded
    row = (within + (chunk_ends - chunk_counts + pstarts[None, :] - 1)[:, None, :]).reshape(t, MOE_EXPERTS)
    dest = (jnp.sum(jnp.where(sel1, row, 0), axis=-1), jnp.sum(jnp.where(sel2, row, 0), axis=-1))

    n_rows = t * MOE_TOP_K + MOE_EXPERTS * MOE_BLOCK
    n_blocks = n_rows // MOE_BLOCK
    block_start = jnp.arange(n_blocks, dtype=jnp.int32) * MOE_BLOCK
    block_e = jnp.sum((pends[None, :] <= block_start[:, None]).astype(jnp.int32), axis=1)
    block_e = jnp.minimum(block_e, MOE_EXPERTS - 1).astype(jnp.int32)
    n_used = (pends[-1] // MOE_BLOCK).astype(jnp.int32).reshape(1)
    pad = (jnp.concatenate([pstarts + counts, pends[-1:]]),
           jnp.concatenate([padded - counts, n_rows - pends[-1:]]))
    return e_w, dest, block_e, n_used, pad, n_rows


def _layer(x2d, p2d, cos_t, sin_t, bsz, seq, w_in, b_gate, ssd_conv_w, ssd_conv_b, ssd_dt_bias, ssd_a_log,
           ssd_d, ssd_norm_w, ssd_w_out, conf_dw_w, conf_dw_b, conf_ln_g, conf_ln_b, conf_w_out, attn_sinks,
           attn_w_out, w_out, ln1_g, ln1_b, moe_w_group, moe_b_group, moe_w_expert, moe_b_expert, moe_w_gate,
           moe_w_up, moe_w_down, ln2_g, ln2_b, ple_w_gate, ple_w_proj):
    t = bsz * seq
    w_main = jnp.concatenate([w_in[:, :DT_COL_ORIG], w_in[:, DT_COL_ORIG + SSD_HEADS:]], axis=1)
    w_main = jnp.concatenate([w_main[:, :COL_Q], w_main[:, COL_Q:COL_V][:, _pair_interleave(COL_V - COL_Q)],
                              w_main[:, COL_V:]], axis=1).astype(BF16)
    w_dt = jnp.pad(w_in[:, DT_COL_ORIG:DT_COL_ORIG + SSD_HEADS], ((0, 0), (0, LANES - SSD_HEADS))).astype(BF16)
    h, dt_raw = _in_projection(x2d, w_main, w_dt)

    y_ssd = _ssd_mixer(h, dt_raw, bsz, seq, ssd_conv_w, ssd_conv_b, ssd_dt_bias, ssd_a_log, ssd_d, ssd_norm_w,
                       ssd_w_out)
    y_conf = _conformer(h, bsz, seq, conf_dw_w, conf_dw_b, conf_ln_g, conf_ln_b, conf_w_out)
    y_att = _attention(h, cos_t, sin_t, bsz, seq, attn_sinks, attn_w_out)

    n_router = MOE_GROUPS + MOE_EXPERTS
    w_router = jnp.pad(jnp.concatenate([moe_w_group, moe_w_expert], axis=1), ((0, 0), (0, LANES - n_router)))
    b_router = jnp.pad(jnp.concatenate([moe_b_group, moe_b_expert]), (0, LANES - n_router))[None, :]
    x1, logits = _merge(h, b_gate, y_ssd, y_conf, y_att, x2d, w_out, ln1_g, ln1_b, w_router, b_router)

    e_w, dest, block_e, n_used, (pad_start, pad_len), n_rows = _route(logits)
    tm = min(COMB_TM, t)
    dest_blocks = jnp.concatenate([d.reshape(t // tm, tm) for d in dest], axis=1)
    x_rows = _moe_dispatch(x1, dest_blocks, pad_start, pad_len, n_rows)
    y_rows = _moe_experts(x_rows, block_e, n_used, moe_w_gate, moe_w_up, moe_w_down)
    return _combine(y_rows, dest_blocks, e_w, x1, p2d, ln2_g, ln2_b, ple_w_gate, ple_w_proj)


def kernel(x, p, positions, w_in, b_gate, ssd_conv_w, ssd_conv_b, ssd_dt_bias, ssd_a_log, ssd_d, ssd_norm_w, ssd_w_out, conf_dw_w, conf_dw_b, conf_ln_g, conf_ln_b, conf_w_out, attn_sinks, attn_w_out, w_out, ln1_g, ln1_b, moe_w_group, moe_b_group, moe_w_expert, moe_b_expert, moe_w_gate, moe_w_up, moe_w_down, ln2_g, ln2_b, ple_w_gate, ple_w_proj):
    bsz, seq, d = x.shape
    t = bsz * seq
    cos_t, sin_t = _rope_tables(positions)
    x2d = x.reshape(t, d)
    per_layer = (w_in, b_gate, ssd_conv_w, ssd_conv_b, ssd_dt_bias, ssd_a_log, ssd_d, ssd_norm_w, ssd_w_out,
                 conf_dw_w, conf_dw_b, conf_ln_g, conf_ln_b, conf_w_out, attn_sinks, attn_w_out, w_out, ln1_g,
                 ln1_b, moe_w_group, moe_b_group, moe_w_expert, moe_b_expert, moe_w_gate, moe_w_up, moe_w_down,
                 ln2_g, ln2_b, ple_w_gate, ple_w_proj)
    for layer in range(w_in.shape[0]):
        x2d = _layer(x2d, p[layer].reshape(t, -1), cos_t, sin_t, bsz, seq, *(w[layer] for w in per_layer))
    return x2d.reshape(bsz, seq, d)
```

```python
import functools

import jax
import jax.numpy as jnp
from jax import lax
from jax.experimental import pallas as pl
from jax.experimental.pallas import tpu as pltpu

F32 = jnp.float32
BF16 = jnp.bfloat16

D_MODEL = 1024
N_LAYERS = 2
PLE_DIM = 256
SSD_HEADS = 16
SSD_HEAD_DIM = 64
SSD_INNER = SSD_HEADS * SSD_HEAD_DIM
SSD_GROUPS = 4
SSD_STATE = 128
SSD_CONV = 4
SSD_CHUNK = 128
SSD_XBC = SSD_INNER + 2 * SSD_GROUPS * SSD_STATE
CONF_CH = 1024
CONF_KERNEL = 31
ATT_Q_HEADS = 16
ATT_KV_HEADS = 4
ATT_HEAD_DIM = 64
ATT_BLOCK = 128
ROPE_THETA = 10000.0
MOE_GROUPS = 4
MOE_EXPERTS_PER_GROUP = 8
MOE_EXPERTS = MOE_GROUPS * MOE_EXPERTS_PER_GROUP
MOE_TOP_K = 2
MOE_FF = 512
MOE_BLOCK = 256
DEEPNORM_ALPHA = (2 * N_LAYERS) ** 0.25
LN_EPS = 1e-5

LANES = 128
SUBLANES = 8
VMEM_LIMIT_BYTES = 56 * 1024 * 1024

COL_GATES = 0
COL_Z = 3 * D_MODEL
COL_XBC = COL_Z + SSD_INNER
COL_U = COL_XBC + SSD_XBC
COL_Q = COL_U + 2 * CONF_CH
COL_K = COL_Q + ATT_Q_HEADS * ATT_HEAD_DIM
COL_V = COL_K + ATT_KV_HEADS * ATT_HEAD_DIM
H_WIDTH = COL_V + ATT_KV_HEADS * ATT_HEAD_DIM
DT_COL_ORIG = 3 * D_MODEL + SSD_INNER + SSD_XBC


def _params(*semantics):
    return pltpu.CompilerParams(dimension_semantics=semantics, vmem_limit_bytes=VMEM_LIMIT_BYTES)


def _sigmoid(x):
    return 0.5 * (jnp.tanh(0.5 * x) + 1.0)


def _silu(x):
    half = 0.5 * x
    return half * (jnp.tanh(half) + 1.0)


def _layer_norm(x, g, b):
    mu = jnp.mean(x, axis=-1, keepdims=True)
    xc = x - mu
    var = jnp.mean(xc * xc, axis=-1, keepdims=True)
    return xc * lax.rsqrt(var + LN_EPS) * g + b


ROW_TILES = D_MODEL // LANES


def _store_row_tiles(ref, x):
    rows = x.shape[0]
    for k in range(ROW_TILES):
        ref[pl.ds(k, rows, stride=ROW_TILES), :] = x[:, k * LANES:(k + 1) * LANES]


def _load_row_tiles(ref, first_row, rows):
    return jnp.concatenate(
        [ref[pl.ds(first_row * ROW_TILES + k, rows, stride=ROW_TILES), :] for k in range(ROW_TILES)], axis=1)


IN_TM = 512
IN_TN = 512


def _inproj_kernel(x_ref, w_ref, wdt_ref, h_ref, dt_ref):
    xb = x_ref[...].astype(BF16)
    dt_ref[...] = jnp.dot(xb, wdt_ref[...], preferred_element_type=F32)
    for j in range(H_WIDTH // IN_TN):
        cols = slice(j * IN_TN, (j + 1) * IN_TN)
        acc = jnp.dot(xb, w_ref[:, cols], preferred_element_type=F32)
        if (j + 1) * IN_TN <= COL_Z:
            acc = 0.5 * acc
        h_ref[:, cols] = acc.astype(BF16)


def _in_projection(x2d, w_main, w_dt):
    t = x2d.shape[0]
    tm = min(IN_TM, t)
    resident = pl.Buffered(1)
    return pl.pallas_call(
        _inproj_kernel,
        out_shape=(jax.ShapeDtypeStruct((t, H_WIDTH), BF16), jax.ShapeDtypeStruct((t, LANES), F32)),
        grid=(t // tm,),
        in_specs=[
            pl.BlockSpec((tm, D_MODEL), lambda i: (i, 0)),
            pl.BlockSpec((D_MODEL, H_WIDTH), lambda i: (0, 0), pipeline_mode=resident),
            pl.BlockSpec((D_MODEL, LANES), lambda i: (0, 0), pipeline_mode=resident),
        ],
        out_specs=(
            pl.BlockSpec((tm, H_WIDTH), lambda i: (i, 0)),
            pl.BlockSpec((tm, LANES), lambda i: (i, 0)),
        ),
        compiler_params=_params("parallel"),
        name="in_projection",
    )(x2d, w_main, w_dt)


def _rope_kernel(pos_ref, inv_ref, sign_ref, cos_ref, sin_ref):
    ang = pos_ref[...].astype(F32) * inv_ref[...]
    cos_ref[...] = jnp.cos(ang)
    sin_ref[...] = jnp.sin(ang) * sign_ref[...]


def _rope_tables(positions):
    t = positions.size
    tm = min(1024, t)
    half = ATT_HEAD_DIM // 2
    inv_freq = ROPE_THETA ** (-jnp.arange(half, dtype=F32) / half)
    inv = jnp.tile(inv_freq, LANES // half)[None, :]
    sign = jnp.concatenate([-jnp.ones((LANES // 2,), F32), jnp.ones((LANES // 2,), F32)])[None, :]
    return pl.pallas_call(
        _rope_kernel,
        out_shape=(jax.ShapeDtypeStruct((t, LANES), F32), jax.ShapeDtypeStruct((t, LANES), F32)),
        grid=(t // tm,),
        in_specs=[
            pl.BlockSpec((tm, 1), lambda i: (i, 0)),
            pl.BlockSpec((1, LANES), lambda i: (0, 0)),
            pl.BlockSpec((1, LANES), lambda i: (0, 0)),
        ],
        out_specs=(pl.BlockSpec((tm, LANES), lambda i: (i, 0)), pl.BlockSpec((tm, LANES), lambda i: (i, 0))),
        compiler_params=_params("parallel"),
        name="rope_tables",
    )(positions.reshape(t, 1), inv, sign)


SSD_PAIRS = SSD_HEADS // 2
SSD_STEP_CHUNKS = 4


def _ssd_kernel(xbc_ref, z_ref, dt_ref, shift_ref, cw_ref, cb_ref, dtb_ref, alog_ref, dsk_ref, nw_ref, wp_ref,
                y_ref, state_ref, ext_ref, ybuf_ref):
    L = SSD_CHUNK
    n_sub = xbc_ref.shape[0] // L
    c = pl.program_id(1)

    @pl.when(c == 0)
    def _():
        state_ref[...] = jnp.zeros(state_ref.shape, F32)
        ext_ref[0:L, :] = jnp.zeros((L, SSD_XBC), BF16)

    @pl.when(c > 0)
    def _():
        ext_ref[0:L, :] = ext_ref[n_sub * L:(n_sub + 1) * L, :]

    ext_ref[L:(n_sub + 1) * L, :] = xbc_ref[...]

    row = lax.broadcasted_iota(jnp.int32, (L, L), 0)
    col = lax.broadcasted_iota(jnp.int32, (L, L), 1)
    causal = row >= col
    tril = causal.astype(F32)
    lo = col < SSD_HEAD_DIM
    a = -jnp.exp(alog_ref[...])

    steps = []
    for sub in range(n_sub):
        x_dt = dt_ref[sub * L:(sub + 1) * L, :] + dtb_ref[...]
        dt = jnp.maximum(x_dt, 0.0) + jnp.log1p(jnp.exp(-jnp.abs(x_dt)))
        cs = jnp.dot(tril, dt * a, preferred_element_type=F32, precision=lax.Precision.HIGHEST)
        steps.append((dt, cs, cs.T))

    for sub in range(n_sub):
        rows = slice(sub * L, (sub + 1) * L)
        dt, cs, cs_t = steps[sub]

        taps = jnp.dot(shift_ref[...], ext_ref[sub * L:(sub + 2) * L, :], preferred_element_type=F32)
        acc = cb_ref[...] + cw_ref[0:1, :] * taps[0:L]
        for k in range(1, SSD_CONV):
            acc = acc + cw_ref[k:k + 1, :] * taps[k * L:(k + 1) * L]
        act = _silu(acc)
        xs = act[:, :SSD_INNER]
        bm = act[:, SSD_INNER:SSD_INNER + SSD_GROUPS * SSD_STATE]
        cm = act[:, SSD_INNER + SSD_GROUPS * SSD_STATE:]

        for g in range(SSD_GROUPS):
            bm_g = bm[:, g * SSD_STATE:(g + 1) * SSD_STATE].astype(BF16)
            cm_g = cm[:, g * SSD_STATE:(g + 1) * SSD_STATE].astype(BF16)
            cb_g = lax.dot_general(cm_g, bm_g, (((1,), (1,)), ((), ())), preferred_element_type=F32)
            pairs_per_group = SSD_PAIRS // SSD_GROUPS
            for jj in range(pairs_per_group):
                j = g * pairs_per_group + jj
                h0, h1 = 2 * j, 2 * j + 1
                sl = slice(j * LANES, (j + 1) * LANES)
                col0 = jnp.broadcast_to(cs[:, h0:h0 + 1], (L, L))
                col1 = jnp.broadcast_to(cs[:, h1:h1 + 1], (L, L))
                dec0 = jnp.where(causal, jnp.exp(col0 - cs_t[h0:h0 + 1, :]), 0.0)
                dec1 = jnp.where(causal, jnp.exp(col1 - cs_t[h1:h1 + 1, :]), 0.0)
                m = jnp.concatenate([cb_g * dec0, cb_g * dec1], axis=1).astype(BF16)
                dt_p = jnp.where(lo, jnp.broadcast_to(dt[:, h0:h0 + 1], (L, L)),
                                 jnp.broadcast_to(dt[:, h1:h1 + 1], (L, L)))
                xs_p = xs[:, sl]
                xdt = xs_p * dt_p
                x2 = jnp.concatenate([jnp.where(lo, xdt, 0.0), jnp.where(lo, 0.0, xdt)], axis=0).astype(BF16)
                y_diag = jnp.dot(m, x2, preferred_element_type=F32)
                cs_p = jnp.where(lo, col0, col1)
                st = state_ref[j]
                y_off = jnp.exp(cs_p) * jnp.dot(cm_g, st.astype(BF16), preferred_element_type=F32)
                last = cs_p[L - 1:L, :]
                xdt_end = (xdt * jnp.exp(last - cs_p)).astype(BF16)
                new_st = lax.dot_general(bm_g, xdt_end, (((0,), (0,)), ((), ())), preferred_element_type=F32)
                state_ref[j] = st * jnp.exp(last) + new_st
                ybuf_ref[rows, sl] = y_diag + y_off + xs_p * dsk_ref[:, sl]

    y = ybuf_ref[...] * _silu(z_ref[...].astype(F32))
    gw = SSD_INNER // SSD_GROUPS
    parts = []
    for g in range(SSD_GROUPS):
        yg = y[:, g * gw:(g + 1) * gw]
        parts.append(yg * lax.rsqrt(jnp.mean(yg * yg, axis=-1, keepdims=True) + LN_EPS))
    yn = jnp.concatenate(parts, axis=1) * nw_ref[...]
    y_ref[...] = jnp.dot(yn.astype(BF16), wp_ref[...], preferred_element_type=F32).astype(BF16)


def _ssd_mixer(h, dt_raw, bsz, seq, conv_w, conv_b, dt_bias, a_log, d_skip, norm_w, w_proj):
    t = bsz * seq
    L = SSD_CHUNK
    ts = min(SSD_STEP_CHUNKS * L, seq)
    nc = seq // ts
    pad = LANES - SSD_HEADS
    dtb = jnp.pad(dt_bias, (0, pad))[None, :]
    alog = jnp.pad(a_log, (0, pad))[None, :]
    dsk = jnp.repeat(d_skip, SSD_HEAD_DIM)[None, :]
    r = jnp.arange(SSD_CONV * L, dtype=jnp.int32)
    shift = (jnp.arange(2 * L, dtype=jnp.int32)[None, :]
             == (L + r % L - (SSD_CONV - 1) + r // L)[:, None]).astype(BF16)
    const = lambda b, c: (0, 0)
    return pl.pallas_call(
        _ssd_kernel,
        out_shape=jax.ShapeDtypeStruct((t, D_MODEL), BF16),
        grid=(bsz, nc),
        in_specs=[
            pl.BlockSpec((ts, SSD_XBC), lambda b, c: (b * nc + c, COL_XBC // SSD_XBC)),
            pl.BlockSpec((ts, SSD_INNER), lambda b, c: (b * nc + c, COL_Z // SSD_INNER)),
            pl.BlockSpec((ts, LANES), lambda b, c: (b * nc + c, 0)),
            pl.BlockSpec((SSD_CONV * L, 2 * L), const),
            pl.BlockSpec((SSD_CONV, SSD_XBC), const),
            pl.BlockSpec((1, SSD_XBC), const),
            pl.BlockSpec((1, LANES), const),
            pl.BlockSpec((1, LANES), const),
            pl.BlockSpec((1, SSD_INNER), const),
            pl.BlockSpec((1, SSD_INNER), const),
            pl.BlockSpec((SSD_INNER, D_MODEL), const),
        ],
        out_specs=pl.BlockSpec((ts, D_MODEL), lambda b, c: (b * nc + c, 0)),
        scratch_shapes=[
            pltpu.VMEM((SSD_PAIRS, SSD_STATE, LANES), F32),
            pltpu.VMEM((ts + L, SSD_XBC), BF16),
            pltpu.VMEM((ts, SSD_INNER), F32),
        ],
        compiler_params=_params("parallel", "arbitrary"),
        name="ssd_mixer",
    )(h, h, dt_raw, shift, conv_w, conv_b[None, :], dtb, alog, dsk, norm_w[None, :], w_proj.astype(BF16))


CONF_TS = 256
CONF_HALO = 32
CONF_ROW_BLK = 64
CONF_COL_BLK = 256


def _conf_kernel(u_ref, shift_ref, dw_ref, db_ref, g_ref, b_ref, wp_ref, y_ref, ext_ref, conv_ref):
    ts = u_ref.shape[0]
    s = pl.program_id(1)
    span = ts + SUBLANES
    lead = CONF_HALO - SUBLANES

    @pl.when(s == 0)
    def _():
        ext_ref[:, 0:CONF_HALO, :] = jnp.zeros((SUBLANES, CONF_HALO, CONF_CH), F32)

    @pl.when(s > 0)
    def _():
        ext_ref[:, 0:CONF_HALO, :] = ext_ref[:, ts:ts + CONF_HALO, :]

    u = u_ref[...].astype(F32)
    glu = (u[:, :CONF_CH] * _sigmoid(u[:, CONF_CH:])).astype(BF16)
    ext_ref[0, CONF_HALO:CONF_HALO + ts, :] = glu.astype(F32)
    carried = ext_ref[1:SUBLANES, lead:CONF_HALO, :]
    shifted = jnp.dot(shift_ref[...], glu, preferred_element_type=F32)
    ext_ref[1:SUBLANES, lead:CONF_HALO + ts, :] = shifted.reshape(SUBLANES - 1, span, CONF_CH)
    ext_ref[1:SUBLANES, lead:CONF_HALO, :] = ext_ref[1:SUBLANES, lead:CONF_HALO, :] + carried

    for r0 in range(0, ts, CONF_ROW_BLK):
        for c0 in range(0, CONF_CH, CONF_COL_BLK):
            cols = slice(c0, c0 + CONF_COL_BLK)
            n_acc = CONF_ROW_BLK // SUBLANES
            acc = [jnp.broadcast_to(db_ref[:, cols], (SUBLANES, CONF_COL_BLK))] * n_acc
            for k in range(CONF_KERNEL):
                shift = CONF_HALO - (CONF_KERNEL - 1) + k
                off = shift - shift % SUBLANES + r0
                w = dw_ref[k, :, cols]
                acc = [a + w * ext_ref[shift % SUBLANES, off + SUBLANES * n:off + SUBLANES * (n + 1), cols]
                       for n, a in enumerate(acc)]
            conv_ref[r0:r0 + CONF_ROW_BLK, cols] = jnp.concatenate(acc, axis=0)

    hn = _silu(_layer_norm(conv_ref[...], g_ref[...], b_ref[...]))
    y_ref[...] = jnp.dot(hn.astype(BF16), wp_ref[...], preferred_element_type=F32).astype(BF16)


def _conformer(h, bsz, seq, dw_w, dw_b, ln_g, ln_b, w_proj):
    t = bsz * seq
    ts = min(CONF_TS, seq)
    nt = seq // ts
    const = lambda b, s: (0, 0)
    dw = jnp.broadcast_to(dw_w[:, None, :], (CONF_KERNEL, SUBLANES, CONF_CH))
    span = ts + SUBLANES
    r = jnp.arange((SUBLANES - 1) * span, dtype=jnp.int32)
    shift = (jnp.arange(ts, dtype=jnp.int32)[None, :] == (r % span - SUBLANES + 1 + r // span)[:, None]).astype(BF16)
    return pl.pallas_call(
        _conf_kernel,
        out_shape=jax.ShapeDtypeStruct((t, D_MODEL), BF16),
        grid=(bsz, nt),
        in_specs=[
            pl.BlockSpec((ts, 2 * CONF_CH), lambda b, s: (b * nt + s, COL_U // (2 * CONF_CH))),
            pl.BlockSpec(((SUBLANES - 1) * span, ts), const),
            pl.BlockSpec((CONF_KERNEL, SUBLANES, CONF_CH), lambda b, s: (0, 0, 0)),
            pl.BlockSpec((1, CONF_CH), const),
            pl.BlockSpec((1, CONF_CH), const),
            pl.BlockSpec((1, CONF_CH), const),
            pl.BlockSpec((CONF_CH, D_MODEL), const),
        ],
        out_specs=pl.BlockSpec((ts, D_MODEL), lambda b, s: (b * nt + s, 0)),
        scratch_shapes=[
            pltpu.VMEM((SUBLANES, CONF_HALO + ts, CONF_CH), F32),
            pltpu.VMEM((ts, CONF_CH), F32),
        ],
        compiler_params=_params("parallel", "arbitrary"),
        name="conformer_conv",
    )(h, shift, dw, dw_b[None, :], ln_g[None, :], ln_b[None, :], w_proj.astype(BF16))


def _attn_lane_selectors():
    lane = jnp.arange(LANES, dtype=jnp.int32)
    swap = (lane + LANES // 2) % LANES
    first = (lane % ATT_HEAD_DIM) < ATT_HEAD_DIM // 2
    key_src = (jnp.where(first, lane, lane - ATT_HEAD_DIM // 2), jnp.where(first, lane + ATT_HEAD_DIM // 2, lane))
    val_src = (lane % ATT_HEAD_DIM, ATT_HEAD_DIM + lane % ATT_HEAD_DIM)
    sources = [swap, key_src[0], swap[key_src[0]], key_src[1], swap[key_src[1]], val_src[0], val_src[1]]
    return jnp.concatenate([(lane[:, None] == src[None, :]).astype(BF16) for src in sources], axis=1)


def _attn_kernel(q_ref, kc_ref, kp_ref, vc_ref, vp_ref, cosc_ref, sinc_ref, cosp_ref, sinp_ref, sink_ref,
                 sel_ref, wp_ref, y_ref, obuf_ref, qrot_ref):
    blk = ATT_BLOCK
    n_sub = q_ref.shape[0] // blk
    i = pl.program_id(1)
    rep = ATT_Q_HEADS // ATT_KV_HEADS
    quarter = ATT_HEAD_DIM // 2
    scale = ATT_HEAD_DIM ** -0.5

    lane = lax.broadcasted_iota(jnp.int32, (blk, LANES), 1)
    first_q, lo_q = (lane % ATT_HEAD_DIM) < quarter, lane < ATT_HEAD_DIM

    cos_k = jnp.concatenate([cosp_ref[...], cosc_ref[...]], axis=0)
    sin_k = jnp.concatenate([sinp_ref[...], sinc_ref[...]], axis=0)
    k = jnp.concatenate([kp_ref[...], kc_ref[...]], axis=0)
    v = jnp.concatenate([vp_ref[...], vc_ref[...]], axis=0)

    qi = lax.broadcasted_iota(jnp.int32, (blk, 2 * blk), 0)
    kj = lax.broadcasted_iota(jnp.int32, (blk, 2 * blk), 1)
    mask_inner = (kj <= qi + blk) & (kj > qi)
    mask_first = (kj <= qi + blk) & (kj > jnp.where(i == 0, blk - 1, qi))

    keys = []
    for kh in range(ATT_KV_HEADS):
        block = slice((kh // 2) * LANES, (kh // 2 + 1) * LANES)
        sel = sel_ref[:, (1 + 2 * (kh % 2)) * LANES:(3 + 2 * (kh % 2)) * LANES]
        kd = jnp.dot(k[:, block], sel, preferred_element_type=F32)
        k_dup = kd[:, :LANES] * cos_k + kd[:, LANES:] * sin_k
        v_dup = jnp.dot(v[:, block], sel_ref[:, (5 + kh % 2) * LANES:(6 + kh % 2) * LANES],
                        preferred_element_type=F32)
        keys.append((k_dup.astype(BF16), jnp.concatenate([v_dup, jnp.ones_like(v_dup)], axis=1).astype(BF16)))

    for cb in range(q_ref.shape[1] // LANES):
        cols = slice(cb * LANES, (cb + 1) * LANES)
        q_raw = q_ref[:, cols]
        q_swap = jnp.dot(q_raw, sel_ref[:, 0:LANES], preferred_element_type=F32)
        qrot_ref[:, cols] = ((q_raw.astype(F32) * cosc_ref[...] + q_swap * sinc_ref[...]) * scale).astype(BF16)

    def scores(kh, j):
        q_rows = slice(j * blk, (j + 1) * blk)
        stacked = []
        for qb in range(rep // 2):
            cols = slice((kh * (rep // 2) + qb) * LANES, (kh * (rep // 2) + qb + 1) * LANES)
            qr = qrot_ref[q_rows, cols]
            stacked.append(jnp.where(first_q, qr, jnp.zeros_like(qr)))
            stacked.append(jnp.where(first_q, jnp.zeros_like(qr), qr))
        qs = jnp.concatenate(stacked, axis=0)
        return lax.dot_general(qs, keys[kh][0][j * blk:(j + 2) * blk], (((1,), (1,)), ((), ())),
                               preferred_element_type=F32)

    def attend(kh, j, s_all):
        q_rows = slice(j * blk, (j + 1) * blk)
        mask = mask_first if j == 0 else mask_inner
        es, sink_terms = [], []
        for r in range(rep):
            hq = kh * rep + r
            s = jnp.where(mask, s_all[r * blk:(r + 1) * blk], -jnp.inf)
            sink = sink_ref[hq:hq + 1, :]
            mx = jnp.maximum(jnp.broadcast_to(jnp.max(s, axis=-1, keepdims=True), (blk, LANES)), sink)
            sink_terms.append(jnp.exp(sink - mx))
            es.append(jnp.exp(s - jnp.concatenate([mx, mx], axis=1)).astype(BF16))
        o = jnp.dot(jnp.concatenate(es, axis=0), keys[kh][1][j * blk:(j + 2) * blk], preferred_element_type=F32)
        outs = []
        for r in range(rep):
            rows = slice(r * blk, (r + 1) * blk)
            outs.append(o[rows, :LANES] * (1.0 / (o[rows, LANES:] + sink_terms[r])))
        for qb in range(rep // 2):
            cols = slice((kh * (rep // 2) + qb) * LANES, (kh * (rep // 2) + qb + 1) * LANES)
            obuf_ref[q_rows, cols] = jnp.where(lo_q, outs[2 * qb], outs[2 * qb + 1])

    pairs = [(kh, j) for kh in range(ATT_KV_HEADS) for j in range(n_sub)]
    s_next = scores(*pairs[0])
    for n, pair in enumerate(pairs):
        s_cur = s_next
        if n + 1 < len(pairs):
            s_next = scores(*pairs[n + 1])
        attend(*pair, s_cur)

    y_ref[...] = jnp.dot(obuf_ref[...].astype(BF16), wp_ref[...], preferred_element_type=F32).astype(BF16)


ATT_TQ = 4 * ATT_BLOCK


def _attention(h, cos_t, sin_t, bsz, seq, sinks, w_proj):
    t = bsz * seq
    blk = ATT_BLOCK
    tq = min(ATT_TQ, seq)
    n_sub = tq // blk
    nt = seq // tq
    nb = seq // blk
    qw = ATT_Q_HEADS * ATT_HEAD_DIM
    kw = ATT_KV_HEADS * ATT_HEAD_DIM
    cur = lambda b, i: b * nt + i
    prev = lambda b, i: b * nb + jnp.maximum(i * n_sub - 1, 0)
    const = lambda b, i: (0, 0)
    sink = jnp.broadcast_to(sinks[:, None], (ATT_Q_HEADS, LANES))
    return pl.pallas_call(
        _attn_kernel,
        out_shape=jax.ShapeDtypeStruct((t, D_MODEL), BF16),
        grid=(bsz, nt),
        in_specs=[
            pl.BlockSpec((tq, qw), lambda b, i: (cur(b, i), COL_Q // qw)),
            pl.BlockSpec((tq, kw), lambda b, i: (cur(b, i), COL_K // kw)),
            pl.BlockSpec((blk, kw), lambda b, i: (prev(b, i), COL_K // kw)),
            pl.BlockSpec((tq, kw), lambda b, i: (cur(b, i), COL_V // kw)),
            pl.BlockSpec((blk, kw), lambda b, i: (prev(b, i), COL_V // kw)),
            pl.BlockSpec((tq, LANES), lambda b, i: (cur(b, i), 0)),
            pl.BlockSpec((tq, LANES), lambda b, i: (cur(b, i), 0)),
            pl.BlockSpec((blk, LANES), lambda b, i: (prev(b, i), 0)),
            pl.BlockSpec((blk, LANES), lambda b, i: (prev(b, i), 0)),
            pl.BlockSpec((ATT_Q_HEADS, LANES), const),
            pl.BlockSpec((LANES, 7 * LANES), const),
            pl.BlockSpec((qw, D_MODEL), const),
        ],
        out_specs=pl.BlockSpec((tq, D_MODEL), lambda b, i: (cur(b, i), 0)),
        scratch_shapes=[pltpu.VMEM((tq, qw), F32), pltpu.VMEM((tq, qw), BF16)],
        compiler_params=_params("parallel", "arbitrary"),
        name="swa_attention",
    )(h, h, h, h, h, cos_t, sin_t, cos_t, sin_t, sink, _attn_lane_selectors(), w_proj.astype(BF16))


MERGE_TM = 512


def _merge_kernel(g0_ref, g1_ref, g2_ref, bg_ref, y0_ref, y1_ref, y2_ref, x_ref, wo_ref, lg_ref, lb_ref,
                  wrh_ref, wrl_ref, br_ref, x1_ref, logit_ref):
    mixed = None
    for n, (g_ref, y_ref) in enumerate(((g0_ref, y0_ref), (g1_ref, y1_ref), (g2_ref, y2_ref))):
        term = (jnp.tanh(g_ref[...].astype(F32) + bg_ref[n:n + 1, :]) + 1.0) * y_ref[...].astype(F32)
        mixed = term if mixed is None else mixed + term
    mixed = 0.5 * mixed
    r = DEEPNORM_ALPHA * x_ref[...] + jnp.dot(mixed.astype(BF16), wo_ref[...], preferred_element_type=F32)
    x1 = _layer_norm(r, lg_ref[...], lb_ref[...])
    _store_row_tiles(x1_ref, x1)
    x_hi = x1.astype(BF16)
    x_lo = (x1 - x_hi.astype(F32)).astype(BF16)
    logit_ref[...] = (jnp.dot(x_hi, wrh_ref[...], preferred_element_type=F32)
                      + (jnp.dot(x_hi, wrl_ref[...], preferred_element_type=F32)
                         + jnp.dot(x_lo, wrh_ref[...], preferred_element_type=F32))) + br_ref[...]


def _merge(h, b_gate, y_ssd, y_conf, y_att, x2d, w_out, ln_g, ln_b, w_router, b_router):
    t = x2d.shape[0]
    tm = min(MERGE_TM, t)
    const = lambda i: (0, 0)
    row = lambda i: (i, 0)
    w_router_hi = w_router.astype(BF16)
    w_router_lo = (w_router - w_router_hi.astype(F32)).astype(BF16)
    return pl.pallas_call(
        _merge_kernel,
        out_shape=(jax.ShapeDtypeStruct((t * ROW_TILES, LANES), F32), jax.ShapeDtypeStruct((t, LANES), F32)),
        grid=(t // tm,),
        in_specs=[
            pl.BlockSpec((tm, D_MODEL), lambda i: (i, 0)),
            pl.BlockSpec((tm, D_MODEL), lambda i: (i, 1)),
            pl.BlockSpec((tm, D_MODEL), lambda i: (i, 2)),
            pl.BlockSpec((3, D_MODEL), const),
            pl.BlockSpec((tm, D_MODEL), row),
            pl.BlockSpec((tm, D_MODEL), row),
            pl.BlockSpec((tm, D_MODEL), row),
            pl.BlockSpec((tm, D_MODEL), row),
            pl.BlockSpec((D_MODEL, D_MODEL), const),
            pl.BlockSpec((1, D_MODEL), const),
            pl.BlockSpec((1, D_MODEL), const),
            pl.BlockSpec((D_MODEL, LANES), const),
            pl.BlockSpec((D_MODEL, LANES), const),
            pl.BlockSpec((1, LANES), const),
        ],
        out_specs=(pl.BlockSpec((tm * ROW_TILES, LANES), row), pl.BlockSpec((tm, LANES), row)),
        compiler_params=_params("parallel"),
        name="merge_ln_router",
    )(h, h, h, 0.5 * b_gate, y_ssd, y_conf, y_att, x2d, w_out.astype(BF16), ln_g[None, :], ln_b[None, :],
      w_router_hi, w_router_lo, b_router)


GATHER_UNROLL = 16


def _on_slot(slot, fn, n_slots=2):
    for s in range(n_slots):
        @pl.when(slot == s)
        def _(s=s):
            fn(s)


def _gather_pipeline(i, n_steps, idx_hbm, src_hbm, idx_smem, buf, isem, gsem):
    n_rows = buf[0].shape[0] // ROW_TILES
    slot = i % 2

    def idx_copy(step, s):
        return pltpu.make_async_copy(idx_hbm.at[step], idx_smem[s], isem.at[s])

    def issue_rows(s):
        def body(it, carry):
            for u in range(GATHER_UNROLL):
                r = it * GATHER_UNROLL + u
                src_row = pl.multiple_of(idx_smem[s][r], ROW_TILES)
                dst_row = pl.multiple_of(r * ROW_TILES, ROW_TILES)
                pltpu.make_async_copy(src_hbm.at[pl.ds(src_row, ROW_TILES)],
                                      buf[s].at[pl.ds(dst_row, ROW_TILES)], gsem.at[s]).start(priority=u % 2)
            return carry

        lax.fori_loop(0, n_rows // GATHER_UNROLL, body, 0)

    @pl.when(i == 0)
    def _():
        idx_copy(0, 0).start()
        idx_copy(0, 0).wait()
        issue_rows(0)

        @pl.when(n_steps > 1)
        def _():
            idx_copy(1, 1).start()

    @pl.when(i + 1 < n_steps)
    def _():
        def prefetch(s):
            idx_copy(i + 1, 1 - s).wait()
            issue_rows(1 - s)

            @pl.when(i + 2 < n_steps)
            def _():
                idx_copy(i + 2, s).start()

        _on_slot(slot, prefetch)


def _gather_wait(s, src_hbm, buf, gsem):
    pltpu.make_async_copy(src_hbm.at[pl.ds(0, buf[s].shape[0])], buf[s], gsem.at[s]).wait()


DISPATCH_SLOTS = 3


def _dispatch_kernel(pad_start_ref, pad_len_ref, idx_hbm, x1_hbm, rows_hbm, idx0, idx1, idx2, st0, st1, st2,
                     zero_ref, isem, xsem, dsem, zsem):
    i = pl.program_id(0)
    n_steps = pl.num_programs(0)
    idx = (idx0, idx1, idx2)
    stage = (st0, st1, st2)
    n_idx = idx0.shape[0]
    tm = n_idx // MOE_TOP_K

    def loads(step, s):
        first = pl.multiple_of(step * (tm * ROW_TILES), tm * ROW_TILES)
        return (pltpu.make_async_copy(idx_hbm.at[step], idx[s], isem.at[s]),
                pltpu.make_async_copy(x1_hbm.at[pl.ds(first, tm * ROW_TILES)], stage[s], xsem.at[s]))

    def start_loads(step, s):
        for c in loads(step, s):
            c.start()

    def wait_rows(s):
        pltpu.make_async_copy(x1_hbm.at[pl.ds(0, n_idx * ROW_TILES)], rows_hbm.at[pl.ds(0, n_idx * ROW_TILES)],
                              dsem.at[s]).wait()

    def zero_copy(row, n_rows):
        dst_row = pl.multiple_of(row * ROW_TILES, ROW_TILES)
        if n_rows is None:
            return pltpu.make_async_copy(zero_ref, rows_hbm.at[pl.ds(dst_row, ROW_TILES)], zsem.at[0])
        return pltpu.make_async_copy(x1_hbm.at[pl.ds(0, n_rows * ROW_TILES)],
                                     rows_hbm.at[pl.ds(dst_row, n_rows * ROW_TILES)], zsem.at[0])

    @pl.when(i == 0)
    def _():
        start_loads(0, 0)

        @pl.when(n_steps > 1)
        def _():
            start_loads(1, 1)

        zero_ref[...] = jnp.zeros(zero_ref.shape, F32)

        def zero_expert(e, carry):
            def zero_row(k, c):
                zero_copy(pad_start_ref[e] + k, None).start()
                return c

            return lax.fori_loop(0, pad_len_ref[e], zero_row, carry)

        lax.fori_loop(0, pad_len_ref.shape[0], zero_expert, 0)

    @pl.when(i == n_steps - 1)
    def _():
        def wait_expert(e, carry):
            @pl.when(pad_len_ref[e] > 0)
            def _():
                zero_copy(pad_start_ref[e], pad_len_ref[e]).wait()

            return carry

        lax.fori_loop(0, pad_len_ref.shape[0], wait_expert, 0)

    def step(s):
        freed = (s + DISPATCH_SLOTS - 1) % DISPATCH_SLOTS
        for c in loads(i, s):
            c.wait()

        def body(it, carry):
            for u in range(GATHER_UNROLL):
                r = it * GATHER_UNROLL + u
                src_row = pl.multiple_of((r % tm) * ROW_TILES, ROW_TILES)
                dst_row = pl.multiple_of(idx[s][r], ROW_TILES)
                pltpu.make_async_copy(stage[s].at[pl.ds(src_row, ROW_TILES)],
                                      rows_hbm.at[pl.ds(dst_row, ROW_TILES)], dsem.at[s]).start(priority=u % 2)
            return carry

        lax.fori_loop(0, n_idx // GATHER_UNROLL, body, 0)

        @pl.when(i > 0)
        def _():
            wait_rows(freed)

        @pl.when(i + 2 < n_steps)
        def _():
            start_loads(i + 2, freed)

        @pl.when(i == n_steps - 1)
        def _():
            wait_rows(s)

    _on_slot(i % DISPATCH_SLOTS, step, DISPATCH_SLOTS)


def _moe_dispatch(x1, dest_blocks, pad_start, pad_len, n_rows):
    n_steps, n_idx = dest_blocks.shape
    any_spec = pl.BlockSpec(memory_space=pl.ANY)
    grid_spec = pltpu.PrefetchScalarGridSpec(
        num_scalar_prefetch=2,
        grid=(n_steps,),
        in_specs=[any_spec, any_spec],
        out_specs=any_spec,
        scratch_shapes=(
            [pltpu.SMEM((n_idx,), jnp.int32)] * DISPATCH_SLOTS
            + [pltpu.VMEM((n_idx // MOE_TOP_K * ROW_TILES, LANES), F32)] * DISPATCH_SLOTS
            + [pltpu.VMEM((ROW_TILES, LANES), F32)]
            + [pltpu.SemaphoreType.DMA((DISPATCH_SLOTS,))] * 3
            + [pltpu.SemaphoreType.DMA((1,))]
        ),
    )
    return pl.pallas_call(
        _dispatch_kernel,
        out_shape=jax.ShapeDtypeStruct((n_rows * ROW_TILES, LANES), F32),
        grid_spec=grid_spec,
        compiler_params=_params("arbitrary"),
        name="moe_dispatch",
    )(pad_start, pad_len, dest_blocks, x1)


def _moe_kernel(be_ref, nu_ref, x_ref, wg_ref, wu_ref, wd_ref, y_ref, wgb_ref, wub_ref, wdb_ref):
    i = pl.program_id(0)

    @pl.when(i < nu_ref[0])
    def _():
        @pl.when((i == 0) | (be_ref[i] != be_ref[jnp.maximum(i - 1, 0)]))
        def _():
            wgb_ref[...] = wg_ref[...].astype(BF16)
            wub_ref[...] = wu_ref[...].astype(BF16)
            wdb_ref[...] = wd_ref[...].astype(BF16)

        xb = _load_row_tiles(x_ref, 0, MOE_BLOCK).astype(BF16)
        gate = jnp.dot(xb, wgb_ref[...], preferred_element_type=F32)
        up = jnp.dot(xb, wub_ref[...], preferred_element_type=F32)
        hmid = (_silu(gate) * up).astype(BF16)
        _store_row_tiles(y_ref, jnp.dot(hmid, wdb_ref[...], preferred_element_type=F32))

    @pl.when(i >= nu_ref[0])
    def _():
        y_ref[...] = jnp.zeros(y_ref.shape, F32)


def _moe_experts(x_rows, block_e, n_used, w_gate, w_up, w_down):
    n_blocks = block_e.shape[0]
    grid_spec = pltpu.PrefetchScalarGridSpec(
        num_scalar_prefetch=2,
        grid=(n_blocks,),
        in_specs=[
            pl.BlockSpec((MOE_BLOCK * ROW_TILES, LANES), lambda i, be, nu: (jnp.minimum(i, nu[0] - 1), 0)),
            pl.BlockSpec((None, D_MODEL, MOE_FF), lambda i, be, nu: (be[i], 0, 0)),
            pl.BlockSpec((None, D_MODEL, MOE_FF), lambda i, be, nu: (be[i], 0, 0)),
            pl.BlockSpec((None, MOE_FF, D_MODEL), lambda i, be, nu: (be[i], 0, 0)),
        ],
        out_specs=pl.BlockSpec((MOE_BLOCK * ROW_TILES, LANES), lambda i, be, nu: (i, 0)),
        scratch_shapes=[
            pltpu.VMEM((D_MODEL, MOE_FF), BF16),
            pltpu.VMEM((D_MODEL, MOE_FF), BF16),
            pltpu.VMEM((MOE_FF, D_MODEL), BF16),
        ],
    )
    return pl.pallas_call(
        _moe_kernel,
        out_shape=jax.ShapeDtypeStruct((n_blocks * MOE_BLOCK * ROW_TILES, LANES), F32),
        grid_spec=grid_spec,
        compiler_params=_params("arbitrary"),
        name="moe_experts",
    )(block_e, n_used, x_rows, w_gate, w_up, w_down)


COMB_TM = 256


def _combine_kernel(idx_hbm, y_hbm, ew_ref, x1_ref, p_ref, lg_ref, lb_ref, wpg_ref, wpp_ref, out_ref,
                    idx0, idx1, ybuf0, ybuf1, ffn_ref, isem, gsem):
    i = pl.program_id(0)
    tm = out_ref.shape[0]
    ybuf = (ybuf0, ybuf1)
    _gather_pipeline(i, pl.num_programs(0), idx_hbm, y_hbm, (idx0, idx1), ybuf, isem, gsem)

    def weighted_sum(s):
        _gather_wait(s, y_hbm, ybuf, gsem)
        ew = ew_ref[...]
        ffn_ref[...] = (_load_row_tiles(ybuf[s], 0, tm) * ew[:, 0:1]
                        + _load_row_tiles(ybuf[s], tm, tm) * ew[:, 1:2])

    _on_slot(i % 2, weighted_sum)
    x1 = _load_row_tiles(x1_ref, 0, tm)
    x2 = _layer_norm(DEEPNORM_ALPHA * x1 + ffn_ref[...], lg_ref[...], lb_ref[...])
    gate = _sigmoid(jnp.dot(x2.astype(BF16), wpg_ref[...], preferred_element_type=F32))
    emb = jnp.dot(p_ref[...].astype(BF16), wpp_ref[...], preferred_element_type=F32)
    out_ref[...] = x2 + gate * emb


def _combine(y_rows, dest_blocks, e_w, x1, p2d, ln_g, ln_b, ple_w_gate, ple_w_proj):
    t = x1.shape[0] // ROW_TILES
    tm = dest_blocks.shape[1] // MOE_TOP_K
    const = lambda i: (0, 0)
    row = lambda i: (i, 0)
    return pl.pallas_call(
        _combine_kernel,
        out_shape=jax.ShapeDtypeStruct((t, D_MODEL), F32),
        grid=(t // tm,),
        in_specs=[
            pl.BlockSpec(memory_space=pl.ANY),
            pl.BlockSpec(memory_space=pl.ANY),
            pl.BlockSpec((tm, MOE_TOP_K), row),
            pl.BlockSpec((tm * ROW_TILES, LANES), row),
            pl.BlockSpec((tm, PLE_DIM), row),
            pl.BlockSpec((1, D_MODEL), const),
            pl.BlockSpec((1, D_MODEL), const),
            pl.BlockSpec((D_MODEL, D_MODEL), const),
            pl.BlockSpec((PLE_DIM, D_MODEL), const),
        ],
        out_specs=pl.BlockSpec((tm, D_MODEL), row),
        scratch_shapes=[
            pltpu.SMEM((MOE_TOP_K * tm,), jnp.int32),
            pltpu.SMEM((MOE_TOP_K * tm,), jnp.int32),
            pltpu.VMEM((MOE_TOP_K * tm * ROW_TILES, LANES), F32),
            pltpu.VMEM((MOE_TOP_K * tm * ROW_TILES, LANES), F32),
            pltpu.VMEM((tm, D_MODEL), F32),
            pltpu.SemaphoreType.DMA((2,)),
            pltpu.SemaphoreType.DMA((2,)),
        ],
        compiler_params=_params("arbitrary"),
        name="moe_combine",
    )(dest_blocks, y_rows, e_w, x1, p2d, ln_g[None, :], ln_b[None, :], ple_w_gate.astype(BF16),
      ple_w_proj.astype(BF16))


ROUTE_CHUNK = 256


def _route(logits):
    t = logits.shape[0]
    g_logits = logits[:, :MOE_GROUPS]
    e_logits = logits[:, MOE_GROUPS:MOE_GROUPS + MOE_EXPERTS]
    g_idx = jnp.argmax(g_logits, axis=-1).astype(jnp.int32)
    g_w = 1.0 / jnp.sum(jnp.exp(g_logits - jnp.max(g_logits, axis=-1, keepdims=True)), axis=-1)

    lane = jnp.arange(MOE_EXPERTS, dtype=jnp.int32)[None, :]
    masked = jnp.where(lane // MOE_EXPERTS_PER_GROUP == g_idx[:, None], e_logits, -jnp.inf)
    e1 = jnp.argmax(masked, axis=-1).astype(jnp.int32)
    sel1 = lane == e1[:, None]
    rest = jnp.where(sel1, -jnp.inf, masked)
    e2 = jnp.argmax(rest, axis=-1).astype(jnp.int32)
    sel2 = lane == e2[:, None]
    r = jnp.exp(jnp.max(rest, axis=-1) - jnp.max(masked, axis=-1))
    e_w = jnp.stack([1.0 / (1.0 + r), r / (1.0 + r)], axis=-1) * g_w[:, None]

    chunk = min(ROUTE_CHUNK, t)
    hot = (sel1 | sel2).reshape(t // chunk, chunk, MOE_EXPERTS)
    tril = jnp.tril(jnp.ones((chunk, chunk), BF16))
    within = jnp.einsum("ij,cje->cie", tril, hot.astype(BF16), preferred_element_type=F32).astype(jnp.int32)
    chunk_counts = within[:, -1, :]
    chunk_ends = jnp.cumsum(chunk_counts, axis=0)
    counts = chunk_ends[-1]
    padded = ((counts + MOE_BLOCK - 1) // MOE_BLOCK) * MOE_BLOCK
    pends = jnp.cumsum(padded)
    pstarts = pends - padded
    row = (within + (chunk_ends - chunk_counts + pstarts[None, :] - 1)[:, None, :]).reshape(t, MOE_EXPERTS)
    dest = (jnp.sum(jnp.where(sel1, row, 0), axis=-1), jnp.sum(jnp.where(sel2, row, 0), axis=-1))

    n_rows = t * MOE_TOP_K + MOE_EXPERTS * MOE_BLOCK
    n_blocks = n_rows // MOE_BLOCK
    block_start = jnp.arange(n_blocks, dtype=jnp.int32) * MOE_BLOCK
    block_e = jnp.sum((pends[None, :] <= block_start[:, None]).astype(jnp.int32), axis=1)
    block_e = jnp.minimum(block_e, MOE_EXPERTS - 1).astype(jnp.int32)
    n_used = (pends[-1] // MOE_BLOCK).astype(jnp.int32).reshape(1)
    pad = (jnp.concatenate([pstarts + counts, pends[-1:]]),
           jnp.concatenate([padded - counts, n_rows - pends[-1:]]))
    return e_w, dest, block_e, n_used, pad, n_rows


def _layer(x2d, p2d, cos_t, sin_t, bsz, seq, w_in, b_gate, ssd_conv_w, ssd_conv_b, ssd_dt_bias, ssd_a_log,
           ssd_d, ssd_norm_w, ssd_w_out, conf_dw_w, conf_dw_b, conf_ln_g, conf_ln_b, conf_w_out, attn_sinks,
           attn_w_out, w_out, ln1_g, ln1_b, moe_w_group, moe_b_group, moe_w_expert, moe_b_expert, moe_w_gate,
           moe_w_up, moe_w_down, ln2_g, ln2_b, ple_w_gate, ple_w_proj):
    t = bsz * seq
    wb = w_in.astype(BF16)
    after_dt = DT_COL_ORIG + SSD_HEADS
    qk_width = COL_V - COL_Q
    half = ATT_HEAD_DIM // 2
    qk = wb[:, after_dt + COL_Q - COL_U:after_dt + COL_V - COL_U]
    qk = qk.reshape(D_MODEL, qk_width // LANES, 2, 2, half).transpose(0, 1, 3, 2, 4).reshape(D_MODEL, qk_width)
    w_main = jnp.concatenate([wb[:, :DT_COL_ORIG], wb[:, after_dt:after_dt + COL_Q - COL_U], qk,
                              wb[:, after_dt + COL_V - COL_U:]], axis=1)
    w_dt = jnp.pad(wb[:, DT_COL_ORIG:after_dt], ((0, 0), (0, LANES - SSD_HEADS)))
    h, dt_raw = _in_projection(x2d, w_main, w_dt)

    y_ssd = _ssd_mixer(h, dt_raw, bsz, seq, ssd_conv_w, ssd_conv_b, ssd_dt_bias, ssd_a_log, ssd_d, ssd_norm_w,
                       ssd_w_out)
    y_conf = _conformer(h, bsz, seq, conf_dw_w, conf_dw_b, conf_ln_g, conf_ln_b, conf_w_out)
    y_att = _attention(h, cos_t, sin_t, bsz, seq, attn_sinks, attn_w_out)

    n_router = MOE_GROUPS + MOE_EXPERTS
    w_router = jnp.pad(jnp.concatenate([moe_w_group, moe_w_expert], axis=1), ((0, 0), (0, LANES - n_router)))
    b_router = jnp.pad(jnp.concatenate([moe_b_group, moe_b_expert]), (0, LANES - n_router))[None, :]
    x1, logits = _merge(h, b_gate, y_ssd, y_conf, y_att, x2d, w_out, ln1_g, ln1_b, w_router, b_router)

    e_w, dest, block_e, n_used, (pad_start, pad_len), n_rows = _route(logits)
    tm = min(COMB_TM, t)
    dest_blocks = jnp.concatenate([d.reshape(t // tm, tm) for d in dest], axis=1) * ROW_TILES
    x_rows = _moe_dispatch(x1, dest_blocks, pad_start, pad_len, n_rows)
    y_rows = _moe_experts(x_rows, block_e, n_used, moe_w_gate, moe_w_up, moe_w_down)
    return _combine(y_rows, dest_blocks, e_w, x1, p2d, ln2_g, ln2_b, ple_w_gate, ple_w_proj)


def kernel(x, p, positions, w_in, b_gate, ssd_conv_w, ssd_conv_b, ssd_dt_bias, ssd_a_log, ssd_d, ssd_norm_w, ssd_w_out, conf_dw_w, conf_dw_b, conf_ln_g, conf_ln_b, conf_w_out, attn_sinks, attn_w_out, w_out, ln1_g, ln1_b, moe_w_group, moe_b_group, moe_w_expert, moe_b_expert, moe_w_gate, moe_w_up, moe_w_down, ln2_g, ln2_b, ple_w_gate, ple_w_proj):
    bsz, seq, d = x.shape
    t = bsz * seq
    cos_t, sin_t = _rope_tables(positions)
    x2d = x.reshape(t, d)
    per_layer = (w_in, b_gate, ssd_conv_w, ssd_conv_b, ssd_dt_bias, ssd_a_log, ssd_d, ssd_norm_w, ssd_w_out,
                 conf_dw_w, conf_dw_b, conf_ln_g, conf_ln_b, conf_w_out, attn_sinks, attn_w_out, w_out, ln1_g,
                 ln1_b, moe_w_group, moe_b_group, moe_w_expert, moe_b_expert, moe_w_gate, moe_w_up, moe_w_down,
                 ln2_g, ln2_b, ple_w_gate, ple_w_proj)
    for layer in range(w_in.shape[0]):
        x2d = _layer(x2d, p[layer].reshape(t, -1), cos_t, sin_t, bsz, seq, *(w[layer] for w in per_layer))
    return x2d.reshape(bsz, seq, d)
```
